```python
import math
import jax, jax.numpy as jnp
from jax import lax
import numpy as np

D_MODEL = 1024
BATCH = 8
SEQ = 4096
DEPTH = 2

HEAD_DIM = 64
N_HEADS = D_MODEL // HEAD_DIM
N_KV = 4
HPG = N_HEADS // N_KV
Q_WIDTH = N_HEADS * HEAD_DIM
KV_WIDTH = N_KV * HEAD_DIM
CMP_LEN = 32
CMP_STRIDE = 16
CMP_HIDDEN = 256
SLC_LEN = 64
N_SELECT = 16
NSA_WINDOW = 512
NSA_QBLK = 64
NSA_IN = Q_WIDTH + 6 * KV_WIDTH + 3 * N_HEADS
NSA_SPLITS = (Q_WIDTH, Q_WIDTH + KV_WIDTH, Q_WIDTH + 2 * KV_WIDTH, Q_WIDTH + 3 * KV_WIDTH,
              Q_WIDTH + 4 * KV_WIDTH, Q_WIDTH + 5 * KV_WIDTH, Q_WIDTH + 6 * KV_WIDTH)
SWA_WINDOW = 128
SWA_QBLK = 128
N_BUCKETS = 32
MAX_DISTANCE = 128
N_EXPERTS = 16
N_GROUPS = 4
EXPERTS_PER_GROUP = N_EXPERTS // N_GROUPS
TOP_K = 2
D_EXPERT = 256
N_A_LAYERS = max(1, DEPTH // 2)
N_B_LAYERS = DEPTH - N_A_LAYERS
ALPHA = (2.0 * DEPTH) ** 0.25
BETA = (8.0 * DEPTH) ** -0.25
LN_EPS = 1e-5
NEG = -1e30
ATTN_SCALE = HEAD_DIM ** -0.5

kernel_name = 'nsa_swa_sink_yoco_grouped_moe'


def layer_norm(x, g, b):
    xf = x.astype(jnp.float32)
    mu = xf.mean(-1, keepdims=True)
    var = jnp.mean(jnp.square(xf - mu), -1, keepdims=True)
    return ((xf - mu) * lax.rsqrt(var + LN_EPS) * g.astype(jnp.float32)
            + b.astype(jnp.float32)).astype(x.dtype)


def modulate(x, shift, scale):
    return x * (1.0 + scale[:, None, :]) + shift[:, None, :]


def rel_bucket(dist):
    max_exact = N_BUCKETS // 2
    d = jnp.maximum(dist, 0)
    large = max_exact + (jnp.log(jnp.maximum(d, 1).astype(jnp.float32) / max_exact)
                         / math.log(MAX_DISTANCE / max_exact)
                         * (N_BUCKETS - max_exact)).astype(jnp.int32)
    large = jnp.minimum(large, N_BUCKETS - 1)
    return jnp.where(d < max_exact, d, large)


def masked_softmax(logits, valid):
    p = jax.nn.softmax(jnp.where(valid, logits, NEG), axis=-1)
    return jnp.where(valid, p, 0.0)


def compress_blocks(k, pos, w1, w2):
    B, G, S, dk = k.shape
    chunks = k.reshape(B, G, S // CMP_STRIDE, CMP_STRIDE, dk)
    blocks = jnp.concatenate([chunks[:, :, :-1], chunks[:, :, 1:]], axis=3) + pos
    blocks = blocks.reshape(B, G, S // CMP_STRIDE - 1, CMP_LEN * dk)
    return jax.nn.gelu(blocks @ w1) @ w2


def band_bias(rel_table, qblk, window):
    n_k = qblk + window
    dist = jnp.arange(qblk)[:, None] + window - jnp.arange(n_k)[None, :]
    band = (dist >= 0) & (dist < window)
    bias = rel_table[rel_bucket(dist)].astype(jnp.float32)
    bias = bias.reshape(qblk, n_k, N_KV, HPG).transpose(2, 3, 0, 1)
    return dist, band, bias


def nsa_mixer(h, w_in, k_w1, k_w2, v_w1, v_w2, k_pos, v_pos, w_out, rel_table):
    B, S, _ = h.shape
    G, J, dk = N_KV, HPG, HEAD_DIM
    proj = h @ w_in
    q, kc, vc, ksl, vsl, kw, vw, gl = jnp.split(proj, NSA_SPLITS, axis=-1)
    q = q.reshape(B, S, G, J, dk).transpose(0, 2, 3, 1, 4)
    to_kv = lambda t: t.reshape(B, S, G, dk).transpose(0, 2, 1, 3)
    kc, vc, ksl, vsl, kw, vw = [to_kv(t) for t in (kc, vc, ksl, vsl, kw, vw)]
    gates = jax.nn.sigmoid(gl.astype(jnp.float32)).reshape(B, S, 3, G, J).transpose(0, 3, 4, 1, 2)

    k_cmp = compress_blocks(kc, k_pos, k_w1, k_w2)
    v_cmp = compress_blocks(vc, v_pos, v_w1, v_w2)
    n_cmp = k_cmp.shape[2]
    cmp_end = jnp.arange(n_cmp) * CMP_STRIDE + CMP_LEN - 1
    n_slc = S // SLC_LEN
    n_sel = min(N_SELECT, n_slc)
    c_start = jnp.arange(n_cmp)[:, None] * CMP_STRIDE
    s_start = jnp.arange(n_slc)[None, :] * SLC_LEN
    overlap = ((c_start < s_start + SLC_LEN) & (c_start + CMP_LEN > s_start)).astype(jnp.float32)
    k_blk = ksl.reshape(B, G, n_slc, SLC_LEN, dk)
    v_blk = vsl.reshape(B, G, n_slc, SLC_LEN, dk)
    blk_off = jnp.arange(SLC_LEN)
    b_idx = jnp.arange(B)[:, None, None, None]
    g_idx = jnp.arange(G)[None, :, None, None]
    table_g = rel_table.astype(jnp.float32).reshape(N_BUCKETS, G, J).transpose(1, 0, 2)
    jblk = jnp.arange(n_slc)
    kw_pad = jnp.pad(kw, ((0, 0), (0, 0), (NSA_WINDOW, 0), (0, 0)))
    vw_pad = jnp.pad(vw, ((0, 0), (0, 0), (NSA_WINDOW, 0), (0, 0)))
    _, win_band, win_bias = band_bias(rel_table, NSA_QBLK, NSA_WINDOW)
    n_kw = NSA_QBLK + NSA_WINDOW

    def q_block(qb):
        s0 = qb * NSA_QBLK
        tq = s0 + jnp.arange(NSA_QBLK)
        qblk = lax.dynamic_slice_in_dim(q, s0, NSA_QBLK, axis=3)

        dist_c = tq[:, None] - cmp_end[None, :]
        bias_c = rel_table[rel_bucket(dist_c)].astype(jnp.float32)
        bias_c = bias_c.reshape(NSA_QBLK, n_cmp, G, J).transpose(2, 3, 0, 1)
        lg_c = jnp.einsum('bghqd,bgnd->bghqn', qblk, k_cmp).astype(jnp.float32) * ATTN_SCALE + bias_c
        p_c = masked_softmax(lg_c, dist_c >= 0)
        o_c = jnp.einsum('bghqn,bgnd->bghqd', p_c.astype(v_cmp.dtype), v_cmp)

        imp = jnp.einsum('bghqn,nj->bgqj', p_c, overlap)
        cur = (tq // SLC_LEN)[:, None]
        forced = (jblk[None, :] == 0) | (jblk[None, :] == cur) | (jblk[None, :] == cur - 1)
        future = jblk[None, :] > cur
        imp = jnp.where(forced, 1e9, jnp.where(future, -1e9, imp))
        _, idx = lax.top_k(imp, n_sel)
        n_ks = n_sel * SLC_LEN
        k_sel = k_blk[b_idx, g_idx, idx].reshape(B, G, NSA_QBLK, n_ks, dk)
        v_sel = v_blk[b_idx, g_idx, idx].reshape(B, G, NSA_QBLK, n_ks, dk)
        kpos_s = (idx[..., None] * SLC_LEN + blk_off).reshape(B, G, NSA_QBLK, n_ks)
        dist_s = tq[None, None, :, None] - kpos_s
        bias_s = table_g[g_idx, rel_bucket(dist_s)].transpose(0, 1, 4, 2, 3)
        lg_s = jnp.einsum('bghqd,bgqkd->bghqk', qblk, k_sel).astype(jnp.float32) * ATTN_SCALE + bias_s
        p_s = masked_softmax(lg_s, (dist_s >= 0)[:, :, None])
        o_s = jnp.einsum('bghqk,bgqkd->bghqd', p_s.astype(v_sel.dtype), v_sel)

        kb = lax.dynamic_slice_in_dim(kw_pad, s0, n_kw, axis=2)
        vb = lax.dynamic_slice_in_dim(vw_pad, s0, n_kw, axis=2)
        kpos_w = s0 - NSA_WINDOW + jnp.arange(n_kw)
        valid_w = win_band & (kpos_w >= 0)[None, :]
        lg_w = jnp.einsum('bghqd,bgkd->bghqk', qblk, kb).astype(jnp.float32) * ATTN_SCALE + win_bias
        p_w = masked_softmax(lg_w, valid_w)
        o_w = jnp.einsum('bghqk,bgkd->bghqd', p_w.astype(vb.dtype), vb)

        g = lax.dynamic_slice_in_dim(gates, s0, NSA_QBLK, axis=3).astype(o_c.dtype)
        return g[..., 0:1] * o_c + g[..., 1:2] * o_s + g[..., 2:3] * o_w

    o = lax.map(q_block, jnp.arange(S // NSA_QBLK))
    o = o.transpose(1, 0, 4, 2, 3, 5).reshape(B, S, Q_WIDTH)
    return o @ w_out


def swa_sink_mixer(h, w_q, k_sh, v_sh, sinks, w_out, rel_table):
    B, S, _ = h.shape
    G, J, dk = N_KV, HPG, HEAD_DIM
    q = (h @ w_q).reshape(B, S, G, J, dk).transpose(0, 2, 3, 1, 4)
    k_pad = jnp.pad(k_sh, ((0, 0), (0, 0), (SWA_WINDOW, 0), (0, 0)))
    v_pad = jnp.pad(v_sh, ((0, 0), (0, 0), (SWA_WINDOW, 0), (0, 0)))
    _, band, bias = band_bias(rel_table, SWA_QBLK, SWA_WINDOW)
    n_k = SWA_QBLK + SWA_WINDOW
    sink = jnp.broadcast_to(sinks.astype(jnp.float32).reshape(1, G, J, 1, 1), (B, G, J, SWA_QBLK, 1))

    def q_block(qb):
        s0 = qb * SWA_QBLK
        qblk = lax.dynamic_slice_in_dim(q, s0, SWA_QBLK, axis=3)
        kb = lax.dynamic_slice_in_dim(k_pad, s0, n_k, axis=2)
        vb = lax.dynamic_slice_in_dim(v_pad, s0, n_k, axis=2)
        kpos = s0 - SWA_WINDOW + jnp.arange(n_k)
        valid = band & (kpos >= 0)[None, :]
        lg = jnp.einsum('bghqd,bgkd->bghqk', qblk, kb).astype(jnp.float32) * ATTN_SCALE + bias
        lg = jnp.where(valid, lg, NEG)
        p = jax.nn.softmax(jnp.concatenate([lg, sink], axis=-1), axis=-1)[..., :-1]
        return jnp.einsum('bghqk,bgkd->bghqd', p.astype(vb.dtype), vb)

    o = lax.map(q_block, jnp.arange(S // SWA_QBLK))
    o = o.transpose(1, 0, 4, 2, 3, 5).reshape(B, S, Q_WIDTH)
    return o @ w_out


def grouped_moe(h, router_w, router_b, w_gate, w_up, w_down):
    B, S, D = h.shape
    T = B * S
    t = h.reshape(T, D)
    s = jax.nn.sigmoid((t @ router_w).astype(jnp.float32))
    sb = s + router_b.astype(jnp.float32)
    grp_score = lax.top_k(sb.reshape(T, N_GROUPS, EXPERTS_PER_GROUP), TOP_K)[0].sum(-1)
    grp = jnp.argmax(grp_score, axis=-1)
    in_grp = (jnp.arange(N_EXPERTS) // EXPERTS_PER_GROUP)[None, :] == grp[:, None]
    _, e_idx = lax.top_k(jnp.where(in_grp, sb, -jnp.inf), TOP_K)
    s_sel = jnp.take_along_axis(s, e_idx, axis=-1)
    w = s_sel / s_sel.sum(-1, keepdims=True)
    gates = jnp.einsum('tk,tke->te', w, jax.nn.one_hot(e_idx, N_EXPERTS, dtype=jnp.float32)).astype(t.dtype)
    out = jnp.zeros_like(t)
    for e in range(N_EXPERTS):
        he = jax.nn.silu(t @ w_gate[e]) * (t @ w_up[e])
        out = out + gates[:, e:e + 1] * (he @ w_down[e])
    return out.reshape(B, S, D)


def setup_inputs(seed: int = 0) -> dict:
    key = jax.random.key(seed)
    ks = jax.random.split(key, 24)
    D = D_MODEL
    nrm = lambda k, shape, s: jax.random.normal(k, shape, jnp.float32) * s
    return {
        'x': nrm(ks[0], (BATCH, SEQ, D), 1.0),
        'c': nrm(ks[1], (BATCH, D), 1.0),
        'nsa_w_in': nrm(ks[2], (N_A_LAYERS, D, NSA_IN), D ** -0.5),
        'cmp_k_w1': nrm(ks[3], (N_A_LAYERS, CMP_LEN * HEAD_DIM, CMP_HIDDEN), (CMP_LEN * HEAD_DIM) ** -0.5),
        'cmp_k_w2': nrm(ks[4], (N_A_LAYERS, CMP_HIDDEN, HEAD_DIM), CMP_HIDDEN ** -0.5),
        'cmp_v_w1': nrm(ks[5], (N_A_LAYERS, CMP_LEN * HEAD_DIM, CMP_HIDDEN), (CMP_LEN * HEAD_DIM) ** -0.5),
        'cmp_v_w2': nrm(ks[6], (N_A_LAYERS, CMP_HIDDEN, HEAD_DIM), CMP_HIDDEN ** -0.5),
        'cmp_k_pos': nrm(ks[7], (N_A_LAYERS, CMP_LEN, HEAD_DIM), 0.1),
        'cmp_v_pos': nrm(ks[8], (N_A_LAYERS, CMP_LEN, HEAD_DIM), 0.1),
        'nsa_w_out': nrm(ks[9], (N_A_LAYERS, Q_WIDTH, D), Q_WIDTH ** -0.5 * BETA),
        'swa_w_q': nrm(ks[10], (N_B_LAYERS, D, Q_WIDTH), D ** -0.5),
        'swa_sinks': nrm(ks[11], (N_B_LAYERS, N_HEADS), 0.5),
        'swa_w_out': nrm(ks[12], (N_B_LAYERS, Q_WIDTH, D), Q_WIDTH ** -0.5 * BETA),
        'shared_w_kv': nrm(ks[13], (D, 2 * KV_WIDTH), D ** -0.5),
        'rel_bias': nrm(ks[14], (N_BUCKETS, N_HEADS), 0.5),
        'router_w': nrm(ks[15], (D, N_EXPERTS), D ** -0.5),
        'router_b': nrm(ks[16], (N_EXPERTS,), 0.01),
        'moe_w_gate': nrm(ks[17], (DEPTH, N_EXPERTS, D, D_EXPERT), D ** -0.5),
        'moe_w_up': nrm(ks[18], (DEPTH, N_EXPERTS, D, D_EXPERT), D ** -0.5),
        'moe_w_down': nrm(ks[19], (DEPTH, N_EXPERTS, D_EXPERT, D), D_EXPERT ** -0.5 * BETA),
        'ada_w': nrm(ks[20], (DEPTH, D, 6 * D), 0.1 * D ** -0.5),
        'ada_b': nrm(ks[21], (DEPTH, 6 * D), 0.02),
        'ln_g': 1.0 + nrm(ks[22], (DEPTH, 2, D), 0.02),
        'ln_b': nrm(ks[23], (DEPTH, 2, D), 0.02),
    }


def reference(x, c, nsa_w_in, cmp_k_w1, cmp_k_w2, cmp_v_w1, cmp_v_w2, cmp_k_pos, cmp_v_pos,
              nsa_w_out, swa_w_q, swa_sinks, swa_w_out, shared_w_kv, rel_bias, router_w, router_b,
              moe_w_gate, moe_w_up, moe_w_down, ada_w, ada_b, ln_g, ln_b):
    B, S, _ = x.shape
    cond = jax.nn.silu(c)
    k_sh = None
    v_sh = None
    for layer in range(DEPTH):
        ada = cond @ ada_w[layer] + ada_b[layer]
        sh_a, sc_a, g_a, sh_f, sc_f, g_f = jnp.split(ada, 6, axis=-1)
        h = modulate(x, sh_a, sc_a)
        if layer < N_A_LAYERS:
            mix = nsa_mixer(h, nsa_w_in[layer], cmp_k_w1[layer], cmp_k_w2[layer], cmp_v_w1[layer],
                            cmp_v_w2[layer], cmp_k_pos[layer], cmp_v_pos[layer], nsa_w_out[layer], rel_bias)
        else:
            if layer == N_A_LAYERS:
                kv = (x @ shared_w_kv).reshape(B, S, 2, N_KV, HEAD_DIM).transpose(2, 0, 3, 1, 4)
                k_sh, v_sh = kv[0], kv[1]
            lb = layer - N_A_LAYERS
            mix = swa_sink_mixer(h, swa_w_q[lb], k_sh, v_sh, swa_sinks[lb], swa_w_out[lb], rel_bias)
        x = layer_norm(ALPHA * x + (1.0 + g_a)[:, None, :] * mix, ln_g[layer, 0], ln_b[layer, 0])
        h = modulate(x, sh_f, sc_f)
        ffn = grouped_moe(h, router_w, router_b, moe_w_gate[layer], moe_w_up[layer], moe_w_down[layer])
        x = layer_norm(ALPHA * x + (1.0 + g_f)[:, None, :] * ffn, ln_g[layer, 1], ln_b[layer, 1])
    return x
```

```python
import functools
import math

import numpy as np
import jax
import jax.numpy as jnp
from jax import lax
from jax.experimental import pallas as pl
from jax.experimental.pallas import tpu as pltpu

F32 = jnp.float32
BF16 = jnp.bfloat16

D_MODEL = 1024
HEAD_DIM = 64
N_HEADS = 16
N_KV = 4
HPG = 4
Q_WIDTH = N_HEADS * HEAD_DIM
KV_WIDTH = N_KV * HEAD_DIM
CMP_LEN = 32
CMP_STRIDE = 16
CMP_HIDDEN = 256
SLC_LEN = 64
N_SELECT = 16
NSA_WINDOW = 512
SWA_WINDOW = 128
N_BUCKETS = 32
MAX_DISTANCE = 128
N_EXPERTS = 16
N_GROUPS = 4
EXPERTS_PER_GROUP = 4
D_EXPERT = 256
DEPTH = 2
ALPHA = (2.0 * DEPTH) ** 0.25
LN_EPS = 1e-5
NEG = -1e30
ATTN_SCALE = HEAD_DIM ** -0.5

QB = 128
LQ = HPG * QB
AUG = 128
CMP_BAND = 24
FAR_TK = 512
SEL_PAD = 128
VMEM_LIMIT = 48 * 1024 * 1024


def _cparams(sem):
    return pltpu.CompilerParams(dimension_semantics=sem, vmem_limit_bytes=VMEM_LIMIT)


def _ada_kernel(c_ref, w_ref, b_ref, o_ref):
    c = c_ref[...]
    cond = c * jax.nn.sigmoid(c)
    o_ref[0] = jnp.dot(cond, w_ref[0], preferred_element_type=F32,
                       precision=lax.Precision.HIGHEST) + b_ref[0]


def _ada(c, ada_w, ada_b):
    B, D = c.shape
    n6 = ada_w.shape[-1]
    tn = 1536
    return pl.pallas_call(
        _ada_kernel,
        grid=(DEPTH, n6 // tn),
        in_specs=[pl.BlockSpec((B, D), lambda l, n: (0, 0)),
                  pl.BlockSpec((1, D, tn), lambda l, n: (l, 0, n)),
                  pl.BlockSpec((1, 1, tn), lambda l, n: (l, 0, n))],
        out_specs=pl.BlockSpec((1, B, tn), lambda l, n: (l, 0, n)),
        out_shape=jax.ShapeDtypeStruct((DEPTH, B, n6), F32),
        compiler_params=_cparams(("arbitrary", "arbitrary")),
        name="ada",
    )(c, ada_w, ada_b.reshape(DEPTH, 1, n6))


def _modmm_kernel(x_ref, sh_ref, sc_ref, w_ref, o_ref, *, sigmoid):
    h = x_ref[0] * (1.0 + sc_ref[0]) + sh_ref[0]
    y = jnp.dot(h.astype(BF16), w_ref[...], preferred_element_type=F32)
    if sigmoid:
        y = jax.nn.sigmoid(y)
    o_ref[0] = y.astype(o_ref.dtype)


def _modmm(x, shift, scale, w, out_dtype, *, sigmoid=False, ts=512, tn=None, name="modmm"):
    B, S, D = x.shape
    N = w.shape[1]
    tn = N if tn is None else tn
    return pl.pallas_call(
        functools.partial(_modmm_kernel, sigmoid=sigmoid),
        grid=(B, S // ts, N // tn),
        in_specs=[pl.BlockSpec((1, ts, D), lambda b, s, n: (b, s, 0)),
                  pl.BlockSpec((1, 1, D), lambda b, s, n: (b, 0, 0)),
                  pl.BlockSpec((1, 1, D), lambda b, s, n: (b, 0, 0)),
                  pl.BlockSpec((D, tn), lambda b, s, n: (0, n))],
        out_specs=pl.BlockSpec((1, ts, tn), lambda b, s, n: (b, s, n)),
        out_shape=jax.ShapeDtypeStruct((B, S, N), out_dtype),
        compiler_params=_cparams(("arbitrary", "arbitrary", "arbitrary")),
        name=name,
    )(x, shift.reshape(B, 1, D), scale.reshape(B, 1, D), w)


def _gelu_tanh(x):
    return 0.5 * x * (1.0 + jnp.tanh(math.sqrt(2.0 / math.pi) * (x + 0.044715 * (x * x * x))))


def _compress_kernel(ch_ref, w1_ref, w2_ref, pos_ref, o_ref, *, nc):
    half = CMP_STRIDE * HEAD_DIM
    ch = ch_ref[0]
    w1 = w1_ref[...]
    top = jnp.dot(ch, w1[:half], preferred_element_type=F32)
    bot = jnp.dot(ch, w1[half:], preferred_element_type=F32)
    posw = jnp.dot(pos_ref[...], w1, preferred_element_type=F32)[0:1]
    w2 = w2_ref[...]
    for g in range(N_KV):
        bot_g = pltpu.roll(bot[g * nc:(g + 1) * nc], nc - 1, 0)
        pre = top[g * nc:(g + 1) * nc] + bot_g + posw
        o_ref[0, g] = jnp.dot(_gelu_tanh(pre).astype(BF16), w2, preferred_element_type=F32)


def _compress(chunks, w1, w2, pos):
    B, gn, ck = chunks.shape
    nc = gn // N_KV
    pos8 = jnp.broadcast_to(pos.reshape(1, CMP_LEN * HEAD_DIM), (8, CMP_LEN * HEAD_DIM)).astype(BF16)
    return pl.pallas_call(
        functools.partial(_compress_kernel, nc=nc),
        grid=(B,),
        in_specs=[pl.BlockSpec((1, gn, ck), lambda b: (b, 0, 0)),
                  pl.BlockSpec(w1.shape, lambda b: (0, 0)),
                  pl.BlockSpec(w2.shape, lambda b: (0, 0)),
                  pl.BlockSpec(pos8.shape, lambda b: (0, 0))],
        out_specs=pl.BlockSpec((1, N_KV, nc, HEAD_DIM), lambda b: (b, 0, 0, 0)),
        out_shape=jax.ShapeDtypeStruct((B, N_KV, nc, HEAD_DIM), F32),
        compiler_params=_cparams(("arbitrary",)),
        name="compress",
    )(chunks, w1.astype(BF16), w2.astype(BF16), pos8)


def _bucket_np(d):
    d = np.maximum(np.asarray(d, np.int64), 0)
    max_exact = N_BUCKETS // 2
    large = max_exact + (np.log(np.maximum(d, 1).astype(np.float32) / np.float32(max_exact))
                         / np.float32(math.log(MAX_DISTANCE / max_exact))
                         * np.float32(N_BUCKETS - max_exact)).astype(np.int32)
    large = np.minimum(large, N_BUCKETS - 1)
    return np.where(d < max_exact, d, large).astype(np.int32)


def _shifted_bias(rel_table):
    t = rel_table.astype(F32)
    return (t[_bucket_np(np.arange(MAX_DISTANCE))] - t[N_BUCKETS - 1][None, :]).T


def _to_group_lanes(m):
    H, K, Q = m.shape
    return m.reshape(N_KV, HPG, K, Q).transpose(0, 2, 1, 3).reshape(N_KV, K, HPG * Q)


def _near_band(fb, window):
    H = fb.shape[0]
    nv = 3 * QB - 1
    dist = np.arange(nv) - (QB - 1)
    ok = (dist >= 0) & (dist < window)
    idx = np.clip(dist, 0, MAX_DISTANCE - 1)
    far = dist >= MAX_DISTANCE
    v = jnp.where(ok[None, :], jnp.where(far[None, :], 0.0, fb[:, idx]), NEG)
    rows = 2 * QB
    flat = jnp.tile(v, (1, rows + 1))[:, :rows * (nv + 1)]
    m = flat.reshape(H, rows, nv + 1)[:, ::-1, :QB]
    return _to_group_lanes(m)


def _cmp_band(fb, nc):
    m = np.arange(CMP_BAND)[:, None]
    i = np.arange(QB)[None, :]
    dist = i + CMP_STRIDE * 16 - CMP_STRIDE * m - (CMP_LEN - 1)
    idx = np.clip(dist, 0, MAX_DISTANCE - 1)
    v = jnp.where((dist >= 0)[None], jnp.where((dist >= MAX_DISTANCE)[None], 0.0, fb[:, idx]), NEG)
    band = _to_group_lanes(v)
    return jnp.concatenate([jnp.zeros((N_KV, nc, LQ), F32), band,
                            jnp.full((N_KV, nc, LQ), NEG, F32)], axis=1)


def _softmax_cols(parts):
    m = functools.reduce(jnp.maximum, [jnp.max(p, axis=0, keepdims=True) for p in parts])
    return m, [jnp.exp(p - m) for p in parts]


def _nsa_kernel(q_ref, g_ref, kc_ref, vct_ref, ks_ref, vst_ref, kw_ref, vwt_ref, ovl_ref, cb_ref,
                band_ref, o_ref, acc_ref, *, nc):
    qb = pl.program_id(2)
    s0 = qb * QB
    qT = q_ref[0, 0, 0]
    row = lax.broadcasted_iota(jnp.int32, (AUG, LQ), 0)

    sc = jnp.dot(kc_ref[0, 0], qT, preferred_element_type=F32)
    cstart = pl.multiple_of(nc + 16 - 8 * qb, 8)
    sc = sc + cb_ref[0, pl.ds(cstart, nc), :]
    m_c = jnp.max(sc, axis=0, keepdims=True)
    e_c = jnp.exp(sc - m_c)
    l_c = jnp.sum(e_c, axis=0, keepdims=True)
    p_c = jnp.where(sc > 0.1 * NEG, e_c / l_c, 0.0)
    o_cmp = jnp.dot(vct_ref[0, 0], p_c.astype(BF16), preferred_element_type=F32)[:HEAD_DIM]

    p_sum = functools.reduce(jnp.add, [p_c[:, j * QB:(j + 1) * QB] for j in range(HPG)])
    imp = jnp.dot(ovl_ref[...], p_sum, preferred_element_type=F32,
                  precision=lax.Precision.HIGHEST)[HEAD_DIM:]
    blk = lax.broadcasted_iota(jnp.int32, (SLC_LEN, QB), 0)
    tq = s0 + lax.broadcasted_iota(jnp.int32, (SLC_LEN, QB), 1)
    cur = tq // SLC_LEN
    forced = (blk == 0) | (blk == cur) | (blk == cur - 1)
    future = blk > cur
    val = jnp.where(forced, 1e9, jnp.where(future, -1e9, imp))
    vals = [val[8 * a:8 * a + 8] for a in range(8)]
    sub = lax.broadcasted_iota(jnp.int32, (8, QB), 0)
    cnts = [jnp.zeros((8, QB), F32) for _ in range(8)]
    for jp in range(SLC_LEN):
        ap, r = divmod(jp, 8)
        rv = jnp.broadcast_to(vals[ap][r:r + 1], (8, QB))
        for a in range(8):
            gt = jnp.where(rv > vals[a], 1.0, 0.0)
            ge = jnp.where(rv >= vals[a], 1.0, 0.0)
            if a < ap:
                inc = gt
            elif a > ap:
                inc = ge
            else:
                inc = jnp.where(sub > r, ge, gt)
            cnts[a] = cnts[a] + inc
    cnt = jnp.concatenate(cnts, axis=0)
    allowed = (cnt < float(N_SELECT)) & (blk <= cur)
    near_blk = blk >= 2 * qb - 2
    neg_near = jnp.where(allowed, 0.0, NEG).astype(BF16)
    neg_far = jnp.where(allowed & jnp.logical_not(near_blk), 0.0, NEG).astype(BF16)
    q_near = jnp.concatenate([qT[:HEAD_DIM], jnp.tile(neg_near, (1, HPG))], axis=0)
    q_far = jnp.concatenate([qT[:HEAD_DIM], jnp.tile(neg_far, (1, HPG))], axis=0)

    acc_ref[...] = jnp.zeros_like(acc_ref)

    def far_body(t, m_prev):
        k0 = pl.multiple_of(SEL_PAD + t * FAR_TK, 128)
        s = jnp.dot(ks_ref[0, 0, pl.ds(k0, FAR_TK), :], q_far, preferred_element_type=F32)
        m_new = jnp.maximum(m_prev, jnp.max(s, axis=0, keepdims=True))
        p = jnp.exp(s - m_new).astype(BF16)
        pv = jnp.dot(vst_ref[0, 0, :, pl.ds(k0, FAR_TK)], p, preferred_element_type=F32)
        acc_ref[...] = acc_ref[...] * jnp.exp(m_prev - m_new) + pv
        return m_new

    n_far = (jnp.maximum(s0 - QB, 0) + FAR_TK - 1) // FAR_TK
    m_far = lax.fori_loop(0, n_far, far_body, jnp.full((1, LQ), NEG, F32))
    n0 = pl.multiple_of(s0, 128)
    band = band_ref[0]
    s = jnp.dot(ks_ref[0, 0, pl.ds(n0, 2 * QB), :], q_near, preferred_element_type=F32) + band
    m_s = jnp.maximum(m_far, jnp.max(s, axis=0, keepdims=True))
    p = jnp.exp(s - m_s).astype(BF16)
    acc = acc_ref[...] * jnp.exp(m_far - m_s) + jnp.dot(
        vst_ref[0, 0, :, pl.ds(n0, 2 * QB)], p, preferred_element_type=F32)
    o_sel = acc[:HEAD_DIM] / acc[HEAD_DIM:HEAD_DIM + 1]

    q_win = jnp.where(row < HEAD_DIM, qT, jnp.where(row == HEAD_DIM, NEG, 0.0).astype(BF16))
    sw = jnp.dot(kw_ref[0, 0, pl.ds(n0, NSA_WINDOW + QB), :], q_win, preferred_element_type=F32)
    kk = lax.broadcasted_iota(jnp.int32, (QB, LQ), 0)
    qi = lax.broadcasted_iota(jnp.int32, (QB, LQ), 1) % QB
    s_edge = sw[:QB] + jnp.where(kk > qi, 0.0, NEG)
    s_mid = sw[QB:NSA_WINDOW - QB]
    s_near = sw[NSA_WINDOW - QB:] + band
    m_w, (p_edge, p_mid, p_near) = _softmax_cols([s_edge, s_mid, s_near])
    p_w = jnp.concatenate([p_edge, p_mid, p_near], axis=0).astype(BF16)
    acc_w = jnp.dot(vwt_ref[0, 0, :, pl.ds(n0, NSA_WINDOW + QB)], p_w, preferred_element_type=F32)
    o_win = acc_w[:HEAD_DIM] / acc_w[HEAD_DIM:HEAD_DIM + 1]

    g = g_ref[0, 0, 0]
    o_ref[0, 0, 0] = (g[0:1] * o_cmp + g[1:2] * o_sel + g[2:3] * o_win).astype(o_ref.dtype)


def _nsa_attention(qT, gT, kc_aug, vcT, ks_aug, vsT, kw_aug, vwT, ovl, cband, band):
    B, G, nqb = qT.shape[:3]
    nc = kc_aug.shape[2]
    per_bg = lambda a: pl.BlockSpec((1, 1) + a.shape[2:], lambda b, g, q: (b, g) + (0,) * (a.ndim - 2))
    per_g = lambda a: pl.BlockSpec((1,) + a.shape[1:], lambda b, g, q: (g,) + (0,) * (a.ndim - 1))
    per_q = lambda a: pl.BlockSpec((1, 1, 1) + a.shape[3:], lambda b, g, q: (b, g, q, 0, 0))
    return pl.pallas_call(
        functools.partial(_nsa_kernel, nc=nc),
        grid=(B, G, nqb),
        in_specs=[per_q(qT), per_q(gT), per_bg(kc_aug), per_bg(vcT), per_bg(ks_aug), per_bg(vsT),
                  per_bg(kw_aug), per_bg(vwT), pl.BlockSpec(ovl.shape, lambda b, g, q: (0, 0)),
                  per_g(cband), per_g(band)],
        out_specs=pl.BlockSpec((1, 1, 1, HEAD_DIM, LQ), lambda b, g, q: (b, g, q, 0, 0)),
        out_shape=jax.ShapeDtypeStruct((B, G, nqb, HEAD_DIM, LQ), BF16),
        scratch_shapes=[pltpu.VMEM((AUG, LQ), F32)],
        compiler_params=_cparams(("arbitrary", "arbitrary", "arbitrary")),
        name="nsa_attention",
    )(qT, gT, kc_aug, vcT, ks_aug, vsT, kw_aug, vwT, ovl, cband, band)


def _swa_kernel(q_ref, k_ref, vt_ref, band_ref, sink_ref, o_ref):
    qb = pl.program_id(2)
    n0 = pl.multiple_of(qb * QB, 128)
    qT = q_ref[0, 0, 0]
    row = lax.broadcasted_iota(jnp.int32, (AUG, LQ), 0)
    q_aug = jnp.where(row < HEAD_DIM, qT, jnp.where(row == HEAD_DIM, NEG, 0.0).astype(BF16))
    s = jnp.dot(k_ref[0, 0, pl.ds(n0, 2 * QB), :], q_aug, preferred_element_type=F32) + band_ref[0]
    sink = sink_ref[0, 0:1]
    m = jnp.maximum(jnp.max(s, axis=0, keepdims=True), sink)
    p = jnp.exp(s - m).astype(BF16)
    acc = jnp.dot(vt_ref[0, 0, :, pl.ds(n0, 2 * QB)], p, preferred_element_type=F32)
    denom = acc[HEAD_DIM:HEAD_DIM + 1] + jnp.exp(sink - m)
    o_ref[0, 0, 0] = (acc[:HEAD_DIM] / denom).astype(o_ref.dtype)


def _swa_attention(qT, k_aug, vT, band, sink):
    B, G, nqb = qT.shape[:3]
    per_bg = lambda a: pl.BlockSpec((1, 1) + a.shape[2:], lambda b, g, q: (b, g) + (0,) * (a.ndim - 2))
    per_g = lambda a: pl.BlockSpec((1,) + a.shape[1:], lambda b, g, q: (g,) + (0,) * (a.ndim - 1))
    return pl.pallas_call(
        _swa_kernel,
        grid=(B, G, nqb),
        in_specs=[pl.BlockSpec((1, 1, 1, AUG, LQ), lambda b, g, q: (b, g, q, 0, 0)),
                  per_bg(k_aug), per_bg(vT), per_g(band), per_g(sink)],
        out_specs=pl.BlockSpec((1, 1, 1, HEAD_DIM, LQ), lambda b, g, q: (b, g, q, 0, 0)),
        out_shape=jax.ShapeDtypeStruct((B, G, nqb, HEAD_DIM, LQ), BF16),
        compiler_params=_cparams(("arbitrary", "arbitrary", "arbitrary")),
        name="swa_attention",
    )(qT, k_aug, vT, band, sink)


def _layer_norm(y, g, b):
    mu = jnp.mean(y, axis=-1, keepdims=True)
    yc = y - mu
    var = jnp.mean(yc * yc, axis=-1, keepdims=True)
    return yc * lax.rsqrt(var + LN_EPS) * g + b


def _top2_of4(a, b, c, d):
    hi1, lo1 = jnp.maximum(a, b), jnp.minimum(a, b)
    hi2, lo2 = jnp.maximum(c, d), jnp.minimum(c, d)
    return jnp.maximum(hi1, hi2) + jnp.maximum(jnp.minimum(hi1, hi2), jnp.maximum(lo1, lo2))


def _route(s, sb):
    n = EXPERTS_PER_GROUP
    score = [_top2_of4(*sb[n * r:n * r + n]) for r in range(N_GROUPS)]
    best = functools.reduce(jnp.maximum, score)
    taken = jnp.zeros_like(best) > 1.0
    in_grp = []
    for r in range(N_GROUPS):
        pick = (score[r] == best) & jnp.logical_not(taken)
        in_grp.append(pick)
        taken = taken | pick
    gates = []
    for e in range(N_EXPERTS):
        r = e // n
        ahead = jnp.zeros_like(best)
        for f in range(n * r, n * r + n):
            if f != e:
                beats = (sb[f] >= sb[e]) if f < e else (sb[f] > sb[e])
                ahead = ahead + jnp.where(beats, 1.0, 0.0)
        gates.append(jnp.where(in_grp[r] & (ahead < 2.0), s[e], 0.0))
    total = functools.reduce(jnp.add, gates)
    return [g / total for g in gates]


def _oproj_kernel(o_ref, x_ref, w_ref, ga_ref, lg_ref, lb_ref, shf_ref, scf_ref, rw_ref, rb_ref,
                  x1_ref, h_ref, gate_ref):
    mix = jnp.dot(o_ref[0], w_ref[...], preferred_element_type=F32)
    x1 = _layer_norm(ALPHA * x_ref[0] + (1.0 + ga_ref[0]) * mix, lg_ref[...], lb_ref[...])
    x1_ref[0] = x1
    h = x1 * (1.0 + scf_ref[0]) + shf_ref[0]
    h_ref[0] = h.astype(BF16)
    logit = lax.dot_general(rw_ref[...], h, (((1,), (1,)), ((), ())), preferred_element_type=F32,
                            precision=lax.Precision.HIGHEST)
    aff = jax.nn.sigmoid(logit)
    biased = aff + rb_ref[...][:, 0:1]
    s = [aff[e:e + 1] for e in range(N_EXPERTS)]
    sb = [biased[e:e + 1] for e in range(N_EXPERTS)]
    gate_ref[0] = jnp.concatenate(_route(s, sb), axis=0)


def _oproj(o, x, w_out, g_a, ln_g, ln_b, sh_f, sc_f, router_w, router_b, ts=512):
    B, S, D = x.shape
    row = lambda a: a.reshape(1, D)
    per_b = pl.BlockSpec((1, 1, D), lambda b, s: (b, 0, 0))
    full = lambda a: pl.BlockSpec(a.shape, lambda b, s: (0,) * a.ndim)
    rwT = router_w.T
    rb = jnp.broadcast_to(router_b.reshape(N_EXPERTS, 1), (N_EXPERTS, 128))
    args = (o, x, w_out.astype(BF16), g_a.reshape(B, 1, D), row(ln_g), row(ln_b),
            sh_f.reshape(B, 1, D), sc_f.reshape(B, 1, D), rwT, rb)
    tile = pl.BlockSpec((1, ts, D), lambda b, s: (b, s, 0))
    return pl.pallas_call(
        _oproj_kernel,
        grid=(B, S // ts),
        in_specs=[tile, tile, full(args[2]), per_b, full(args[4]), full(args[5]), per_b, per_b,
                  full(rwT), full(rb)],
        out_specs=[tile, tile, pl.BlockSpec((1, N_EXPERTS, ts), lambda b, s: (b, 0, s))],
        out_shape=[jax.ShapeDtypeStruct((B, S, D), F32), jax.ShapeDtypeStruct((B, S, D), BF16),
                   jax.ShapeDtypeStruct((B, N_EXPERTS, S), F32)],
        compiler_params=_cparams(("arbitrary", "arbitrary")),
        name="oproj_ln_router",
    )(*args)


def _moe_kernel(h_ref, gate_ref, wgu_ref, wd_ref, x_ref, gf_ref, lg_ref, lb_ref, o_ref, acc_ref):
    e = pl.program_id(1)

    @pl.when(e == 0)
    def _():
        acc_ref[...] = jnp.zeros_like(acc_ref)

    gu = jnp.dot(h_ref[...], wgu_ref[0], preferred_element_type=F32)
    a, u = gu[:, :D_EXPERT], gu[:, D_EXPERT:]
    gates = gate_ref[...]
    lane = lax.broadcasted_iota(jnp.int32, gates.shape, 1)
    gcol = jnp.sum(jnp.where(lane == e, gates, 0.0), axis=1, keepdims=True)
    he = (a * jax.nn.sigmoid(a) * u * gcol).astype(BF16)
    acc_ref[...] += jnp.dot(he, wd_ref[0], preferred_element_type=F32)

    @pl.when(e == N_EXPERTS - 1)
    def _():
        y = ALPHA * x_ref[...] + (1.0 + gf_ref[0]) * acc_ref[...]
        o_ref[...] = _layer_norm(y, lg_ref[...], lb_ref[...])


def _moe(h, gates, w_gate, w_up, w_down, x, g_f, ln_g, ln_b, ts=1024):
    B, S, D = x.shape
    T = B * S
    wgu = jnp.concatenate([w_gate, w_up], axis=-1).astype(BF16)
    wd = w_down.astype(BF16)
    gl = jnp.pad(gates.transpose(0, 2, 1).reshape(T, N_EXPERTS), ((0, 0), (0, 128 - N_EXPERTS)))
    tile = pl.BlockSpec((ts, D), lambda t, e: (t, 0))
    spt = S // ts
    out = pl.pallas_call(
        _moe_kernel,
        grid=(T // ts, N_EXPERTS),
        in_specs=[tile, pl.BlockSpec((ts, 128), lambda t, e: (t, 0)),
                  pl.BlockSpec((1, D, 2 * D_EXPERT), lambda t, e: (e, 0, 0)),
                  pl.BlockSpec((1, D_EXPERT, D), lambda t, e: (e, 0, 0)),
                  tile, pl.BlockSpec((1, 1, D), lambda t, e: (t // spt, 0, 0)),
                  pl.BlockSpec((1, D), lambda t, e: (0, 0)), pl.BlockSpec((1, D), lambda t, e: (0, 0))],
        out_specs=tile,
        out_shape=jax.ShapeDtypeStruct((T, D), F32),
        scratch_shapes=[pltpu.VMEM((ts, D), F32)],
        compiler_params=_cparams(("arbitrary", "arbitrary")),
        name="moe_ln",
    )(h.reshape(T, D), gl, wgu, wd, x.reshape(T, D), g_f.reshape(B, 1, D), ln_g.reshape(1, D),
      ln_b.reshape(1, D))
    return out.reshape(B, S, D)


def _q_blocks(q, B, S):
    nqb = S // QB
    q = (q.astype(F32) * ATTN_SCALE).astype(BF16).reshape(B, nqb, QB, N_KV, HPG, HEAD_DIM)
    q = q.transpose(0, 3, 1, 5, 4, 2).reshape(B, N_KV, nqb, HEAD_DIM, LQ)
    return jnp.pad(q, ((0, 0), (0, 0), (0, 0), (0, AUG - HEAD_DIM), (0, 0)))


def _kv_rows(t, B, S):
    return t.reshape(B, S, N_KV, HEAD_DIM).transpose(0, 2, 1, 3)


def _keys_aug(k, extra, pad, pad_extra):
    B, G, S, _ = k.shape
    body = jnp.concatenate([k, jnp.broadcast_to(extra.astype(BF16), (B, G) + extra.shape)], axis=-1)
    front = jnp.concatenate([jnp.zeros((pad, HEAD_DIM), BF16), pad_extra.astype(BF16)], axis=-1)
    return jnp.concatenate([jnp.broadcast_to(front, (B, G) + front.shape), body], axis=2)


def _values_t(v, pad):
    B, G, S, _ = v.shape
    vt = jnp.pad(v.transpose(0, 1, 3, 2), ((0, 0), (0, 0), (0, 0), (pad, 0)))
    return jnp.concatenate([vt, jnp.ones((B, G, AUG - HEAD_DIM, pad + S), v.dtype)], axis=2)


def _from_blocks(oT, B, S):
    nqb = S // QB
    o = oT.reshape(B, N_KV, nqb, HEAD_DIM, HPG, QB).transpose(0, 2, 5, 1, 4, 3)
    return o.reshape(B, S, Q_WIDTH)


def _nsa_layer_attention(x, sh, sc, w_in, k_w1, k_w2, v_w1, v_w2, k_pos, v_pos, fb):
    B, S, _ = x.shape
    nqb, nc, n_slc = S // QB, S // CMP_STRIDE, S // SLC_LEN
    w_main = w_in[:, :Q_WIDTH + 6 * KV_WIDTH].astype(BF16)
    w_gate = jnp.pad(w_in[:, Q_WIDTH + 6 * KV_WIDTH:], ((0, 0), (0, 128 - 3 * N_HEADS))).astype(BF16)
    proj = _modmm(x, sh, sc, w_main, BF16, tn=640, name="nsa_proj")
    gates = _modmm(x, sh, sc, w_gate, F32, sigmoid=True, name="nsa_gates")[..., :3 * N_HEADS]

    parts = [proj[..., Q_WIDTH + i * KV_WIDTH:Q_WIDTH + (i + 1) * KV_WIDTH] for i in range(6)]
    kc, vc, ksl, vsl, kw, vw = [_kv_rows(t, B, S) for t in parts]
    qT = _q_blocks(proj[..., :Q_WIDTH], B, S)
    gT = gates.reshape(B, nqb, QB, 3, N_KV, HPG).transpose(0, 4, 1, 3, 5, 2).reshape(B, N_KV, nqb, 3, LQ)
    gT = jnp.pad(gT, ((0, 0), (0, 0), (0, 0), (0, 5), (0, 0)))

    to_chunks = lambda t: t.reshape(B, N_KV * nc, CMP_STRIDE * HEAD_DIM)
    k_cmp = _compress(to_chunks(kc), k_w1, k_w2, k_pos)
    v_cmp = _compress(to_chunks(vc), v_w1, v_w2, v_pos)
    kc_aug = jnp.pad(k_cmp.astype(BF16), ((0, 0), (0, 0), (0, 0), (0, AUG - HEAD_DIM)))
    vcT = _values_t(v_cmp.astype(BF16), 0)

    onehot = (np.arange(S)[:, None] // SLC_LEN == np.arange(AUG - HEAD_DIM)[None, :]).astype(np.float32)
    ks_aug = _keys_aug(ksl, jnp.asarray(onehot), SEL_PAD, jnp.ones((SEL_PAD, AUG - HEAD_DIM), F32))
    vsT = _values_t(vsl, SEL_PAD)
    pad_lane = (np.arange(AUG - HEAD_DIM)[None, :] == 0).astype(np.float32)
    kw_aug = _keys_aug(kw, jnp.zeros((S, AUG - HEAD_DIM), F32), NSA_WINDOW,
                       jnp.asarray(np.broadcast_to(pad_lane, (NSA_WINDOW, AUG - HEAD_DIM))))
    vwT = _values_t(vw, NSA_WINDOW)

    c_start = np.arange(nc)[None, :] * CMP_STRIDE
    s_start = np.arange(AUG - HEAD_DIM)[:, None] * SLC_LEN
    ovl = ((c_start < s_start + SLC_LEN) & (c_start + CMP_LEN > s_start)
           & (np.arange(nc)[None, :] < nc - 1) & (np.arange(AUG - HEAD_DIM)[:, None] < n_slc))
    ovl = np.concatenate([np.zeros((HEAD_DIM, nc)), ovl.astype(np.float64)], axis=0).astype(np.float32)

    oT = _nsa_attention(qT, gT, kc_aug, vcT, ks_aug, vsT, kw_aug, vwT, jnp.asarray(ovl),
                        _cmp_band(fb, nc), _near_band(fb, NSA_WINDOW))
    return _from_blocks(oT, B, S)


def _swa_layer_attention(x, sh, sc, w_q, w_kv, sinks, rel_bias, fb):
    B, S, D = x.shape
    zero = jnp.zeros((B, D), F32)
    q = _modmm(x, sh, sc, w_q.astype(BF16), BF16, name="swa_q")
    kv = _modmm(x, zero, zero, w_kv.astype(BF16), BF16, name="shared_kv")
    k = _kv_rows(kv[..., :KV_WIDTH], B, S)
    v = _kv_rows(kv[..., KV_WIDTH:], B, S)
    pad_lane = (np.arange(AUG - HEAD_DIM)[None, :] == 0).astype(np.float32)
    k_aug = _keys_aug(k, jnp.zeros((S, AUG - HEAD_DIM), F32), QB,
                      jnp.asarray(np.broadcast_to(pad_lane, (QB, AUG - HEAD_DIM))))
    vT = _values_t(v, QB)
    sink = sinks.astype(F32) - rel_bias.astype(F32)[N_BUCKETS - 1]
    sink = jnp.broadcast_to(sink.reshape(N_KV, 1, HPG, 1), (N_KV, 8, HPG, QB)).reshape(N_KV, 8, LQ)
    oT = _swa_attention(_q_blocks(q, B, S), k_aug, vT, _near_band(fb, SWA_WINDOW), sink)
    return _from_blocks(oT, B, S)


def kernel(x, c, nsa_w_in, cmp_k_w1, cmp_k_w2, cmp_v_w1, cmp_v_w2, cmp_k_pos, cmp_v_pos, nsa_w_out,
           swa_w_q, swa_sinks, swa_w_out, shared_w_kv, rel_bias, router_w, router_b, moe_w_gate,
           moe_w_up, moe_w_down, ada_w, ada_b, ln_g, ln_b):
    B, S, D = x.shape
    ada = _ada(c, ada_w, ada_b)
    fb = _shifted_bias(rel_bias)
    for layer in range(DEPTH):
        sh_a, sc_a, g_a, sh_f, sc_f, g_f = [ada[layer, :, i * D:(i + 1) * D] for i in range(6)]
        if layer == 0:
            o = _nsa_layer_attention(x, sh_a, sc_a, nsa_w_in[0], cmp_k_w1[0], cmp_k_w2[0], cmp_v_w1[0],
                                     cmp_v_w2[0], cmp_k_pos[0], cmp_v_pos[0], fb)
            w_out = nsa_w_out[0]
        else:
            o = _swa_layer_attention(x, sh_a, sc_a, swa_w_q[0], shared_w_kv, swa_sinks[0], rel_bias, fb)
            w_out = swa_w_out[0]
        x1, h, gates = _oproj(o, x, w_out, g_a, ln_g[layer, 0], ln_b[layer, 0], sh_f, sc_f,
                              router_w, router_b)
        x = _moe(h, gates, moe_w_gate[layer], moe_w_up[layer], moe_w_down[layer], x1, g_f,
                 ln_g[layer, 1], ln_b[layer, 1])
    return x
```

```python
import functools
import math

import numpy as np
import jax
import jax.numpy as jnp
from jax import lax
from jax.experimental import pallas as pl
from jax.experimental.pallas import tpu as pltpu

F32 = jnp.float32
BF16 = jnp.bfloat16

D_MODEL = 1024
HEAD_DIM = 64
N_HEADS = 16
N_KV = 4
HPG = 4
Q_WIDTH = N_HEADS * HEAD_DIM
KV_WIDTH = N_KV * HEAD_DIM
CMP_LEN = 32
CMP_STRIDE = 16
CMP_HIDDEN = 256
SLC_LEN = 64
N_SELECT = 16
NSA_WINDOW = 512
SWA_WINDOW = 128
N_BUCKETS = 32
MAX_DISTANCE = 128
N_EXPERTS = 16
N_GROUPS = 4
EXPERTS_PER_GROUP = 4
D_EXPERT = 256
DEPTH = 2
ALPHA = (2.0 * DEPTH) ** 0.25
LN_EPS = 1e-5
NEG = -1e30
ATTN_SCALE = HEAD_DIM ** -0.5

QB = 128
LQ = HPG * QB
AUG = 128
MASK_LANES = AUG - HEAD_DIM
CMP_BAND = 24
FAR_TK = 512
WIN_KEYS = NSA_WINDOW + QB
N_GATE = 3 * N_HEADS
VMEM_LIMIT = 48 * 1024 * 1024

_NT = (((1,), (1,)), ((), ()))


def _cparams(sem):
    return pltpu.CompilerParams(dimension_semantics=sem, vmem_limit_bytes=VMEM_LIMIT)


def _ada_kernel(c_ref, w_ref, b_ref, o_ref):
    c = c_ref[...]
    cond = c * jax.nn.sigmoid(c)
    o_ref[0] = jnp.dot(cond, w_ref[0], preferred_element_type=F32,
                       precision=lax.Precision.HIGHEST) + b_ref[0]


def _ada(c, ada_w, ada_b):
    B, D = c.shape
    n6 = ada_w.shape[-1]
    tn = 1536
    return pl.pallas_call(
        _ada_kernel,
        grid=(DEPTH, n6 // tn),
        in_specs=[pl.BlockSpec((B, D), lambda l, n: (0, 0)),
                  pl.BlockSpec((1, D, tn), lambda l, n: (l, 0, n)),
                  pl.BlockSpec((1, 1, tn), lambda l, n: (l, 0, n))],
        out_specs=pl.BlockSpec((1, B, tn), lambda l, n: (l, 0, n)),
        out_shape=jax.ShapeDtypeStruct((DEPTH, B, n6), F32),
        compiler_params=_cparams(("arbitrary", "arbitrary")),
        name="ada",
    )(c, ada_w, ada_b.reshape(DEPTH, 1, n6))


def _store_q_blocks(q_ref, qT, ts):
    zeros = jnp.zeros((MASK_LANES, LQ), BF16)
    for g in range(N_KV):
        for k in range(ts // QB):
            for j in range(HPG):
                r0 = (g * HPG + j) * HEAD_DIM
                q_ref[0, g, k, 0:HEAD_DIM, j * QB:(j + 1) * QB] = (
                    qT[r0:r0 + HEAD_DIM, k * QB:(k + 1) * QB].astype(BF16))
            q_ref[0, g, k, HEAD_DIM:AUG, :] = zeros


def _store_vt(vt_ref, vT, ts):
    ones = jnp.ones((MASK_LANES, ts), BF16)
    for g in range(N_KV):
        vt_ref[0, g, 0:HEAD_DIM, :] = vT[g * HEAD_DIM:(g + 1) * HEAD_DIM].astype(BF16)
        vt_ref[0, g, HEAD_DIM:AUG, :] = ones


def _nsa_proj_kernel(x_ref, sh_ref, sc_ref, wt_ref, ws_ref, q_ref, gt_ref, ks_ref, vst_ref, kw_ref,
                     vwt_ref, cv_ref, *, ts):
    s = pl.program_id(1)
    h = (x_ref[0] * (1.0 + sc_ref[0]) + sh_ref[0]).astype(BF16)
    yT = lax.dot_general(wt_ref[...], h, _NT, preferred_element_type=F32)
    y = jnp.dot(h, ws_ref[...], preferred_element_type=F32)
    _store_q_blocks(q_ref, yT[:Q_WIDTH], ts)
    _store_vt(vst_ref, yT[Q_WIDTH:Q_WIDTH + KV_WIDTH], ts)
    _store_vt(vwt_ref, yT[Q_WIDTH + KV_WIDTH:Q_WIDTH + 2 * KV_WIDTH], ts)
    gates = jax.nn.sigmoid(yT[Q_WIDTH + 2 * KV_WIDTH:])
    for k in range(ts // QB):
        gt_ref[0, k] = gates[:, k * QB:(k + 1) * QB]
    lane = lax.broadcasted_iota(jnp.int32, (ts, AUG), 1)
    tok = s * ts + lax.broadcasted_iota(jnp.int32, (ts, AUG), 0)
    onehot = (lane - HEAD_DIM) == tok // SLC_LEN
    for g in range(N_KV):
        ks_ref[0, g] = jnp.where(onehot, 1.0, y[:, g * AUG:(g + 1) * AUG]).astype(BF16)
        kw_ref[0, g] = y[:, (N_KV + g) * AUG:(N_KV + g + 1) * AUG].astype(BF16)
    cv_ref[0] = y[:, 2 * N_KV * AUG:].astype(BF16)


def _pad_heads(w):
    D = w.shape[0]
    w = w.reshape(D, N_KV, HEAD_DIM)
    return jnp.pad(w, ((0, 0), (0, 0), (0, MASK_LANES))).reshape(D, N_KV * AUG)


def _nsa_proj(x, shift, scale, w_in, ts=512):
    B, S, D = x.shape
    nqb = S // QB
    cols = lambda i: w_in[:, Q_WIDTH + i * KV_WIDTH:Q_WIDTH + (i + 1) * KV_WIDTH]
    w_kc, w_vc, w_ksl, w_vsl, w_kw, w_vw = [cols(i) for i in range(6)]
    wt = jnp.concatenate([w_in[:, :Q_WIDTH] * ATTN_SCALE, w_vsl, w_vw,
                          w_in[:, Q_WIDTH + 6 * KV_WIDTH:]], axis=1).T.astype(BF16)
    ws = jnp.concatenate([_pad_heads(w_ksl), _pad_heads(w_kw), w_kc, w_vc], axis=1).astype(BF16)
    per_b = pl.BlockSpec((1, 1, D), lambda b, s: (b, 0, 0))
    kspec = pl.BlockSpec((1, N_KV, ts, AUG), lambda b, s: (b, 0, s, 0))
    vspec = pl.BlockSpec((1, N_KV, AUG, ts), lambda b, s: (b, 0, 0, s))
    k_shape = jax.ShapeDtypeStruct((B, N_KV, S, AUG), BF16)
    v_shape = jax.ShapeDtypeStruct((B, N_KV, AUG, S), BF16)
    return pl.pallas_call(
        functools.partial(_nsa_proj_kernel, ts=ts),
        grid=(B, S // ts),
        in_specs=[pl.BlockSpec((1, ts, D), lambda b, s: (b, s, 0)), per_b, per_b,
                  pl.BlockSpec(wt.shape, lambda b, s: (0, 0)), pl.BlockSpec(ws.shape, lambda b, s: (0, 0))],
        out_specs=[pl.BlockSpec((1, N_KV, ts // QB, AUG, LQ), lambda b, s: (b, 0, s, 0, 0)),
                   pl.BlockSpec((1, ts // QB, N_GATE, QB), lambda b, s: (b, s, 0, 0)),
                   kspec, vspec, kspec, vspec,
                   pl.BlockSpec((1, ts, 2 * KV_WIDTH), lambda b, s: (b, s, 0))],
        out_shape=[jax.ShapeDtypeStruct((B, N_KV, nqb, AUG, LQ), BF16),
                   jax.ShapeDtypeStruct((B, nqb, N_GATE, QB), F32),
                   k_shape, v_shape, k_shape, v_shape,
                   jax.ShapeDtypeStruct((B, S, 2 * KV_WIDTH), BF16)],
        compiler_params=_cparams(("arbitrary", "arbitrary")),
        name="nsa_proj",
    )(x, shift.reshape(B, 1, D), scale.reshape(B, 1, D), wt, ws)


def _swa_proj_kernel(x_ref, sh_ref, sc_ref, wq_ref, wk_ref, wv_ref, q_ref, k_ref, vt_ref, *, ts):
    x = x_ref[0]
    h = (x * (1.0 + sc_ref[0]) + sh_ref[0]).astype(BF16)
    xb = x.astype(BF16)
    _store_q_blocks(q_ref, lax.dot_general(wq_ref[...], h, _NT, preferred_element_type=F32), ts)
    _store_vt(vt_ref, lax.dot_general(wv_ref[...], xb, _NT, preferred_element_type=F32), ts)
    k = jnp.dot(xb, wk_ref[...], preferred_element_type=F32)
    for g in range(N_KV):
        k_ref[0, g] = k[:, g * AUG:(g + 1) * AUG].astype(BF16)


def _swa_proj(x, shift, scale, w_q, w_kv, ts=512):
    B, S, D = x.shape
    wq = (w_q * ATTN_SCALE).T.astype(BF16)
    wk = _pad_heads(w_kv[:, :KV_WIDTH]).astype(BF16)
    wv = w_kv[:, KV_WIDTH:].T.astype(BF16)
    per_b = pl.BlockSpec((1, 1, D), lambda b, s: (b, 0, 0))
    full = lambda a: pl.BlockSpec(a.shape, lambda b, s: (0, 0))
    return pl.pallas_call(
        functools.partial(_swa_proj_kernel, ts=ts),
        grid=(B, S // ts),
        in_specs=[pl.BlockSpec((1, ts, D), lambda b, s: (b, s, 0)), per_b, per_b, full(wq), full(wk), full(wv)],
        out_specs=[pl.BlockSpec((1, N_KV, ts // QB, AUG, LQ), lambda b, s: (b, 0, s, 0, 0)),
                   pl.BlockSpec((1, N_KV, ts, AUG), lambda b, s: (b, 0, s, 0)),
                   pl.BlockSpec((1, N_KV, AUG, ts), lambda b, s: (b, 0, 0, s))],
        out_shape=[jax.ShapeDtypeStruct((B, N_KV, S // QB, AUG, LQ), BF16),
                   jax.ShapeDtypeStruct((B, N_KV, S, AUG), BF16),
                   jax.ShapeDtypeStruct((B, N_KV, AUG, S), BF16)],
        compiler_params=_cparams(("arbitrary", "arbitrary")),
        name="swa_proj",
    )(x, shift.reshape(B, 1, D), scale.reshape(B, 1, D), wq, wk, wv)


def _gelu_tanh(x):
    return 0.5 * x * (1.0 + jnp.tanh(math.sqrt(2.0 / math.pi) * (x + 0.044715 * (x * x * x))))


def _compress_kernel(ch_ref, w1_ref, w2_ref, pos_ref, o_ref, *, nc):
    half = CMP_STRIDE * HEAD_DIM
    ch = ch_ref[0]
    w1 = w1_ref[...]
    top = jnp.dot(ch, w1[:half], preferred_element_type=F32)
    bot = jnp.dot(ch, w1[half:], preferred_element_type=F32)
    posw = jnp.dot(pos_ref[...], w1, preferred_element_type=F32)[0:1]
    w2 = w2_ref[...]
    for g in range(N_KV):
        bot_g = pltpu.roll(bot[g * nc:(g + 1) * nc], nc - 1, 0)
        pre = top[g * nc:(g + 1) * nc] + bot_g + posw
        o_ref[0, g] = jnp.dot(_gelu_tanh(pre).astype(BF16), w2, preferred_element_type=F32)


def _compress(chunks, w1, w2, pos):
    B, gn, ck = chunks.shape
    nc = gn // N_KV
    pos8 = jnp.broadcast_to(pos.reshape(1, CMP_LEN * HEAD_DIM), (8, CMP_LEN * HEAD_DIM)).astype(BF16)
    return pl.pallas_call(
        functools.partial(_compress_kernel, nc=nc),
        grid=(B,),
        in_specs=[pl.BlockSpec((1, gn, ck), lambda b: (b, 0, 0)),
                  pl.BlockSpec(w1.shape, lambda b: (0, 0)),
                  pl.BlockSpec(w2.shape, lambda b: (0, 0)),
                  pl.BlockSpec(pos8.shape, lambda b: (0, 0))],
        out_specs=pl.BlockSpec((1, N_KV, nc, HEAD_DIM), lambda b: (b, 0, 0, 0)),
        out_shape=jax.ShapeDtypeStruct((B, N_KV, nc, HEAD_DIM), F32),
        compiler_params=_cparams(("arbitrary",)),
        name="compress",
    )(chunks, w1.astype(BF16), w2.astype(BF16), pos8)


def _bucket_np(d):
    d = np.maximum(np.asarray(d, np.int64), 0)
    max_exact = N_BUCKETS // 2
    large = max_exact + (np.log(np.maximum(d, 1).astype(np.float32) / np.float32(max_exact))
                         / np.float32(math.log(MAX_DISTANCE / max_exact))
                         * np.float32(N_BUCKETS - max_exact)).astype(np.int32)
    large = np.minimum(large, N_BUCKETS - 1)
    return np.where(d < max_exact, d, large).astype(np.int32)


def _shifted_bias(rel_table):
    t = rel_table.astype(F32)
    return (t[_bucket_np(np.arange(MAX_DISTANCE))] - t[N_BUCKETS - 1][None, :]).T


def _to_group_lanes(m):
    H, K, Q = m.shape
    return m.reshape(N_KV, HPG, K, Q).transpose(0, 2, 1, 3).reshape(N_KV, K, HPG * Q)


def _near_band(fb, window):
    H = fb.shape[0]
    nv = 3 * QB - 1
    dist = np.arange(nv) - (QB - 1)
    ok = (dist >= 0) & (dist < window)
    idx = np.clip(dist, 0, MAX_DISTANCE - 1)
    far = dist >= MAX_DISTANCE
    v = jnp.where(ok[None, :], jnp.where(far[None, :], 0.0, fb[:, idx]), NEG)
    rows = 2 * QB
    flat = jnp.tile(v, (1, rows + 1))[:, :rows * (nv + 1)]
    m = flat.reshape(H, rows, nv + 1)[:, ::-1, :QB]
    return _to_group_lanes(m)


def _masked_rows(n):
    return jnp.full((N_KV, n, LQ), NEG, F32)


def _window_table(fb):
    kk = np.arange(QB)[:, None]
    qi = np.arange(LQ)[None, :] % QB
    edge = np.where(kk > qi, 0.0, NEG).astype(np.float32)
    edge = jnp.broadcast_to(jnp.asarray(edge), (N_KV, QB, LQ))
    mid = jnp.zeros((N_KV, NSA_WINDOW - 2 * QB, LQ), F32)
    return jnp.concatenate([edge, mid, _near_band(fb, NSA_WINDOW), _masked_rows(NSA_WINDOW)], axis=1)


def _cmp_band(fb, nc):
    m = np.arange(CMP_BAND)[:, None]
    i = np.arange(QB)[None, :]
    dist = i + CMP_STRIDE * 16 - CMP_STRIDE * m - (CMP_LEN - 1)
    idx = np.clip(dist, 0, MAX_DISTANCE - 1)
    v = jnp.where((dist >= 0)[None], jnp.where((dist >= MAX_DISTANCE)[None], 0.0, fb[:, idx]), NEG)
    band = _to_group_lanes(v)
    return jnp.concatenate([jnp.zeros((N_KV, nc, LQ), F32), band, _masked_rows(nc)], axis=1)


def _to_token_rows(oT):
    pair = lambda a: jnp.concatenate([oT[:, a * QB:(a + 1) * QB], oT[:, (a + 1) * QB:(a + 2) * QB]], axis=0)
    return jnp.concatenate([pair(0).T, pair(2).T], axis=1)


def _nsa_kernel(q_ref, gt_ref, kc_ref, vct_ref, ks_ref, vst_ref, kw_ref, vwt_ref, ovl_ref, cb_ref,
                wt_ref, o_ref, acc_ref, *, nc):
    g = pl.program_id(1)
    qb = pl.program_id(2)
    s0 = qb * QB
    qT = q_ref[0, 0, 0]

    sc = jnp.dot(kc_ref[0, 0], qT, preferred_element_type=F32)
    cstart = pl.multiple_of(nc + 16 - 8 * qb, 8)
    sc = sc + cb_ref[0, pl.ds(cstart, nc), :]
    m_c = jnp.max(sc, axis=0, keepdims=True)
    e_c = jnp.exp(sc - m_c)
    l_c = jnp.sum(e_c, axis=0, keepdims=True)
    p_c = jnp.where(sc > 0.1 * NEG, e_c / l_c, 0.0)
    o_cmp = jnp.dot(vct_ref[0, 0], p_c.astype(BF16), preferred_element_type=F32)[:HEAD_DIM]

    p_sum = functools.reduce(jnp.add, [p_c[:, j * QB:(j + 1) * QB] for j in range(HPG)])
    imp = jnp.dot(ovl_ref[...], p_sum, preferred_element_type=F32,
                  precision=lax.Precision.HIGHEST)[HEAD_DIM:]
    blk = lax.broadcasted_iota(jnp.int32, (SLC_LEN, QB), 0)
    tq = s0 + lax.broadcasted_iota(jnp.int32, (SLC_LEN, QB), 1)
    cur = tq // SLC_LEN
    forced = (blk == 0) | (blk == cur) | (blk == cur - 1)
    future = blk > cur
    val = jnp.where(forced, 1e9, jnp.where(future, -1e9, imp))
    vals = [val[8 * a:8 * a + 8] for a in range(8)]
    sub = lax.broadcasted_iota(jnp.int32, (8, QB), 0)
    cnts = [jnp.zeros((8, QB), F32) for _ in range(8)]
    for jp in range(SLC_LEN):
        ap, r = divmod(jp, 8)
        rv = jnp.broadcast_to(vals[ap][r:r + 1], (8, QB))
        for a in range(8):
            gt = jnp.where(rv > vals[a], 1.0, 0.0)
            ge = jnp.where(rv >= vals[a], 1.0, 0.0)
            if a < ap:
                inc = gt
            elif a > ap:
                inc = ge
            else:
                inc = jnp.where(sub > r, ge, gt)
            cnts[a] = cnts[a] + inc
    cnt = jnp.concatenate(cnts, axis=0)
    allowed = (cnt < float(N_SELECT)) & (blk <= cur)
    near_blk = blk >= 2 * qb - 2
    neg_near = jnp.where(allowed, 0.0, NEG).astype(BF16)
    neg_far = jnp.where(allowed & jnp.logical_not(near_blk), 0.0, NEG).astype(BF16)
    q_near = jnp.concatenate([qT[:HEAD_DIM], jnp.tile(neg_near, (1, HPG))], axis=0)
    q_far = jnp.concatenate([qT[:HEAD_DIM], jnp.tile(neg_far, (1, HPG))], axis=0)

    acc_ref[...] = jnp.zeros_like(acc_ref)

    def far_body(t, m_prev):
        k0 = pl.multiple_of(t * FAR_TK, FAR_TK)
        s = jnp.dot(ks_ref[0, 0, pl.ds(k0, FAR_TK), :], q_far, preferred_element_type=F32)
        m_new = jnp.maximum(m_prev, jnp.max(s, axis=0, keepdims=True))
        p = jnp.exp(s - m_new).astype(BF16)
        pv = jnp.dot(vst_ref[0, 0, :, pl.ds(k0, FAR_TK)], p, preferred_element_type=F32)
        acc_ref[...] = acc_ref[...] * jnp.exp(m_prev - m_new) + pv
        return m_new

    n_far = (jnp.maximum(s0 - QB, 0) + FAR_TK - 1) // FAR_TK
    m_far = lax.fori_loop(0, n_far, far_body, jnp.full((1, LQ), NEG, F32))
    n0 = pl.multiple_of(jnp.maximum(s0 - QB, 0), QB)
    b0 = pl.multiple_of(NSA_WINDOW - QB + n0 - (s0 - QB), QB)
    s = (jnp.dot(ks_ref[0, 0, pl.ds(n0, 2 * QB), :], q_near, preferred_element_type=F32)
         + wt_ref[0, pl.ds(b0, 2 * QB), :])
    m_s = jnp.maximum(m_far, jnp.max(s, axis=0, keepdims=True))
    p = jnp.exp(s - m_s).astype(BF16)
    acc = acc_ref[...] * jnp.exp(m_far - m_s) + jnp.dot(
        vst_ref[0, 0, :, pl.ds(n0, 2 * QB)], p, preferred_element_type=F32)
    o_sel = acc[:HEAD_DIM] / acc[HEAD_DIM:HEAD_DIM + 1]

    w0 = pl.multiple_of(jnp.maximum(s0 - NSA_WINDOW, 0), QB)
    t0 = pl.multiple_of(w0 - (s0 - NSA_WINDOW), QB)
    sw = (jnp.dot(kw_ref[0, 0, pl.ds(w0, WIN_KEYS), :], qT, preferred_element_type=F32)
          + wt_ref[0, pl.ds(t0, WIN_KEYS), :])
    m_w = jnp.max(sw, axis=0, keepdims=True)
    p_w = jnp.exp(sw - m_w).astype(BF16)
    acc_w = jnp.dot(vwt_ref[0, 0, :, pl.ds(w0, WIN_KEYS)], p_w, preferred_element_type=F32)
    o_win = acc_w[:HEAD_DIM] / acc_w[HEAD_DIM:HEAD_DIM + 1]

    def gate(branch):
        rows = [gt_ref[0, 0, pl.ds(branch * N_HEADS + g * HPG + j, 1), :] for j in range(HPG)]
        return jnp.concatenate(rows, axis=1)

    o = gate(0) * o_cmp + gate(1) * o_sel + gate(2) * o_win
    o_ref[0] = _to_token_rows(o).astype(o_ref.dtype)


def _nsa_attention(qT, gatesT, kc_aug, vcT, ks_aug, vsT, kw_aug, vwT, ovl, cband, wtable):
    B, G, nqb = qT.shape[:3]
    nc = kc_aug.shape[2]
    per_bg = lambda a: pl.BlockSpec((1, 1) + a.shape[2:], lambda b, g, q: (b, g) + (0,) * (a.ndim - 2))
    per_g = lambda a: pl.BlockSpec((1,) + a.shape[1:], lambda b, g, q: (g,) + (0,) * (a.ndim - 1))
    return pl.pallas_call(
        functools.partial(_nsa_kernel, nc=nc),
        grid=(B, G, nqb),
        in_specs=[pl.BlockSpec((1, 1, 1, AUG, LQ), lambda b, g, q: (b, g, q, 0, 0)),
                  pl.BlockSpec((1, 1, N_GATE, QB), lambda b, g, q: (b, q, 0, 0)),
                  per_bg(kc_aug), per_bg(vcT), per_bg(ks_aug), per_bg(vsT),
                  per_bg(kw_aug), per_bg(vwT), pl.BlockSpec(ovl.shape, lambda b, g, q: (0, 0)),
                  per_g(cband), per_g(wtable)],
        out_specs=pl.BlockSpec((1, QB, HPG * HEAD_DIM), lambda b, g, q: (b, q, g)),
        out_shape=jax.ShapeDtypeStruct((B, nqb * QB, Q_WIDTH), BF16),
        scratch_shapes=[pltpu.VMEM((AUG, LQ), F32)],
        compiler_params=_cparams(("arbitrary", "arbitrary", "arbitrary")),
        name="nsa_attention",
    )(qT, gatesT, kc_aug, vcT, ks_aug, vsT, kw_aug, vwT, ovl, cband, wtable)


def _swa_kernel(q_ref, k_ref, vt_ref, band_ref, sink_ref, o_ref):
    s0 = pl.program_id(2) * QB
    n0 = pl.multiple_of(jnp.maximum(s0 - QB, 0), QB)
    b0 = pl.multiple_of(n0 - (s0 - QB), QB)
    s = (jnp.dot(k_ref[0, 0, pl.ds(n0, 2 * QB), :], q_ref[0, 0, 0], preferred_element_type=F32)
         + band_ref[0, pl.ds(b0, 2 * QB), :])
    sink = sink_ref[0, 0:1]
    m = jnp.maximum(jnp.max(s, axis=0, keepdims=True), sink)
    p = jnp.exp(s - m).astype(BF16)
    acc = jnp.dot(vt_ref[0, 0, :, pl.ds(n0, 2 * QB)], p, preferred_element_type=F32)
    denom = acc[HEAD_DIM:HEAD_DIM + 1] + jnp.exp(sink - m)
    o_ref[0] = _to_token_rows(acc[:HEAD_DIM] / denom).astype(o_ref.dtype)


def _swa_attention(qT, k_aug, vT, band, sink):
    B, G, nqb = qT.shape[:3]
    per_bg = lambda a: pl.BlockSpec((1, 1) + a.shape[2:], lambda b, g, q: (b, g) + (0,) * (a.ndim - 2))
    per_g = lambda a: pl.BlockSpec((1,) + a.shape[1:], lambda b, g, q: (g,) + (0,) * (a.ndim - 1))
    return pl.pallas_call(
        _swa_kernel,
        grid=(B, G, nqb),
        in_specs=[pl.BlockSpec((1, 1, 1, AUG, LQ), lambda b, g, q: (b, g, q, 0, 0)),
                  per_bg(k_aug), per_bg(vT), per_g(band), per_g(sink)],
        out_specs=pl.BlockSpec((1, QB, HPG * HEAD_DIM), lambda b, g, q: (b, q, g)),
        out_shape=jax.ShapeDtypeStruct((B, nqb * QB, Q_WIDTH), BF16),
        compiler_params=_cparams(("arbitrary", "arbitrary", "arbitrary")),
        name="swa_attention",
    )(qT, k_aug, vT, band, sink)


def _layer_norm(y, g, b):
    mu = jnp.mean(y, axis=-1, keepdims=True)
    yc = y - mu
    var = jnp.mean(yc * yc, axis=-1, keepdims=True)
    return yc * lax.rsqrt(var + LN_EPS) * g + b


def _top2_of4(a, b, c, d):
    hi1, lo1 = jnp.maximum(a, b), jnp.minimum(a, b)
    hi2, lo2 = jnp.maximum(c, d), jnp.minimum(c, d)
    return jnp.maximum(hi1, hi2) + jnp.maximum(jnp.minimum(hi1, hi2), jnp.maximum(lo1, lo2))


def _route(s, sb):
    n = EXPERTS_PER_GROUP
    score = [_top2_of4(*sb[n * r:n * r + n]) for r in range(N_GROUPS)]
    best = functools.reduce(jnp.maximum, score)
    taken = jnp.zeros_like(best) > 1.0
    in_grp = []
    for r in range(N_GROUPS):
        pick = (score[r] == best) & jnp.logical_not(taken)
        in_grp.append(pick)
        taken = taken | pick
    gates = []
    for e in range(N_EXPERTS):
        r = e // n
        ahead = jnp.zeros_like(best)
        for f in range(n * r, n * r + n):
            if f != e:
                beats = (sb[f] >= sb[e]) if f < e else (sb[f] > sb[e])
                ahead = ahead + jnp.where(beats, 1.0, 0.0)
        gates.append(jnp.where(in_grp[r] & (ahead < 2.0), s[e], 0.0))
    total = functools.reduce(jnp.add, gates)
    return [g / total for g in gates]


def _oproj_kernel(o_ref, x_ref, w_ref, ga_ref, lg_ref, lb_ref, shf_ref, scf_ref, rw_ref, rb_ref,
                  x1_ref, h_ref, gate_ref):
    mix = jnp.dot(o_ref[0], w_ref[...], preferred_element_type=F32)
    x1 = _layer_norm(ALPHA * x_ref[0] + (1.0 + ga_ref[0]) * mix, lg_ref[...], lb_ref[...])
    x1_ref[0] = x1
    h = x1 * (1.0 + scf_ref[0]) + shf_ref[0]
    h_ref[0] = h.astype(BF16)
    logit = lax.dot_general(rw_ref[...], h, _NT, preferred_element_type=F32,
                            precision=lax.Precision.HIGHEST)
    aff = jax.nn.sigmoid(logit)
    biased = aff + rb_ref[...][:, 0:1]
    s = [aff[e:e + 1] for e in range(N_EXPERTS)]
    sb = [biased[e:e + 1] for e in range(N_EXPERTS)]
    gate_ref[0] = jnp.concatenate(_route(s, sb), axis=0)


def _oproj(o, x, w_out, g_a, ln_g, ln_b, sh_f, sc_f, router_w, router_b, ts=512):
    B, S, D = x.shape
    row = lambda a: a.reshape(1, D)
    per_b = pl.BlockSpec((1, 1, D), lambda b, s: (b, 0, 0))
    full = lambda a: pl.BlockSpec(a.shape, lambda b, s: (0,) * a.ndim)
    rwT = router_w.T
    rb = jnp.broadcast_to(router_b.reshape(N_EXPERTS, 1), (N_EXPERTS, 128))
    args = (o, x, w_out.astype(BF16), g_a.reshape(B, 1, D), row(ln_g), row(ln_b),
            sh_f.reshape(B, 1, D), sc_f.reshape(B, 1, D), rwT, rb)
    tile = pl.BlockSpec((1, ts, D), lambda b, s: (b, s, 0))
    return pl.pallas_call(
        _oproj_kernel,
        grid=(B, S // ts),
        in_specs=[tile, tile, full(args[2]), per_b, full(args[4]), full(args[5]), per_b, per_b,
                  full(rwT), full(rb)],
        out_specs=[tile, tile, pl.BlockSpec((1, N_EXPERTS, ts), lambda b, s: (b, 0, s))],
        out_shape=[jax.ShapeDtypeStruct((B, S, D), F32), jax.ShapeDtypeStruct((B, S, D), BF16),
                   jax.ShapeDtypeStruct((B, N_EXPERTS, S), F32)],
        compiler_params=_cparams(("arbitrary", "arbitrary")),
        name="oproj_ln_router",
    )(*args)


def _moe_kernel(h_ref, gate_ref, wgu_ref, wd_ref, x_ref, gf_ref, lg_ref, lb_ref, o_ref, acc_ref):
    e = pl.program_id(1)

    @pl.when(e == 0)
    def _():
        acc_ref[...] = jnp.zeros_like(acc_ref)

    gu = jnp.dot(h_ref[...], wgu_ref[0], preferred_element_type=F32)
    a, u = gu[:, :D_EXPERT], gu[:, D_EXPERT:]
    gates = gate_ref[...]
    lane = lax.broadcasted_iota(jnp.int32, gates.shape, 1)
    gcol = jnp.sum(jnp.where(lane == e, gates, 0.0), axis=1, keepdims=True)
    he = (a * jax.nn.sigmoid(a) * u * gcol).astype(BF16)
    acc_ref[...] += jnp.dot(he, wd_ref[0], preferred_element_type=F32)

    @pl.when(e == N_EXPERTS - 1)
    def _():
        y = ALPHA * x_ref[...] + (1.0 + gf_ref[0]) * acc_ref[...]
        o_ref[...] = _layer_norm(y, lg_ref[...], lb_ref[...])


def _moe(h, gates, w_gate, w_up, w_down, x, g_f, ln_g, ln_b, ts=1024):
    B, S, D = x.shape
    T = B * S
    wgu = jnp.concatenate([w_gate, w_up], axis=-1).astype(BF16)
    wd = w_down.astype(BF16)
    gl = jnp.pad(gates.transpose(0, 2, 1).reshape(T, N_EXPERTS), ((0, 0), (0, 128 - N_EXPERTS)))
    tile = pl.BlockSpec((ts, D), lambda t, e: (t, 0))
    spt = S // ts
    out = pl.pallas_call(
        _moe_kernel,
        grid=(T // ts, N_EXPERTS),
        in_specs=[tile, pl.BlockSpec((ts, 128), lambda t, e: (t, 0)),
                  pl.BlockSpec((1, D, 2 * D_EXPERT), lambda t, e: (e, 0, 0)),
                  pl.BlockSpec((1, D_EXPERT, D), lambda t, e: (e, 0, 0)),
                  tile, pl.BlockSpec((1, 1, D), lambda t, e: (t // spt, 0, 0)),
                  pl.BlockSpec((1, D), lambda t, e: (0, 0)), pl.BlockSpec((1, D), lambda t, e: (0, 0))],
        out_specs=tile,
        out_shape=jax.ShapeDtypeStruct((T, D), F32),
        scratch_shapes=[pltpu.VMEM((ts, D), F32)],
        compiler_params=_cparams(("arbitrary", "arbitrary")),
        name="moe_ln",
    )(h.reshape(T, D), gl, wgu, wd, x.reshape(T, D), g_f.reshape(B, 1, D), ln_g.reshape(1, D),
      ln_b.reshape(1, D))
    return out.reshape(B, S, D)


def _values_t(v):
    B, G, n, _ = v.shape
    return jnp.concatenate([v.transpose(0, 1, 3, 2), jnp.ones((B, G, MASK_LANES, n), v.dtype)], axis=2)


def _nsa_layer_attention(x, sh, sc, w_in, k_w1, k_w2, v_w1, v_w2, k_pos, v_pos, fb):
    B, S, _ = x.shape
    nc, n_slc = S // CMP_STRIDE, S // SLC_LEN
    qT, gatesT, ks_aug, vsT, kw_aug, vwT, cv = _nsa_proj(x, sh, sc, w_in)

    def to_chunks(t):
        t = t.reshape(B, nc, CMP_STRIDE, N_KV, HEAD_DIM).transpose(0, 3, 1, 2, 4)
        return t.reshape(B, N_KV * nc, CMP_STRIDE * HEAD_DIM)

    k_cmp = _compress(to_chunks(cv[..., :KV_WIDTH]), k_w1, k_w2, k_pos)
    v_cmp = _compress(to_chunks(cv[..., KV_WIDTH:]), v_w1, v_w2, v_pos)
    kc_aug = jnp.pad(k_cmp.astype(BF16), ((0, 0), (0, 0), (0, 0), (0, MASK_LANES)))
    vcT = _values_t(v_cmp.astype(BF16))

    c_start = np.arange(nc)[None, :] * CMP_STRIDE
    s_start = np.arange(MASK_LANES)[:, None] * SLC_LEN
    ovl = ((c_start < s_start + SLC_LEN) & (c_start + CMP_LEN > s_start)
           & (np.arange(nc)[None, :] < nc - 1) & (np.arange(MASK_LANES)[:, None] < n_slc))
    ovl = np.concatenate([np.zeros((HEAD_DIM, nc)), ovl.astype(np.float64)], axis=0).astype(np.float32)

    return _nsa_attention(qT, gatesT, kc_aug, vcT, ks_aug, vsT, kw_aug, vwT, jnp.asarray(ovl),
                          _cmp_band(fb, nc), _window_table(fb))


def _swa_layer_attention(x, sh, sc, w_q, w_kv, sinks, rel_bias, fb):
    qT, k_aug, vT = _swa_proj(x, sh, sc, w_q, w_kv)
    sink = sinks.astype(F32) - rel_bias.astype(F32)[N_BUCKETS - 1]
    sink = jnp.broadcast_to(sink.reshape(N_KV, 1, HPG, 1), (N_KV, 8, HPG, QB)).reshape(N_KV, 8, LQ)
    band = jnp.concatenate([_near_band(fb, SWA_WINDOW), _masked_rows(QB)], axis=1)
    return _swa_attention(qT, k_aug, vT, band, sink)


def kernel(x, c, nsa_w_in, cmp_k_w1, cmp_k_w2, cmp_v_w1, cmp_v_w2, cmp_k_pos, cmp_v_pos, nsa_w_out,
           swa_w_q, swa_sinks, swa_w_out, shared_w_kv, rel_bias, router_w, router_b, moe_w_gate,
           moe_w_up, moe_w_down, ada_w, ada_b, ln_g, ln_b):
    B, S, D = x.shape
    ada = _ada(c, ada_w, ada_b)
    fb = _shifted_bias(rel_bias)
    for layer in range(DEPTH):
        sh_a, sc_a, g_a, sh_f, sc_f, g_f = [ada[layer, :, i * D:(i + 1) * D] for i in range(6)]
        if layer == 0:
            o = _nsa_layer_attention(x, sh_a, sc_a, nsa_w_in[0], cmp_k_w1[0], cmp_k_w2[0], cmp_v_w1[0],
                                     cmp_v_w2[0], cmp_k_pos[0], cmp_v_pos[0], fb)
            w_out = nsa_w_out[0]
        else:
            o = _swa_layer_attention(x, sh_a, sc_a, swa_w_q[0], shared_w_kv, swa_sinks[0], rel_bias, fb)
            w_out = swa_w_out[0]
        x1, h, gates = _oproj(o, x, w_out, g_a, ln_g[layer, 0], ln_b[layer, 0], sh_f, sc_f,
                              router_w, router_b)
        x = _moe(h, gates, moe_w_gate[layer], moe_w_up[layer], moe_w_down[layer], x1, g_f,
                 ln_g[layer, 1], ln_b[layer, 1])
    return x
```

```python
import functools
import math

import numpy as np
import jax
import jax.numpy as jnp
from jax import lax
from jax.experimental import pallas as pl
from jax.experimental.pallas import tpu as pltpu

F32 = jnp.float32
BF16 = jnp.bfloat16

D_MODEL = 1024
HEAD_DIM = 64
N_HEADS = 16
N_KV = 4
HPG = 4
Q_WIDTH = N_HEADS * HEAD_DIM
KV_WIDTH = N_KV * HEAD_DIM
CMP_LEN = 32
CMP_STRIDE = 16
CMP_HIDDEN = 256
SLC_LEN = 64
N_SELECT = 16
NSA_WINDOW = 512
SWA_WINDOW = 128
N_BUCKETS = 32
MAX_DISTANCE = 128
N_EXPERTS = 16
N_GROUPS = 4
EXPERTS_PER_GROUP = 4
D_EXPERT = 256
DEPTH = 2
ALPHA = (2.0 * DEPTH) ** 0.25
LN_EPS = 1e-5
NEG = -1e30
ATTN_SCALE = HEAD_DIM ** -0.5

NSA_QB = 256
SWA_QB = 128
SWA_QPS = 4
AUG = 128
MASK_LANES = AUG - HEAD_DIM
CMP_LEAD = 16
FAR_TK = 256
N_GATE = 3 * N_HEADS
VMEM_LIMIT = 56 * 1024 * 1024

_NT = (((1,), (1,)), ((), ()))


def _cparams(sem):
    return pltpu.CompilerParams(dimension_semantics=sem, vmem_limit_bytes=VMEM_LIMIT)


def _ada_kernel(c_ref, w_ref, b_ref, o_ref):
    c = c_ref[...]
    cond = c * jax.nn.sigmoid(c)
    o_ref[0] = jnp.dot(cond, w_ref[0], preferred_element_type=F32,
                       precision=lax.Precision.HIGHEST) + b_ref[0]


def _ada(c, ada_w, ada_b):
    B, D = c.shape
    n6 = ada_w.shape[-1]
    tn = 1536
    return pl.pallas_call(
        _ada_kernel,
        grid=(DEPTH, n6 // tn),
        in_specs=[pl.BlockSpec((B, D), lambda l, n: (0, 0)),
                  pl.BlockSpec((1, D, tn), lambda l, n: (l, 0, n)),
                  pl.BlockSpec((1, 1, tn), lambda l, n: (l, 0, n))],
        out_specs=pl.BlockSpec((1, B, tn), lambda l, n: (l, 0, n)),
        out_shape=jax.ShapeDtypeStruct((DEPTH, B, n6), F32),
        compiler_params=_cparams(("arbitrary", "arbitrary")),
        name="ada",
    )(c, ada_w, ada_b.reshape(DEPTH, 1, n6))


def _store_q_blocks(q_ref, qT, ts, qb):
    zeros = jnp.zeros((MASK_LANES, HPG * qb), BF16)
    for g in range(N_KV):
        for k in range(ts // qb):
            for j in range(HPG):
                r0 = (g * HPG + j) * HEAD_DIM
                q_ref[0, g, k, 0:HEAD_DIM, j * qb:(j + 1) * qb] = (
                    qT[r0:r0 + HEAD_DIM, k * qb:(k + 1) * qb].astype(BF16))
            q_ref[0, g, k, HEAD_DIM:AUG, :] = zeros


def _store_vt(vt_ref, vT, ts):
    ones = jnp.ones((MASK_LANES, ts), BF16)
    for g in range(N_KV):
        vt_ref[0, g, 0:HEAD_DIM, :] = vT[g * HEAD_DIM:(g + 1) * HEAD_DIM].astype(BF16)
        vt_ref[0, g, HEAD_DIM:AUG, :] = ones


def _nsa_proj_kernel(x_ref, sh_ref, sc_ref, wt_ref, ws_ref, q_ref, gt_ref, ks_ref, vst_ref, kw_ref,
                     vwt_ref, cv_ref, *, ts):
    s = pl.program_id(1)
    h = (x_ref[0] * (1.0 + sc_ref[0]) + sh_ref[0]).astype(BF16)
    yT = lax.dot_general(wt_ref[...], h, _NT, preferred_element_type=F32)
    y = jnp.dot(h, ws_ref[...], preferred_element_type=F32)
    _store_q_blocks(q_ref, yT[:Q_WIDTH], ts, NSA_QB)
    _store_vt(vst_ref, yT[Q_WIDTH:Q_WIDTH + KV_WIDTH], ts)
    _store_vt(vwt_ref, yT[Q_WIDTH + KV_WIDTH:Q_WIDTH + 2 * KV_WIDTH], ts)
    gates = jax.nn.sigmoid(yT[Q_WIDTH + 2 * KV_WIDTH:])
    for k in range(ts // NSA_QB):
        gt_ref[0, k] = gates[:, k * NSA_QB:(k + 1) * NSA_QB]
    lane = lax.broadcasted_iota(jnp.int32, (ts, AUG), 1)
    tok = s * ts + lax.broadcasted_iota(jnp.int32, (ts, AUG), 0)
    onehot = (lane - HEAD_DIM) == tok // SLC_LEN
    for g in range(N_KV):
        ks_ref[0, g] = jnp.where(onehot, 1.0, y[:, g * AUG:(g + 1) * AUG]).astype(BF16)
        kw_ref[0, g] = y[:, (N_KV + g) * AUG:(N_KV + g + 1) * AUG].astype(BF16)
    cv_ref[0] = y[:, 2 * N_KV * AUG:].astype(BF16)


def _pad_heads(w):
    D = w.shape[0]
    w = w.reshape(D, N_KV, HEAD_DIM)
    return jnp.pad(w, ((0, 0), (0, 0), (0, MASK_LANES))).reshape(D, N_KV * AUG)


def _nsa_proj(x, shift, scale, w_in, ts=512):
    B, S, D = x.shape
    nqb, lq = S // NSA_QB, HPG * NSA_QB
    cols = lambda i: w_in[:, Q_WIDTH + i * KV_WIDTH:Q_WIDTH + (i + 1) * KV_WIDTH]
    w_kc, w_vc, w_ksl, w_vsl, w_kw, w_vw = [cols(i) for i in range(6)]
    wt = jnp.concatenate([w_in[:, :Q_WIDTH] * ATTN_SCALE, w_vsl, w_vw,
                          w_in[:, Q_WIDTH + 6 * KV_WIDTH:]], axis=1).T.astype(BF16)
    ws = jnp.concatenate([_pad_heads(w_ksl), _pad_heads(w_kw), w_kc, w_vc], axis=1).astype(BF16)
    per_b = pl.BlockSpec((1, 1, D), lambda b, s: (b, 0, 0))
    kspec = pl.BlockSpec((1, N_KV, ts, AUG), lambda b, s: (b, 0, s, 0))
    vspec = pl.BlockSpec((1, N_KV, AUG, ts), lambda b, s: (b, 0, 0, s))
    k_shape = jax.ShapeDtypeStruct((B, N_KV, S, AUG), BF16)
    v_shape = jax.ShapeDtypeStruct((B, N_KV, AUG, S), BF16)
    return pl.pallas_call(
        functools.partial(_nsa_proj_kernel, ts=ts),
        grid=(B, S // ts),
        in_specs=[pl.BlockSpec((1, ts, D), lambda b, s: (b, s, 0)), per_b, per_b,
                  pl.BlockSpec(wt.shape, lambda b, s: (0, 0)), pl.BlockSpec(ws.shape, lambda b, s: (0, 0))],
        out_specs=[pl.BlockSpec((1, N_KV, ts // NSA_QB, AUG, lq), lambda b, s: (b, 0, s, 0, 0)),
                   pl.BlockSpec((1, ts // NSA_QB, N_GATE, NSA_QB), lambda b, s: (b, s, 0, 0)),
                   kspec, vspec, kspec, vspec,
                   pl.BlockSpec((1, ts, 2 * KV_WIDTH), lambda b, s: (b, s, 0))],
        out_shape=[jax.ShapeDtypeStruct((B, N_KV, nqb, AUG, lq), BF16),
                   jax.ShapeDtypeStruct((B, nqb, N_GATE, NSA_QB), F32),
                   k_shape, v_shape, k_shape, v_shape,
                   jax.ShapeDtypeStruct((B, S, 2 * KV_WIDTH), BF16)],
        compiler_params=_cparams(("arbitrary", "arbitrary")),
        name="nsa_proj",
    )(x, shift.reshape(B, 1, D), scale.reshape(B, 1, D), wt, ws)


def _swa_proj_kernel(x_ref, sh_ref, sc_ref, wq_ref, wk_ref, wv_ref, q_ref, k_ref, vt_ref, *, ts):
    x = x_ref[0]
    h = (x * (1.0 + sc_ref[0]) + sh_ref[0]).astype(BF16)
    xb = x.astype(BF16)
    _store_q_blocks(q_ref, lax.dot_general(wq_ref[...], h, _NT, preferred_element_type=F32), ts, SWA_QB)
    _store_vt(vt_ref, lax.dot_general(wv_ref[...], xb, _NT, preferred_element_type=F32), ts)
    k = jnp.dot(xb, wk_ref[...], preferred_element_type=F32)
    for g in range(N_KV):
        k_ref[0, g] = k[:, g * AUG:(g + 1) * AUG].astype(BF16)


def _swa_proj(x, shift, scale, w_q, w_kv, ts=512):
    B, S, D = x.shape
    lq = HPG * SWA_QB
    wq = (w_q * ATTN_SCALE).T.astype(BF16)
    wk = _pad_heads(w_kv[:, :KV_WIDTH]).astype(BF16)
    wv = w_kv[:, KV_WIDTH:].T.astype(BF16)
    per_b = pl.BlockSpec((1, 1, D), lambda b, s: (b, 0, 0))
    full = lambda a: pl.BlockSpec(a.shape, lambda b, s: (0, 0))
    return pl.pallas_call(
        functools.partial(_swa_proj_kernel, ts=ts),
        grid=(B, S // ts),
        in_specs=[pl.BlockSpec((1, ts, D), lambda b, s: (b, s, 0)), per_b, per_b, full(wq), full(wk), full(wv)],
        out_specs=[pl.BlockSpec((1, N_KV, ts // SWA_QB, AUG, lq), lambda b, s: (b, 0, s, 0, 0)),
                   pl.BlockSpec((1, N_KV, ts, AUG), lambda b, s: (b, 0, s, 0)),
                   pl.BlockSpec((1, N_KV, AUG, ts), lambda b, s: (b, 0, 0, s))],
        out_shape=[jax.ShapeDtypeStruct((B, N_KV, S // SWA_QB, AUG, lq), BF16),
                   jax.ShapeDtypeStruct((B, N_KV, S, AUG), BF16),
                   jax.ShapeDtypeStruct((B, N_KV, AUG, S), BF16)],
        compiler_params=_cparams(("arbitrary", "arbitrary")),
        name="swa_proj",
    )(x, shift.reshape(B, 1, D), scale.reshape(B, 1, D), wq, wk, wv)


def _gelu_tanh(x):
    return 0.5 * x * (1.0 + jnp.tanh(math.sqrt(2.0 / math.pi) * (x + 0.044715 * (x * x * x))))


def _compress_kernel(ch_ref, w1_ref, w2_ref, pos_ref, o_ref, *, nc):
    half = CMP_STRIDE * HEAD_DIM
    ch = ch_ref[0]
    w1 = w1_ref[...]
    top = jnp.dot(ch, w1[:half], preferred_element_type=F32)
    bot = jnp.dot(ch, w1[half:], preferred_element_type=F32)
    posw = jnp.dot(pos_ref[...], w1, preferred_element_type=F32)[0:1]
    w2 = w2_ref[...]
    for g in range(N_KV):
        bot_g = pltpu.roll(bot[g * nc:(g + 1) * nc], nc - 1, 0)
        pre = top[g * nc:(g + 1) * nc] + bot_g + posw
        o_ref[0, g] = jnp.dot(_gelu_tanh(pre).astype(BF16), w2, preferred_element_type=F32)


def _compress(chunks, w1, w2, pos):
    B, gn, ck = chunks.shape
    nc = gn // N_KV
    pos8 = jnp.broadcast_to(pos.reshape(1, CMP_LEN * HEAD_DIM), (8, CMP_LEN * HEAD_DIM)).astype(BF16)
    return pl.pallas_call(
        functools.partial(_compress_kernel, nc=nc),
        grid=(B,),
        in_specs=[pl.BlockSpec((1, gn, ck), lambda b: (b, 0, 0)),
                  pl.BlockSpec(w1.shape, lambda b: (0, 0)),
                  pl.BlockSpec(w2.shape, lambda b: (0, 0)),
                  pl.BlockSpec(pos8.shape, lambda b: (0, 0))],
        out_specs=pl.BlockSpec((1, N_KV, nc, HEAD_DIM), lambda b: (b, 0, 0, 0)),
        out_shape=jax.ShapeDtypeStruct((B, N_KV, nc, HEAD_DIM), F32),
        compiler_params=_cparams(("arbitrary",)),
        name="compress",
    )(chunks, w1.astype(BF16), w2.astype(BF16), pos8)


def _bucket_np(d):
    d = np.maximum(np.asarray(d, np.int64), 0)
    max_exact = N_BUCKETS // 2
    large = max_exact + (np.log(np.maximum(d, 1).astype(np.float32) / np.float32(max_exact))
                         / np.float32(math.log(MAX_DISTANCE / max_exact))
                         * np.float32(N_BUCKETS - max_exact)).astype(np.int32)
    large = np.minimum(large, N_BUCKETS - 1)
    return np.where(d < max_exact, d, large).astype(np.int32)


def _shifted_bias(rel_table):
    t = rel_table.astype(F32)
    return (t[_bucket_np(np.arange(MAX_DISTANCE))] - t[N_BUCKETS - 1][None, :]).T


def _to_group_lanes(m):
    H, K, Q = m.shape
    return m.reshape(N_KV, HPG, K, Q).transpose(0, 2, 1, 3).reshape(N_KV, K, HPG * Q)


def _bias_of_dist(fb, dist, window):
    ok = (dist >= 0) & (dist < window)
    idx = np.clip(dist, 0, MAX_DISTANCE - 1)
    return jnp.where(ok[None], jnp.where((dist >= MAX_DISTANCE)[None], 0.0, fb[:, idx]), NEG)


def _near_band(fb, window, qb):
    H = fb.shape[0]
    nv = 3 * qb - 1
    v = _bias_of_dist(fb, np.arange(nv) - (qb - 1), window)
    rows = 2 * qb
    flat = jnp.tile(v, (1, rows + 1))[:, :rows * (nv + 1)]
    m = flat.reshape(H, rows, nv + 1)[:, ::-1, :qb]
    return _to_group_lanes(m)


def _masked_rows(n, qb):
    return jnp.full((N_KV, n, HPG * qb), NEG, F32)


def _window_table(fb, qb):
    kk = np.arange(qb)[:, None]
    qi = np.arange(HPG * qb)[None, :] % qb
    edge = np.where(kk > qi, 0.0, NEG).astype(np.float32)
    edge = jnp.broadcast_to(jnp.asarray(edge), (N_KV, qb, HPG * qb))
    mid = jnp.zeros((N_KV, NSA_WINDOW - 2 * qb, HPG * qb), F32)
    return jnp.concatenate([edge, mid, _near_band(fb, NSA_WINDOW, qb), _masked_rows(NSA_WINDOW, qb)], axis=1)


def _cmp_band_rows(qb):
    return CMP_LEAD + qb // CMP_STRIDE


def _cmp_band(fb, nc, qb):
    m = np.arange(_cmp_band_rows(qb))[:, None]
    i = np.arange(qb)[None, :]
    dist = i + CMP_STRIDE * CMP_LEAD - CMP_STRIDE * m - (CMP_LEN - 1)
    band = _to_group_lanes(_bias_of_dist(fb, dist, 1 << 30))
    return jnp.concatenate([jnp.zeros((N_KV, nc, HPG * qb), F32), band, _masked_rows(nc, qb)], axis=1)


def _to_token_rows(oT, qb):
    pair = lambda a: jnp.concatenate([oT[:, a * qb:(a + 1) * qb], oT[:, (a + 1) * qb:(a + 2) * qb]], axis=0)
    return jnp.concatenate([pair(0).T, pair(2).T], axis=1)


def _nsa_kernel(q_ref, gt_ref, kc_ref, vct_ref, ks_ref, vst_ref, kw_ref, vwt_ref, ovl_ref, cb_ref,
                wt_ref, o_ref, acc_ref, ow_ref, sa_ref, sb_ref, pa_ref, pb_ref, *, nc):
    QB, LQ = NSA_QB, HPG * NSA_QB
    g = pl.program_id(1)
    qb = pl.program_id(2)
    s0 = qb * QB
    qT = q_ref[0, 0, 0]

    def gate(branch):
        rows = [gt_ref[0, 0, pl.ds(branch * N_HEADS + g * HPG + j, 1), :] for j in range(HPG)]
        return jnp.concatenate(rows, axis=1)

    win_keys = NSA_WINDOW + QB
    w0 = pl.multiple_of(jnp.maximum(s0 - NSA_WINDOW, 0), QB)
    t0 = pl.multiple_of(w0 - (s0 - NSA_WINDOW), QB)
    sw = (jnp.dot(kw_ref[0, 0, pl.ds(w0, win_keys), :], qT, preferred_element_type=F32)
          + wt_ref[0, pl.ds(t0, win_keys), :])
    m_w = jnp.max(sw, axis=0, keepdims=True)
    p_w = jnp.exp(sw - m_w).astype(BF16)
    acc_w = jnp.dot(vwt_ref[0, 0, :, pl.ds(w0, win_keys)], p_w, preferred_element_type=F32)
    ow_ref[...] = gate(2) * (acc_w[:HEAD_DIM] / acc_w[HEAD_DIM:HEAD_DIM + 1])

    sc = jnp.dot(kc_ref[0, 0], qT, preferred_element_type=F32)
    cstart = pl.multiple_of(nc + CMP_LEAD - (QB // CMP_STRIDE) * qb, 8)
    sc = sc + cb_ref[0, pl.ds(cstart, nc), :]
    m_c = jnp.max(sc, axis=0, keepdims=True)
    e_c = jnp.exp(sc - m_c)
    l_c = jnp.sum(e_c, axis=0, keepdims=True)
    p_c = jnp.where(sc > 0.1 * NEG, e_c / l_c, 0.0)
    o_cmp = jnp.dot(vct_ref[0, 0], p_c.astype(BF16), preferred_element_type=F32)[:HEAD_DIM]

    p_sum = functools.reduce(jnp.add, [p_c[:, j * QB:(j + 1) * QB] for j in range(HPG)])
    imp = jnp.dot(ovl_ref[...], p_sum, preferred_element_type=F32,
                  precision=lax.Precision.HIGHEST)[HEAD_DIM:]
    blk = lax.broadcasted_iota(jnp.int32, (SLC_LEN, QB), 0)
    tq = s0 + lax.broadcasted_iota(jnp.int32, (SLC_LEN, QB), 1)
    cur = tq // SLC_LEN
    forced = (blk == 0) | (blk == cur) | (blk == cur - 1)
    future = blk > cur
    val = jnp.where(forced, 1e9, jnp.where(future, -1e9, imp))
    vals = [val[8 * a:8 * a + 8] for a in range(8)]
    sub = lax.broadcasted_iota(jnp.int32, (8, QB), 0)
    cnts = [jnp.zeros((8, QB), F32) for _ in range(8)]
    for jp in range(SLC_LEN):
        ap, r = divmod(jp, 8)
        rv = jnp.broadcast_to(vals[ap][r:r + 1], (8, QB))
        for a in range(8):
            gt = jnp.where(rv > vals[a], 1.0, 0.0)
            ge = jnp.where(rv >= vals[a], 1.0, 0.0)
            if a < ap:
                inc = gt
            elif a > ap:
                inc = ge
            else:
                inc = jnp.where(sub > r, ge, gt)
            cnts[a] = cnts[a] + inc
    cnt = jnp.concatenate(cnts, axis=0)
    allowed = (cnt < float(N_SELECT)) & (blk <= cur)
    near_blk = blk >= (s0 - QB) // SLC_LEN
    neg_near = jnp.where(allowed, 0.0, NEG).astype(BF16)
    neg_far = jnp.where(allowed & jnp.logical_not(near_blk), 0.0, NEG).astype(BF16)
    q_near = jnp.concatenate([qT[:HEAD_DIM], jnp.tile(neg_near, (1, HPG))], axis=0)
    q_far = jnp.concatenate([qT[:HEAD_DIM], jnp.tile(neg_far, (1, HPG))], axis=0)

    n0 = pl.multiple_of(jnp.maximum(s0 - QB, 0), QB)
    b0 = pl.multiple_of(NSA_WINDOW - QB + n0 - (s0 - QB), QB)
    s_near = (jnp.dot(ks_ref[0, 0, pl.ds(n0, 2 * QB), :], q_near, preferred_element_type=F32)
              + wt_ref[0, pl.ds(b0, 2 * QB), :])
    m_near = jnp.max(s_near, axis=0, keepdims=True)
    p_near = jnp.exp(s_near - m_near).astype(BF16)
    acc_ref[...] = jnp.dot(vst_ref[0, 0, :, pl.ds(n0, 2 * QB)], p_near, preferred_element_type=F32)

    last_tile = ks_ref.shape[2] // FAR_TK - 1

    def tile_start(t):
        return pl.multiple_of(jnp.minimum(t, last_tile) * FAR_TK, FAR_TK)

    def qk(t):
        return jnp.dot(ks_ref[0, 0, pl.ds(tile_start(t), FAR_TK), :], q_far, preferred_element_type=F32)

    def softmax_update(s, m_run):
        m_new = jnp.maximum(m_run, jnp.max(s, axis=0, keepdims=True))
        return jnp.exp(s - m_new).astype(BF16), jnp.exp(m_run - m_new), m_new

    def pv_update(t, p, alpha):
        pv = jnp.dot(vst_ref[0, 0, :, pl.ds(tile_start(t), FAR_TK)], p, preferred_element_type=F32)
        acc_ref[...] = acc_ref[...] * alpha + pv

    sa_ref[...] = qk(0)
    sb_ref[...] = qk(1)
    p0, alpha0, m0 = softmax_update(sa_ref[...], m_near)
    pa_ref[...] = p0

    def far_body(i, carry):
        alpha_a, m_run = carry
        t = 2 * i
        pv_update(t, pa_ref[...], alpha_a)
        p_b, alpha_b, m_run = softmax_update(sb_ref[...], m_run)
        pb_ref[...] = p_b
        sa_ref[...] = qk(t + 2)
        pv_update(t + 1, pb_ref[...], alpha_b)
        p_a, alpha_a, m_run = softmax_update(sa_ref[...], m_run)
        pa_ref[...] = p_a
        sb_ref[...] = qk(t + 3)
        return alpha_a, m_run

    n_far = (jnp.maximum(s0 - QB, 0) + FAR_TK - 1) // FAR_TK
    lax.fori_loop(0, (n_far + 1) // 2, far_body, (alpha0, m0))
    acc = acc_ref[...]
    o_sel = acc[:HEAD_DIM] / acc[HEAD_DIM:HEAD_DIM + 1]

    o = gate(0) * o_cmp + gate(1) * o_sel + ow_ref[...]
    o_ref[0] = _to_token_rows(o, QB).astype(o_ref.dtype)


def _nsa_attention(qT, gatesT, kc_aug, vcT, ks_aug, vsT, kw_aug, vwT, ovl, cband, wtable):
    B, G, nqb = qT.shape[:3]
    QB, LQ = NSA_QB, HPG * NSA_QB
    nc = kc_aug.shape[2]
    per_bg = lambda a: pl.BlockSpec((1, 1) + a.shape[2:], lambda b, g, q: (b, g) + (0,) * (a.ndim - 2))
    per_g = lambda a: pl.BlockSpec((1,) + a.shape[1:], lambda b, g, q: (g,) + (0,) * (a.ndim - 1))
    return pl.pallas_call(
        functools.partial(_nsa_kernel, nc=nc),
        grid=(B, G, nqb),
        in_specs=[pl.BlockSpec((1, 1, 1, AUG, LQ), lambda b, g, q: (b, g, q, 0, 0)),
                  pl.BlockSpec((1, 1, N_GATE, QB), lambda b, g, q: (b, q, 0, 0)),
                  per_bg(kc_aug), per_bg(vcT), per_bg(ks_aug), per_bg(vsT),
                  per_bg(kw_aug), per_bg(vwT), pl.BlockSpec(ovl.shape, lambda b, g, q: (0, 0)),
                  per_g(cband), per_g(wtable)],
        out_specs=pl.BlockSpec((1, QB, HPG * HEAD_DIM), lambda b, g, q: (b, q, g)),
        out_shape=jax.ShapeDtypeStruct((B, nqb * QB, Q_WIDTH), BF16),
        scratch_shapes=[pltpu.VMEM((AUG, LQ), F32), pltpu.VMEM((HEAD_DIM, LQ), F32),
                        pltpu.VMEM((FAR_TK, LQ), F32), pltpu.VMEM((FAR_TK, LQ), F32),
                        pltpu.VMEM((FAR_TK, LQ), BF16), pltpu.VMEM((FAR_TK, LQ), BF16)],
        compiler_params=_cparams(("arbitrary", "arbitrary", "arbitrary")),
        name="nsa_attention",
    )(qT, gatesT, kc_aug, vcT, ks_aug, vsT, kw_aug, vwT, ovl, cband, wtable)


def _swa_kernel(q_ref, k_ref, vt_ref, band_ref, sink_ref, o_ref):
    QB = SWA_QB
    sink = sink_ref[0, 0:1]
    for i in range(SWA_QPS):
        s0 = (pl.program_id(2) * SWA_QPS + i) * QB
        n0 = pl.multiple_of(jnp.maximum(s0 - QB, 0), QB)
        b0 = pl.multiple_of(n0 - (s0 - QB), QB)
        s = (jnp.dot(k_ref[0, 0, pl.ds(n0, 2 * QB), :], q_ref[0, 0, i], preferred_element_type=F32)
             + band_ref[0, pl.ds(b0, 2 * QB), :])
        m = jnp.maximum(jnp.max(s, axis=0, keepdims=True), sink)
        p = jnp.exp(s - m).astype(BF16)
        acc = jnp.dot(vt_ref[0, 0, :, pl.ds(n0, 2 * QB)], p, preferred_element_type=F32)
        denom = acc[HEAD_DIM:HEAD_DIM + 1] + jnp.exp(sink - m)
        o_ref[0, i * QB:(i + 1) * QB, :] = _to_token_rows(acc[:HEAD_DIM] / denom, QB).astype(o_ref.dtype)


def _swa_attention(qT, k_aug, vT, band, sink):
    B, G, nqb = qT.shape[:3]
    QB, LQ = SWA_QB, HPG * SWA_QB
    per_bg = lambda a: pl.BlockSpec((1, 1) + a.shape[2:], lambda b, g, q: (b, g) + (0,) * (a.ndim - 2))
    per_g = lambda a: pl.BlockSpec((1,) + a.shape[1:], lambda b, g, q: (g,) + (0,) * (a.ndim - 1))
    return pl.pallas_call(
        _swa_kernel,
        grid=(B, G, nqb // SWA_QPS),
        in_specs=[pl.BlockSpec((1, 1, SWA_QPS, AUG, LQ), lambda b, g, q: (b, g, q, 0, 0)),
                  per_bg(k_aug), per_bg(vT), per_g(band), per_g(sink)],
        out_specs=pl.BlockSpec((1, SWA_QPS * QB, HPG * HEAD_DIM), lambda b, g, q: (b, q, g)),
        out_shape=jax.ShapeDtypeStruct((B, nqb * QB, Q_WIDTH), BF16),
        compiler_params=_cparams(("arbitrary", "arbitrary", "arbitrary")),
        name="swa_attention",
    )(qT, k_aug, vT, band, sink)


def _layer_norm(y, g, b):
    mu = jnp.mean(y, axis=-1, keepdims=True)
    yc = y - mu
    var = jnp.mean(yc * yc, axis=-1, keepdims=True)
    return yc * lax.rsqrt(var + LN_EPS) * g + b


def _top2_of4(a, b, c, d):
    hi1, lo1 = jnp.maximum(a, b), jnp.minimum(a, b)
    hi2, lo2 = jnp.maximum(c, d), jnp.minimum(c, d)
    return jnp.maximum(hi1, hi2) + jnp.maximum(jnp.minimum(hi1, hi2), jnp.maximum(lo1, lo2))


def _route(s, sb):
    n = EXPERTS_PER_GROUP
    score = [_top2_of4(*sb[n * r:n * r + n]) for r in range(N_GROUPS)]
    best = functools.reduce(jnp.maximum, score)
    taken = jnp.zeros_like(best) > 1.0
    in_grp = []
    for r in range(N_GROUPS):
        pick = (score[r] == best) & jnp.logical_not(taken)
        in_grp.append(pick)
        taken = taken | pick
    gates = []
    for e in range(N_EXPERTS):
        r = e // n
        ahead = jnp.zeros_like(best)
        for f in range(n * r, n * r + n):
            if f != e:
                beats = (sb[f] >= sb[e]) if f < e else (sb[f] > sb[e])
                ahead = ahead + jnp.where(beats, 1.0, 0.0)
        gates.append(jnp.where(in_grp[r] & (ahead < 2.0), s[e], 0.0))
    total = functools.reduce(jnp.add, gates)
    return [g / total for g in gates]


def _oproj_kernel(o_ref, x_ref, w_ref, ga_ref, lg_ref, lb_ref, shf_ref, scf_ref, rw_ref, rb_ref,
                  x1_ref, h_ref, gate_ref):
    mix = jnp.dot(o_ref[0], w_ref[...], preferred_element_type=F32)
    x1 = _layer_norm(ALPHA * x_ref[0] + (1.0 + ga_ref[0]) * mix, lg_ref[...], lb_ref[...])
    x1_ref[0] = x1
    h = x1 * (1.0 + scf_ref[0]) + shf_ref[0]
    h_ref[0] = h.astype(BF16)
    logit = lax.dot_general(rw_ref[...], h, _NT, preferred_element_type=F32,
                            precision=lax.Precision.HIGHEST)
    aff = jax.nn.sigmoid(logit)
    biased = aff + rb_ref[...][:, 0:1]
    s = [aff[e:e + 1] for e in range(N_EXPERTS)]
    sb = [biased[e:e + 1] for e in range(N_EXPERTS)]
    gate_ref[0] = jnp.concatenate(_route(s, sb), axis=0)


def _oproj(o, x, w_out, g_a, ln_g, ln_b, sh_f, sc_f, router_w, router_b, ts=512):
    B, S, D = x.shape
    row = lambda a: a.reshape(1, D)
    per_b = pl.BlockSpec((1, 1, D), lambda b, s: (b, 0, 0))
    full = lambda a: pl.BlockSpec(a.shape, lambda b, s: (0,) * a.ndim)
    rwT = router_w.T
    rb = jnp.broadcast_to(router_b.reshape(N_EXPERTS, 1), (N_EXPERTS, 128))
    args = (o, x, w_out.astype(BF16), g_a.reshape(B, 1, D), row(ln_g), row(ln_b),
            sh_f.reshape(B, 1, D), sc_f.reshape(B, 1, D), rwT, rb)
    tile = pl.BlockSpec((1, ts, D), lambda b, s: (b, s, 0))
    return pl.pallas_call(
        _oproj_kernel,
        grid=(B, S // ts),
        in_specs=[tile, tile, full(args[2]), per_b, full(args[4]), full(args[5]), per_b, per_b,
                  full(rwT), full(rb)],
        out_specs=[tile, tile, pl.BlockSpec((1, N_EXPERTS, ts), lambda b, s: (b, 0, s))],
        out_shape=[jax.ShapeDtypeStruct((B, S, D), F32), jax.ShapeDtypeStruct((B, S, D), BF16),
                   jax.ShapeDtypeStruct((B, N_EXPERTS, S), F32)],
        compiler_params=_cparams(("arbitrary", "arbitrary")),
        name="oproj_ln_router",
    )(*args)


def _moe_kernel(h_ref, gate_ref, wgu_ref, wd_ref, x_ref, gf_ref, lg_ref, lb_ref, o_ref, acc_ref):
    e = pl.program_id(1)

    @pl.when(e == 0)
    def _():
        acc_ref[...] = jnp.zeros_like(acc_ref)

    gu = jnp.dot(h_ref[...], wgu_ref[0], preferred_element_type=F32)
    a, u = gu[:, :D_EXPERT], gu[:, D_EXPERT:]
    gates = gate_ref[...]
    lane = lax.broadcasted_iota(jnp.int32, gates.shape, 1)
    gcol = jnp.sum(jnp.where(lane == e, gates, 0.0), axis=1, keepdims=True)
    he = (a * jax.nn.sigmoid(a) * u * gcol).astype(BF16)
    acc_ref[...] += jnp.dot(he, wd_ref[0], preferred_element_type=F32)

    @pl.when(e == N_EXPERTS - 1)
    def _():
        y = ALPHA * x_ref[...] + (1.0 + gf_ref[0]) * acc_ref[...]
        o_ref[...] = _layer_norm(y, lg_ref[...], lb_ref[...])


def _moe(h, gates, w_gate, w_up, w_down, x, g_f, ln_g, ln_b, ts=1024):
    B, S, D = x.shape
    T = B * S
    wgu = jnp.concatenate([w_gate, w_up], axis=-1).astype(BF16)
    wd = w_down.astype(BF16)
    gl = jnp.pad(gates.transpose(0, 2, 1).reshape(T, N_EXPERTS), ((0, 0), (0, 128 - N_EXPERTS)))
    tile = pl.BlockSpec((ts, D), lambda t, e: (t, 0))
    spt = S // ts
    out = pl.pallas_call(
        _moe_kernel,
        grid=(T // ts, N_EXPERTS),
        in_specs=[tile, pl.BlockSpec((ts, 128), lambda t, e: (t, 0)),
                  pl.BlockSpec((1, D, 2 * D_EXPERT), lambda t, e: (e, 0, 0)),
                  pl.BlockSpec((1, D_EXPERT, D), lambda t, e: (e, 0, 0)),
                  tile, pl.BlockSpec((1, 1, D), lambda t, e: (t // spt, 0, 0)),
                  pl.BlockSpec((1, D), lambda t, e: (0, 0)), pl.BlockSpec((1, D), lambda t, e: (0, 0))],
        out_specs=tile,
        out_shape=jax.ShapeDtypeStruct((T, D), F32),
        scratch_shapes=[pltpu.VMEM((ts, D), F32)],
        compiler_params=_cparams(("arbitrary", "arbitrary")),
        name="moe_ln",
    )(h.reshape(T, D), gl, wgu, wd, x.reshape(T, D), g_f.reshape(B, 1, D), ln_g.reshape(1, D),
      ln_b.reshape(1, D))
    return out.reshape(B, S, D)


def _values_t(v):
    B, G, n, _ = v.shape
    return jnp.concatenate([v.transpose(0, 1, 3, 2), jnp.ones((B, G, MASK_LANES, n), v.dtype)], axis=2)


def _nsa_layer_attention(x, sh, sc, w_in, k_w1, k_w2, v_w1, v_w2, k_pos, v_pos, fb):
    B, S, _ = x.shape
    nc, n_slc = S // CMP_STRIDE, S // SLC_LEN
    qT, gatesT, ks_aug, vsT, kw_aug, vwT, cv = _nsa_proj(x, sh, sc, w_in)

    def to_chunks(t):
        t = t.reshape(B, nc, CMP_STRIDE, N_KV, HEAD_DIM).transpose(0, 3, 1, 2, 4)
        return t.reshape(B, N_KV * nc, CMP_STRIDE * HEAD_DIM)

    k_cmp = _compress(to_chunks(cv[..., :KV_WIDTH]), k_w1, k_w2, k_pos)
    v_cmp = _compress(to_chunks(cv[..., KV_WIDTH:]), v_w1, v_w2, v_pos)
    kc_aug = jnp.pad(k_cmp.astype(BF16), ((0, 0), (0, 0), (0, 0), (0, MASK_LANES)))
    vcT = _values_t(v_cmp.astype(BF16))

    c_start = np.arange(nc)[None, :] * CMP_STRIDE
    s_start = np.arange(MASK_LANES)[:, None] * SLC_LEN
    ovl = ((c_start < s_start + SLC_LEN) & (c_start + CMP_LEN > s_start)
           & (np.arange(nc)[None, :] < nc - 1) & (np.arange(MASK_LANES)[:, None] < n_slc))
    ovl = np.concatenate([np.zeros((HEAD_DIM, nc)), ovl.astype(np.float64)], axis=0).astype(np.float32)

    return _nsa_attention(qT, gatesT, kc_aug, vcT, ks_aug, vsT, kw_aug, vwT, jnp.asarray(ovl),
                          _cmp_band(fb, nc, NSA_QB), _window_table(fb, NSA_QB))


def _swa_layer_attention(x, sh, sc, w_q, w_kv, sinks, rel_bias, fb):
    qT, k_aug, vT = _swa_proj(x, sh, sc, w_q, w_kv)
    sink = sinks.astype(F32) - rel_bias.astype(F32)[N_BUCKETS - 1]
    sink = jnp.broadcast_to(sink.reshape(N_KV, 1, HPG, 1), (N_KV, 8, HPG, SWA_QB))
    sink = sink.reshape(N_KV, 8, HPG * SWA_QB)
    band = jnp.concatenate([_near_band(fb, SWA_WINDOW, SWA_QB), _masked_rows(SWA_QB, SWA_QB)], axis=1)
    return _swa_attention(qT, k_aug, vT, band, sink)


def kernel(x, c, nsa_w_in, cmp_k_w1, cmp_k_w2, cmp_v_w1, cmp_v_w2, cmp_k_pos, cmp_v_pos, nsa_w_out,
           swa_w_q, swa_sinks, swa_w_out, shared_w_kv, rel_bias, router_w, router_b, moe_w_gate,
           moe_w_up, moe_w_down, ada_w, ada_b, ln_g, ln_b):
    B, S, D = x.shape
    ada = _ada(c, ada_w, ada_b)
    fb = _shifted_bias(rel_bias)
    for layer in range(DEPTH):
        sh_a, sc_a, g_a, sh_f, sc_f, g_f = [ada[layer, :, i * D:(i + 1) * D] for i in range(6)]
        if layer == 0:
            o = _nsa_layer_attention(x, sh_a, sc_a, nsa_w_in[0], cmp_k_w1[0], cmp_k_w2[0], cmp_v_w1[0],
                                     cmp_v_w2[0], cmp_k_pos[0], cmp_v_pos[0], fb)
            w_out = nsa_w_out[0]
        else:
            o = _swa_layer_attention(x, sh_a, sc_a, swa_w_q[0], shared_w_kv, swa_sinks[0], rel_bias, fb)
            w_out = swa_w_out[0]
        x1, h, gates = _oproj(o, x, w_out, g_a, ln_g[layer, 0], ln_b[layer, 0], sh_f, sc_f,
                              router_w, router_b)
        x = _moe(h, gates, moe_w_gate[layer], moe_w_up[layer], moe_w_down[layer], x1, g_f,
                 ln_g[layer, 1], ln_b[layer, 1])
    return x
```

```python
import functools
import math

import numpy as np
import jax
import jax.numpy as jnp
from jax import lax
from jax.experimental import pallas as pl
from jax.experimental.pallas import tpu as pltpu

F32 = jnp.float32
BF16 = jnp.bfloat16

D_MODEL = 1024
HEAD_DIM = 64
N_HEADS = 16
N_KV = 4
HPG = 4
Q_WIDTH = N_HEADS * HEAD_DIM
KV_WIDTH = N_KV * HEAD_DIM
CMP_LEN = 32
CMP_STRIDE = 16
CMP_HIDDEN = 256
SLC_LEN = 64
N_SELECT = 16
NSA_WINDOW = 512
SWA_WINDOW = 128
N_BUCKETS = 32
MAX_DISTANCE = 128
N_EXPERTS = 16
N_GROUPS = 4
EXPERTS_PER_GROUP = 4
D_EXPERT = 256
DEPTH = 2
ALPHA = (2.0 * DEPTH) ** 0.25
LN_EPS = 1e-5
NEG = -1e30
ATTN_SCALE = HEAD_DIM ** -0.5
LOG2E = math.log2(math.e)
Q_SCALE = ATTN_SCALE * LOG2E

NSA_QB = 256
SWA_QB = 128
SWA_QPS = 4
AUG = 128
MASK_LANES = AUG - HEAD_DIM
CMP_LEAD = 16
FAR_TK = 256
N_GATE = 3 * N_HEADS
VMEM_LIMIT = 56 * 1024 * 1024

_NT = (((1,), (1,)), ((), ()))


def _cparams(sem):
    return pltpu.CompilerParams(dimension_semantics=sem, vmem_limit_bytes=VMEM_LIMIT)


def _ada_kernel(c_ref, w_ref, b_ref, o_ref):
    c = c_ref[...]
    cond = c * jax.nn.sigmoid(c)
    o_ref[0] = jnp.dot(cond, w_ref[0], preferred_element_type=F32,
                       precision=lax.Precision.HIGHEST) + b_ref[0]


def _ada(c, ada_w, ada_b):
    B, D = c.shape
    n6 = ada_w.shape[-1]
    tn = 1536
    return pl.pallas_call(
        _ada_kernel,
        grid=(DEPTH, n6 // tn),
        in_specs=[pl.BlockSpec((B, D), lambda l, n: (0, 0)),
                  pl.BlockSpec((1, D, tn), lambda l, n: (l, 0, n)),
                  pl.BlockSpec((1, 1, tn), lambda l, n: (l, 0, n))],
        out_specs=pl.BlockSpec((1, B, tn), lambda l, n: (l, 0, n)),
        out_shape=jax.ShapeDtypeStruct((DEPTH, B, n6), F32),
        compiler_params=_cparams(("arbitrary", "arbitrary")),
        name="ada",
    )(c, ada_w, ada_b.reshape(DEPTH, 1, n6))


def _store_q_blocks(q_ref, qT, ts, qb):
    zeros = jnp.zeros((MASK_LANES, HPG * qb), BF16)
    for g in range(N_KV):
        for k in range(ts // qb):
            for j in range(HPG):
                r0 = (g * HPG + j) * HEAD_DIM
                q_ref[0, g, k, 0:HEAD_DIM, j * qb:(j + 1) * qb] = (
                    qT[r0:r0 + HEAD_DIM, k * qb:(k + 1) * qb].astype(BF16))
            q_ref[0, g, k, HEAD_DIM:AUG, :] = zeros


def _store_vt(vt_ref, vT, ts):
    ones = jnp.ones((MASK_LANES, ts), BF16)
    for g in range(N_KV):
        vt_ref[0, g, 0:HEAD_DIM, :] = vT[g * HEAD_DIM:(g + 1) * HEAD_DIM].astype(BF16)
        vt_ref[0, g, HEAD_DIM:AUG, :] = ones


def _nsa_proj_kernel(x_ref, sh_ref, sc_ref, wt_ref, ws_ref, q_ref, gt_ref, ks_ref, vst_ref, kw_ref,
                     vwt_ref, cv_ref, *, ts):
    s = pl.program_id(1)
    h = (x_ref[0] * (1.0 + sc_ref[0]) + sh_ref[0]).astype(BF16)
    yT = lax.dot_general(wt_ref[...], h, _NT, preferred_element_type=F32)
    y = jnp.dot(h, ws_ref[...], preferred_element_type=F32)
    _store_q_blocks(q_ref, yT[:Q_WIDTH], ts, NSA_QB)
    _store_vt(vst_ref, yT[Q_WIDTH:Q_WIDTH + KV_WIDTH], ts)
    _store_vt(vwt_ref, yT[Q_WIDTH + KV_WIDTH:Q_WIDTH + 2 * KV_WIDTH], ts)
    gates = jax.nn.sigmoid(yT[Q_WIDTH + 2 * KV_WIDTH:])
    for k in range(ts // NSA_QB):
        gt_ref[0, k] = gates[:, k * NSA_QB:(k + 1) * NSA_QB]
    lane = lax.broadcasted_iota(jnp.int32, (ts, AUG), 1)
    tok = s * ts + lax.broadcasted_iota(jnp.int32, (ts, AUG), 0)
    onehot = (lane - HEAD_DIM) == tok // SLC_LEN
    for g in range(N_KV):
        ks_ref[0, g] = jnp.where(onehot, 1.0, y[:, g * AUG:(g + 1) * AUG]).astype(BF16)
        kw_ref[0, g] = y[:, (N_KV + g) * AUG:(N_KV + g + 1) * AUG].astype(BF16)
    cv_ref[0] = y[:, 2 * N_KV * AUG:].astype(BF16)


def _pad_heads(w):
    D = w.shape[0]
    w = w.reshape(D, N_KV, HEAD_DIM)
    return jnp.pad(w, ((0, 0), (0, 0), (0, MASK_LANES))).reshape(D, N_KV * AUG)


def _nsa_proj(x, shift, scale, w_in, ts=512):
    B, S, D = x.shape
    nqb, lq = S // NSA_QB, HPG * NSA_QB
    cols = lambda i: w_in[:, Q_WIDTH + i * KV_WIDTH:Q_WIDTH + (i + 1) * KV_WIDTH]
    w_kc, w_vc, w_ksl, w_vsl, w_kw, w_vw = [cols(i) for i in range(6)]
    wt = jnp.concatenate([w_in[:, :Q_WIDTH] * Q_SCALE, w_vsl, w_vw,
                          w_in[:, Q_WIDTH + 6 * KV_WIDTH:]], axis=1).T.astype(BF16)
    ws = jnp.concatenate([_pad_heads(w_ksl), _pad_heads(w_kw), w_kc, w_vc], axis=1).astype(BF16)
    per_b = pl.BlockSpec((1, 1, D), lambda b, s: (b, 0, 0))
    kspec = pl.BlockSpec((1, N_KV, ts, AUG), lambda b, s: (b, 0, s, 0))
    vspec = pl.BlockSpec((1, N_KV, AUG, ts), lambda b, s: (b, 0, 0, s))
    k_shape = jax.ShapeDtypeStruct((B, N_KV, S, AUG), BF16)
    v_shape = jax.ShapeDtypeStruct((B, N_KV, AUG, S), BF16)
    return pl.pallas_call(
        functools.partial(_nsa_proj_kernel, ts=ts),
        grid=(B, S // ts),
        in_specs=[pl.BlockSpec((1, ts, D), lambda b, s: (b, s, 0)), per_b, per_b,
                  pl.BlockSpec(wt.shape, lambda b, s: (0, 0)), pl.BlockSpec(ws.shape, lambda b, s: (0, 0))],
        out_specs=[pl.BlockSpec((1, N_KV, ts // NSA_QB, AUG, lq), lambda b, s: (b, 0, s, 0, 0)),
                   pl.BlockSpec((1, ts // NSA_QB, N_GATE, NSA_QB), lambda b, s: (b, s, 0, 0)),
                   kspec, vspec, kspec, vspec,
                   pl.BlockSpec((1, ts, 2 * KV_WIDTH), lambda b, s: (b, s, 0))],
        out_shape=[jax.ShapeDtypeStruct((B, N_KV, nqb, AUG, lq), BF16),
                   jax.ShapeDtypeStruct((B, nqb, N_GATE, NSA_QB), F32),
                   k_shape, v_shape, k_shape, v_shape,
                   jax.ShapeDtypeStruct((B, S, 2 * KV_WIDTH), BF16)],
        compiler_params=_cparams(("arbitrary", "arbitrary")),
        name="nsa_proj",
    )(x, shift.reshape(B, 1, D), scale.reshape(B, 1, D), wt, ws)


def _swa_proj_kernel(x_ref, sh_ref, sc_ref, wq_ref, wk_ref, wv_ref, q_ref, k_ref, vt_ref, *, ts):
    x = x_ref[0]
    h = (x * (1.0 + sc_ref[0]) + sh_ref[0]).astype(BF16)
    xb = x.astype(BF16)
    _store_q_blocks(q_ref, lax.dot_general(wq_ref[...], h, _NT, preferred_element_type=F32), ts, SWA_QB)
    _store_vt(vt_ref, lax.dot_general(wv_ref[...], xb, _NT, preferred_element_type=F32), ts)
    k = jnp.dot(xb, wk_ref[...], preferred_element_type=F32)
    for g in range(N_KV):
        k_ref[0, g] = k[:, g * AUG:(g + 1) * AUG].astype(BF16)


def _swa_proj(x, shift, scale, w_q, w_kv, ts=512):
    B, S, D = x.shape
    lq = HPG * SWA_QB
    wq = (w_q * Q_SCALE).T.astype(BF16)
    wk = _pad_heads(w_kv[:, :KV_WIDTH]).astype(BF16)
    wv = w_kv[:, KV_WIDTH:].T.astype(BF16)
    per_b = pl.BlockSpec((1, 1, D), lambda b, s: (b, 0, 0))
    full = lambda a: pl.BlockSpec(a.shape, lambda b, s: (0, 0))
    return pl.pallas_call(
        functools.partial(_swa_proj_kernel, ts=ts),
        grid=(B, S // ts),
        in_specs=[pl.BlockSpec((1, ts, D), lambda b, s: (b, s, 0)), per_b, per_b, full(wq), full(wk), full(wv)],
        out_specs=[pl.BlockSpec((1, N_KV, ts // SWA_QB, AUG, lq), lambda b, s: (b, 0, s, 0, 0)),
                   pl.BlockSpec((1, N_KV, ts, AUG), lambda b, s: (b, 0, s, 0)),
                   pl.BlockSpec((1, N_KV, AUG, ts), lambda b, s: (b, 0, 0, s))],
        out_shape=[jax.ShapeDtypeStruct((B, N_KV, S // SWA_QB, AUG, lq), BF16),
                   jax.ShapeDtypeStruct((B, N_KV, S, AUG), BF16),
                   jax.ShapeDtypeStruct((B, N_KV, AUG, S), BF16)],
        compiler_params=_cparams(("arbitrary", "arbitrary")),
        name="swa_proj",
    )(x, shift.reshape(B, 1, D), scale.reshape(B, 1, D), wq, wk, wv)


def _gelu_tanh(x):
    return 0.5 * x * (1.0 + jnp.tanh(math.sqrt(2.0 / math.pi) * (x + 0.044715 * (x * x * x))))


def _compress_kernel(ch_ref, w1_ref, w2_ref, pos_ref, o_ref, *, nc):
    half = CMP_STRIDE * HEAD_DIM
    ch = ch_ref[0]
    w1 = w1_ref[...]
    top = jnp.dot(ch, w1[:half], preferred_element_type=F32)
    bot = jnp.dot(ch, w1[half:], preferred_element_type=F32)
    posw = jnp.dot(pos_ref[...], w1, preferred_element_type=F32)[0:1]
    w2 = w2_ref[...]
    for g in range(N_KV):
        bot_g = pltpu.roll(bot[g * nc:(g + 1) * nc], nc - 1, 0)
        pre = top[g * nc:(g + 1) * nc] + bot_g + posw
        o_ref[0, g] = jnp.dot(_gelu_tanh(pre).astype(BF16), w2, preferred_element_type=F32)


def _compress(chunks, w1, w2, pos):
    B, gn, ck = chunks.shape
    nc = gn // N_KV
    pos8 = jnp.broadcast_to(pos.reshape(1, CMP_LEN * HEAD_DIM), (8, CMP_LEN * HEAD_DIM)).astype(BF16)
    return pl.pallas_call(
        functools.partial(_compress_kernel, nc=nc),
        grid=(B,),
        in_specs=[pl.BlockSpec((1, gn, ck), lambda b: (b, 0, 0)),
                  pl.BlockSpec(w1.shape, lambda b: (0, 0)),
                  pl.BlockSpec(w2.shape, lambda b: (0, 0)),
                  pl.BlockSpec(pos8.shape, lambda b: (0, 0))],
        out_specs=pl.BlockSpec((1, N_KV, nc, HEAD_DIM), lambda b: (b, 0, 0, 0)),
        out_shape=jax.ShapeDtypeStruct((B, N_KV, nc, HEAD_DIM), F32),
        compiler_params=_cparams(("arbitrary",)),
        name="compress",
    )(chunks, w1.astype(BF16), w2.astype(BF16), pos8)


def _bucket_np(d):
    d = np.maximum(np.asarray(d, np.int64), 0)
    max_exact = N_BUCKETS // 2
    large = max_exact + (np.log(np.maximum(d, 1).astype(np.float32) / np.float32(max_exact))
                         / np.float32(math.log(MAX_DISTANCE / max_exact))
                         * np.float32(N_BUCKETS - max_exact)).astype(np.int32)
    large = np.minimum(large, N_BUCKETS - 1)
    return np.where(d < max_exact, d, large).astype(np.int32)


def _shifted_bias(rel_table):
    t = rel_table.astype(F32)
    return ((t[_bucket_np(np.arange(MAX_DISTANCE))] - t[N_BUCKETS - 1][None, :]) * LOG2E).T


def _to_group_lanes(m):
    H, K, Q = m.shape
    return m.reshape(N_KV, HPG, K, Q).transpose(0, 2, 1, 3).reshape(N_KV, K, HPG * Q)


def _bias_of_dist(fb, dist, window):
    ok = (dist >= 0) & (dist < window)
    idx = np.clip(dist, 0, MAX_DISTANCE - 1)
    return jnp.where(ok[None], jnp.where((dist >= MAX_DISTANCE)[None], 0.0, fb[:, idx]), NEG)


def _toeplitz_pair(fb, window):
    H, t = fb.shape[0], MAX_DISTANCE
    nv = 3 * t - 1
    v = _bias_of_dist(fb, np.arange(nv) - (t - 1), window)
    rows = 2 * t
    flat = jnp.tile(v, (1, rows + 1))[:, :rows * (nv + 1)]
    return flat.reshape(H, rows, nv + 1)[:, ::-1, :t]


def _near_band(fb, window, qb):
    H, t = fb.shape[0], MAX_DISTANCE
    nb = qb // t
    assert qb % t == 0 and (nb == 1 or window >= 2 * qb)
    pair = _toeplitz_pair(fb, window)
    blocks = {1: pair[:, :t], 0: pair[:, t:]}
    zero, masked = jnp.zeros((H, t, t), F32), jnp.full((H, t, t), NEG, F32)
    rows = []
    for a in range(2 * nb):
        row = [blocks.get(b + nb - a, zero if b + nb - a > 1 else masked) for b in range(nb)]
        rows.append(jnp.concatenate(row, axis=2))
    return _to_group_lanes(jnp.concatenate(rows, axis=1))


def _masked_rows(n, qb):
    return jnp.full((N_KV, n, HPG * qb), NEG, F32)


def _window_table(fb, qb):
    kk = np.arange(qb)[:, None]
    qi = np.arange(HPG * qb)[None, :] % qb
    edge = np.where(kk > qi, 0.0, NEG).astype(np.float32)
    edge = jnp.broadcast_to(jnp.asarray(edge), (N_KV, qb, HPG * qb))
    mid = jnp.zeros((N_KV, NSA_WINDOW - 2 * qb, HPG * qb), F32)
    return jnp.concatenate([edge, mid, _near_band(fb, NSA_WINDOW, qb), _masked_rows(NSA_WINDOW, qb)], axis=1)


def _cmp_band_rows(qb):
    return CMP_LEAD + qb // CMP_STRIDE


def _cmp_band(fb, nc, qb):
    m = np.arange(_cmp_band_rows(qb))[:, None]
    i = np.arange(qb)[None, :]
    dist = i + CMP_STRIDE * CMP_LEAD - CMP_STRIDE * m - (CMP_LEN - 1)
    band = _to_group_lanes(_bias_of_dist(fb, dist, 1 << 30))
    return jnp.concatenate([jnp.zeros((N_KV, nc, HPG * qb), F32), band, _masked_rows(nc, qb)], axis=1)


def _to_token_rows(oT, qb):
    pair = lambda a: jnp.concatenate([oT[:, a * qb:(a + 1) * qb], oT[:, (a + 1) * qb:(a + 2) * qb]], axis=0)
    return jnp.concatenate([pair(0).T, pair(2).T], axis=1)


def _nsa_kernel(q_ref, gt_ref, kc_ref, vct_ref, ks_ref, vst_ref, kw_ref, vwt_ref, ovl_ref, cb_ref,
                wt_ref, o_ref, acc_ref, ow_ref, sa_ref, sb_ref, pa_ref, pb_ref, *, nc):
    QB, LQ = NSA_QB, HPG * NSA_QB
    g = pl.program_id(1)
    qb = pl.program_id(2)
    s0 = qb * QB
    qT = q_ref[0, 0, 0]

    def gate(branch):
        rows = [gt_ref[0, 0, pl.ds(branch * N_HEADS + g * HPG + j, 1), :] for j in range(HPG)]
        return jnp.concatenate(rows, axis=1)

    win_keys = NSA_WINDOW + QB
    w0 = pl.multiple_of(jnp.maximum(s0 - NSA_WINDOW, 0), QB)
    t0 = pl.multiple_of(w0 - (s0 - NSA_WINDOW), QB)
    sw = (jnp.dot(kw_ref[0, 0, pl.ds(w0, win_keys), :], qT, preferred_element_type=F32)
          + wt_ref[0, pl.ds(t0, win_keys), :])
    m_w = jnp.max(sw, axis=0, keepdims=True)
    p_w = jnp.exp2(sw - m_w).astype(BF16)
    acc_w = jnp.dot(vwt_ref[0, 0, :, pl.ds(w0, win_keys)], p_w, preferred_element_type=F32)
    ow_ref[...] = gate(2) * (acc_w[:HEAD_DIM] / acc_w[HEAD_DIM:HEAD_DIM + 1])

    sc = jnp.dot(kc_ref[0, 0], qT, preferred_element_type=F32)
    cstart = pl.multiple_of(nc + CMP_LEAD - (QB // CMP_STRIDE) * qb, 8)
    sc = sc + cb_ref[0, pl.ds(cstart, nc), :]
    m_c = jnp.max(sc, axis=0, keepdims=True)
    e_c = jnp.exp2(sc - m_c)
    l_c = jnp.sum(e_c, axis=0, keepdims=True)
    p_c = e_c * jnp.where(m_c > 0.1 * NEG, 1.0 / l_c, 0.0)
    o_cmp = jnp.dot(vct_ref[0, 0], p_c.astype(BF16), preferred_element_type=F32)[:HEAD_DIM]

    p_sum = functools.reduce(jnp.add, [p_c[:, j * QB:(j + 1) * QB] for j in range(HPG)])
    imp = jnp.dot(ovl_ref[...], p_sum, preferred_element_type=F32,
                  precision=lax.Precision.HIGHEST)[HEAD_DIM:]
    blk = lax.broadcasted_iota(jnp.int32, (SLC_LEN, QB), 0)
    tq = s0 + lax.broadcasted_iota(jnp.int32, (SLC_LEN, QB), 1)
    cur = tq // SLC_LEN
    forced = (blk == 0) | (blk == cur) | (blk == cur - 1)
    future = blk > cur
    val = jnp.where(forced, 1e9, jnp.where(future, -1e9, imp))
    vals = [val[8 * a:8 * a + 8] for a in range(8)]
    sub = lax.broadcasted_iota(jnp.int32, (8, QB), 0)
    cnts = [jnp.zeros((8, QB), F32) for _ in range(8)]
    for jp in range(SLC_LEN):
        ap, r = divmod(jp, 8)
        rv = jnp.broadcast_to(vals[ap][r:r + 1], (8, QB))
        for a in range(8):
            gt = jnp.where(rv > vals[a], 1.0, 0.0)
            ge = jnp.where(rv >= vals[a], 1.0, 0.0)
            if a < ap:
                inc = gt
            elif a > ap:
                inc = ge
            else:
                inc = jnp.where(sub > r, ge, gt)
            cnts[a] = cnts[a] + inc
    cnt = jnp.concatenate(cnts, axis=0)
    allowed = (cnt < float(N_SELECT)) & (blk <= cur)
    near_blk = blk >= (s0 - QB) // SLC_LEN
    neg_near = jnp.where(allowed, 0.0, NEG).astype(BF16)
    neg_far = jnp.where(allowed & jnp.logical_not(near_blk), 0.0, NEG).astype(BF16)
    q_near = jnp.concatenate([qT[:HEAD_DIM], jnp.tile(neg_near, (1, HPG))], axis=0)
    q_far = jnp.concatenate([qT[:HEAD_DIM], jnp.tile(neg_far, (1, HPG))], axis=0)

    n0 = pl.multiple_of(jnp.maximum(s0 - QB, 0), QB)
    b0 = pl.multiple_of(NSA_WINDOW - QB + n0 - (s0 - QB), QB)
    s_near = (jnp.dot(ks_ref[0, 0, pl.ds(n0, 2 * QB), :], q_near, preferred_element_type=F32)
              + wt_ref[0, pl.ds(b0, 2 * QB), :])
    m_near = jnp.max(s_near, axis=0, keepdims=True)
    p_near = jnp.exp2(s_near - m_near).astype(BF16)
    acc_ref[...] = jnp.dot(vst_ref[0, 0, :, pl.ds(n0, 2 * QB)], p_near, preferred_element_type=F32)

    last_tile = ks_ref.shape[2] // FAR_TK - 1

    def tile_start(t):
        return pl.multiple_of(jnp.minimum(t, last_tile) * FAR_TK, FAR_TK)

    def qk(t):
        return jnp.dot(ks_ref[0, 0, pl.ds(tile_start(t), FAR_TK), :], q_far, preferred_element_type=F32)

    def softmax_update(s, m_run):
        m_new = jnp.maximum(m_run, jnp.max(s, axis=0, keepdims=True))
        return jnp.exp2(s - m_new).astype(BF16), jnp.exp2(m_run - m_new), m_new

    def pv_update(t, p, alpha):
        pv = jnp.dot(vst_ref[0, 0, :, pl.ds(tile_start(t), FAR_TK)], p, preferred_element_type=F32)
        acc_ref[...] = acc_ref[...] * alpha + pv

    sa_ref[...] = qk(0)
    sb_ref[...] = qk(1)
    p0, alpha0, m0 = softmax_update(sa_ref[...], m_near)
    pa_ref[...] = p0

    def far_body(i, carry):
        alpha_a, m_run = carry
        t = 2 * i
        pv_update(t, pa_ref[...], alpha_a)
        p_b, alpha_b, m_run = softmax_update(sb_ref[...], m_run)
        pb_ref[...] = p_b
        sa_ref[...] = qk(t + 2)
        pv_update(t + 1, pb_ref[...], alpha_b)
        p_a, alpha_a, m_run = softmax_update(sa_ref[...], m_run)
        pa_ref[...] = p_a
        sb_ref[...] = qk(t + 3)
        return alpha_a, m_run

    n_far = (jnp.maximum(s0 - QB, 0) + FAR_TK - 1) // FAR_TK
    lax.fori_loop(0, (n_far + 1) // 2, far_body, (alpha0, m0))
    acc = acc_ref[...]
    o_sel = acc[:HEAD_DIM] / acc[HEAD_DIM:HEAD_DIM + 1]

    o = gate(0) * o_cmp + gate(1) * o_sel + ow_ref[...]
    o_ref[0] = _to_token_rows(o, QB).astype(o_ref.dtype)


def _nsa_attention(qT, gatesT, kc_aug, vcT, ks_aug, vsT, kw_aug, vwT, ovl, cband, wtable):
    B, G, nqb = qT.shape[:3]
    QB, LQ = NSA_QB, HPG * NSA_QB
    nc = kc_aug.shape[2]
    per_bg = lambda a: pl.BlockSpec((1, 1) + a.shape[2:], lambda b, g, q: (b, g) + (0,) * (a.ndim - 2))
    per_g = lambda a: pl.BlockSpec((1,) + a.shape[1:], lambda b, g, q: (g,) + (0,) * (a.ndim - 1))
    return pl.pallas_call(
        functools.partial(_nsa_kernel, nc=nc),
        grid=(B, G, nqb),
        in_specs=[pl.BlockSpec((1, 1, 1, AUG, LQ), lambda b, g, q: (b, g, q, 0, 0)),
                  pl.BlockSpec((1, 1, N_GATE, QB), lambda b, g, q: (b, q, 0, 0)),
                  per_bg(kc_aug), per_bg(vcT), per_bg(ks_aug), per_bg(vsT),
                  per_bg(kw_aug), per_bg(vwT), pl.BlockSpec(ovl.shape, lambda b, g, q: (0, 0)),
                  per_g(cband), per_g(wtable)],
        out_specs=pl.BlockSpec((1, QB, HPG * HEAD_DIM), lambda b, g, q: (b, q, g)),
        out_shape=jax.ShapeDtypeStruct((B, nqb * QB, Q_WIDTH), BF16),
        scratch_shapes=[pltpu.VMEM((AUG, LQ), F32), pltpu.VMEM((HEAD_DIM, LQ), F32),
                        pltpu.VMEM((FAR_TK, LQ), F32), pltpu.VMEM((FAR_TK, LQ), F32),
                        pltpu.VMEM((FAR_TK, LQ), BF16), pltpu.VMEM((FAR_TK, LQ), BF16)],
        compiler_params=_cparams(("arbitrary", "arbitrary", "arbitrary")),
        name="nsa_attention",
    )(qT, gatesT, kc_aug, vcT, ks_aug, vsT, kw_aug, vwT, ovl, cband, wtable)


def _swa_kernel(q_ref, k_ref, vt_ref, band_ref, sink_ref, o_ref):
    QB = SWA_QB
    sink = sink_ref[0, 0:1]
    for i in range(SWA_QPS):
        s0 = (pl.program_id(2) * SWA_QPS + i) * QB
        n0 = pl.multiple_of(jnp.maximum(s0 - QB, 0), QB)
        b0 = pl.multiple_of(n0 - (s0 - QB), QB)
        s = (jnp.dot(k_ref[0, 0, pl.ds(n0, 2 * QB), :], q_ref[0, 0, i], preferred_element_type=F32)
             + band_ref[0, pl.ds(b0, 2 * QB), :])
        m = jnp.maximum(jnp.max(s, axis=0, keepdims=True), sink)
        p = jnp.exp2(s - m).astype(BF16)
        acc = jnp.dot(vt_ref[0, 0, :, pl.ds(n0, 2 * QB)], p, preferred_element_type=F32)
        denom = acc[HEAD_DIM:HEAD_DIM + 1] + jnp.exp2(sink - m)
        o_ref[0, i * QB:(i + 1) * QB, :] = _to_token_rows(acc[:HEAD_DIM] / denom, QB).astype(o_ref.dtype)


def _swa_attention(qT, k_aug, vT, band, sink):
    B, G, nqb = qT.shape[:3]
    QB, LQ = SWA_QB, HPG * SWA_QB
    per_bg = lambda a: pl.BlockSpec((1, 1) + a.shape[2:], lambda b, g, q: (b, g) + (0,) * (a.ndim - 2))
    per_g = lambda a: pl.BlockSpec((1,) + a.shape[1:], lambda b, g, q: (g,) + (0,) * (a.ndim - 1))
    return pl.pallas_call(
        _swa_kernel,
        grid=(B, G, nqb // SWA_QPS),
        in_specs=[pl.BlockSpec((1, 1, SWA_QPS, AUG, LQ), lambda b, g, q: (b, g, q, 0, 0)),
                  per_bg(k_aug), per_bg(vT), per_g(band), per_g(sink)],
        out_specs=pl.BlockSpec((1, SWA_QPS * QB, HPG * HEAD_DIM), lambda b, g, q: (b, q, g)),
        out_shape=jax.ShapeDtypeStruct((B, nqb * QB, Q_WIDTH), BF16),
        compiler_params=_cparams(("arbitrary", "arbitrary", "arbitrary")),
        name="swa_attention",
    )(qT, k_aug, vT, band, sink)


def _layer_norm(y, g, b):
    mu = jnp.mean(y, axis=-1, keepdims=True)
    yc = y - mu
    var = jnp.mean(yc * yc, axis=-1, keepdims=True)
    return yc * lax.rsqrt(var + LN_EPS) * g + b


def _top2_of4(a, b, c, d):
    hi1, lo1 = jnp.maximum(a, b), jnp.minimum(a, b)
    hi2, lo2 = jnp.maximum(c, d), jnp.minimum(c, d)
    return jnp.maximum(hi1, hi2) + jnp.maximum(jnp.minimum(hi1, hi2), jnp.maximum(lo1, lo2))


def _route(s, sb):
    n = EXPERTS_PER_GROUP
    score = [_top2_of4(*sb[n * r:n * r + n]) for r in range(N_GROUPS)]
    best = functools.reduce(jnp.maximum, score)
    taken = jnp.zeros_like(best) > 1.0
    in_grp = []
    for r in range(N_GROUPS):
        pick = (score[r] == best) & jnp.logical_not(taken)
        in_grp.append(pick)
        taken = taken | pick
    gates = []
    for e in range(N_EXPERTS):
        r = e // n
        ahead = jnp.zeros_like(best)
        for f in range(n * r, n * r + n):
            if f != e:
                beats = (sb[f] >= sb[e]) if f < e else (sb[f] > sb[e])
                ahead = ahead + jnp.where(beats, 1.0, 0.0)
        gates.append(jnp.where(in_grp[r] & (ahead < 2.0), s[e], 0.0))
    total = functools.reduce(jnp.add, gates)
    return [g / total for g in gates]


def _oproj_kernel(o_ref, x_ref, w_ref, ga_ref, lg_ref, lb_ref, shf_ref, scf_ref, rw_ref, rb_ref,
                  x1_ref, h_ref, gate_ref):
    mix = jnp.dot(o_ref[0], w_ref[...], preferred_element_type=F32)
    x1 = _layer_norm(ALPHA * x_ref[0] + (1.0 + ga_ref[0]) * mix, lg_ref[...], lb_ref[...])
    x1_ref[0] = x1
    h = x1 * (1.0 + scf_ref[0]) + shf_ref[0]
    h_ref[0] = h.astype(BF16)
    logit = lax.dot_general(rw_ref[...], h, _NT, preferred_element_type=F32,
                            precision=lax.Precision.HIGHEST)
    aff = jax.nn.sigmoid(logit)
    biased = aff + rb_ref[...][:, 0:1]
    s = [aff[e:e + 1] for e in range(N_EXPERTS)]
    sb = [biased[e:e + 1] for e in range(N_EXPERTS)]
    gate_ref[0] = jnp.concatenate(_route(s, sb), axis=0)


def _oproj(o, x, w_out, g_a, ln_g, ln_b, sh_f, sc_f, router_w, router_b, ts=512):
    B, S, D = x.shape
    row = lambda a: a.reshape(1, D)
    per_b = pl.BlockSpec((1, 1, D), lambda b, s: (b, 0, 0))
    full = lambda a: pl.BlockSpec(a.shape, lambda b, s: (0,) * a.ndim)
    rwT = router_w.T
    rb = jnp.broadcast_to(router_b.reshape(N_EXPERTS, 1), (N_EXPERTS, 128))
    args = (o, x, w_out.astype(BF16), g_a.reshape(B, 1, D), row(ln_g), row(ln_b),
            sh_f.reshape(B, 1, D), sc_f.reshape(B, 1, D), rwT, rb)
    tile = pl.BlockSpec((1, ts, D), lambda b, s: (b, s, 0))
    return pl.pallas_call(
        _oproj_kernel,
        grid=(B, S // ts),
        in_specs=[tile, tile, full(args[2]), per_b, full(args[4]), full(args[5]), per_b, per_b,
                  full(rwT), full(rb)],
        out_specs=[tile, tile, pl.BlockSpec((1, N_EXPERTS, ts), lambda b, s: (b, 0, s))],
        out_shape=[jax.ShapeDtypeStruct((B, S, D), F32), jax.ShapeDtypeStruct((B, S, D), BF16),
                   jax.ShapeDtypeStruct((B, N_EXPERTS, S), F32)],
        compiler_params=_cparams(("arbitrary", "arbitrary")),
        name="oproj_ln_router",
    )(*args)


def _moe_kernel(h_ref, gate_ref, wgu_ref, wd_ref, x_ref, gf_ref, lg_ref, lb_ref, o_ref, acc_ref):
    r = pl.program_id(1)
    n, width = EXPERTS_PER_GROUP, EXPERTS_PER_GROUP * D_EXPERT

    @pl.when(r == 0)
    def _():
        acc_ref[...] = jnp.zeros_like(acc_ref)

    gu = jnp.dot(h_ref[...], wgu_ref[0], preferred_element_type=F32)
    a, u = gu[:, :width], gu[:, width:]
    act = a * jax.nn.sigmoid(a) * u
    gates = gate_ref[...]
    lane = lax.broadcasted_iota(jnp.int32, gates.shape, 1)
    he = []
    for k in range(n):
        gcol = jnp.sum(jnp.where(lane == r * n + k, gates, 0.0), axis=1, keepdims=True)
        he.append((act[:, k * D_EXPERT:(k + 1) * D_EXPERT] * gcol).astype(BF16))
    acc_ref[...] += jnp.dot(jnp.concatenate(he, axis=1), wd_ref[0], preferred_element_type=F32)

    @pl.when(r == N_GROUPS - 1)
    def _():
        y = ALPHA * x_ref[...] + (1.0 + gf_ref[0]) * acc_ref[...]
        o_ref[...] = _layer_norm(y, lg_ref[...], lb_ref[...])


def _group_fused(w):
    E, D, F = w.shape
    w = w.reshape(N_GROUPS, EXPERTS_PER_GROUP, D, F).transpose(0, 2, 1, 3)
    return w.reshape(N_GROUPS, D, EXPERTS_PER_GROUP * F)


def _moe(h, gates, w_gate, w_up, w_down, x, g_f, ln_g, ln_b, ts=512):
    B, S, D = x.shape
    T = B * S
    width = EXPERTS_PER_GROUP * D_EXPERT
    wgu = jnp.concatenate([_group_fused(w_gate), _group_fused(w_up)], axis=-1).astype(BF16)
    wd = w_down.reshape(N_GROUPS, width, D).astype(BF16)
    gl = jnp.pad(gates.transpose(0, 2, 1).reshape(T, N_EXPERTS), ((0, 0), (0, 128 - N_EXPERTS)))
    tile = pl.BlockSpec((ts, D), lambda t, e: (t, 0))
    spt = S // ts
    out = pl.pallas_call(
        _moe_kernel,
        grid=(T // ts, N_GROUPS),
        in_specs=[tile, pl.BlockSpec((ts, 128), lambda t, e: (t, 0)),
                  pl.BlockSpec((1, D, 2 * width), lambda t, e: (e, 0, 0)),
                  pl.BlockSpec((1, width, D), lambda t, e: (e, 0, 0)),
                  tile, pl.BlockSpec((1, 1, D), lambda t, e: (t // spt, 0, 0)),
                  pl.BlockSpec((1, D), lambda t, e: (0, 0)), pl.BlockSpec((1, D), lambda t, e: (0, 0))],
        out_specs=tile,
        out_shape=jax.ShapeDtypeStruct((T, D), F32),
        scratch_shapes=[pltpu.VMEM((ts, D), F32)],
        compiler_params=_cparams(("arbitrary", "arbitrary")),
        name="moe_ln",
    )(h.reshape(T, D), gl, wgu, wd, x.reshape(T, D), g_f.reshape(B, 1, D), ln_g.reshape(1, D),
      ln_b.reshape(1, D))
    return out.reshape(B, S, D)


def _values_t(v):
    B, G, n, _ = v.shape
    return jnp.concatenate([v.transpose(0, 1, 3, 2), jnp.ones((B, G, MASK_LANES, n), v.dtype)], axis=2)


def _nsa_layer_attention(x, sh, sc, w_in, k_w1, k_w2, v_w1, v_w2, k_pos, v_pos, fb):
    B, S, _ = x.shape
    nc, n_slc = S // CMP_STRIDE, S // SLC_LEN
    qT, gatesT, ks_aug, vsT, kw_aug, vwT, cv = _nsa_proj(x, sh, sc, w_in)

    def to_chunks(t):
        t = t.reshape(B, nc, CMP_STRIDE, N_KV, HEAD_DIM).transpose(0, 3, 1, 2, 4)
        return t.reshape(B, N_KV * nc, CMP_STRIDE * HEAD_DIM)

    k_cmp = _compress(to_chunks(cv[..., :KV_WIDTH]), k_w1, k_w2, k_pos)
    v_cmp = _compress(to_chunks(cv[..., KV_WIDTH:]), v_w1, v_w2, v_pos)
    kc_aug = jnp.pad(k_cmp.astype(BF16), ((0, 0), (0, 0), (0, 0), (0, MASK_LANES)))
    vcT = _values_t(v_cmp.astype(BF16))

    c_start = np.arange(nc)[None, :] * CMP_STRIDE
    s_start = np.arange(MASK_LANES)[:, None] * SLC_LEN
    ovl = ((c_start < s_start + SLC_LEN) & (c_start + CMP_LEN > s_start)
           & (np.arange(nc)[None, :] < nc - 1) & (np.arange(MASK_LANES)[:, None] < n_slc))
    ovl = np.concatenate([np.zeros((HEAD_DIM, nc)), ovl.astype(np.float64)], axis=0).astype(np.float32)

    return _nsa_attention(qT, gatesT, kc_aug, vcT, ks_aug, vsT, kw_aug, vwT, jnp.asarray(ovl),
                          _cmp_band(fb, nc, NSA_QB), _window_table(fb, NSA_QB))


def _swa_layer_attention(x, sh, sc, w_q, w_kv, sinks, rel_bias, fb):
    qT, k_aug, vT = _swa_proj(x, sh, sc, w_q, w_kv)
    sink = (sinks.astype(F32) - rel_bias.astype(F32)[N_BUCKETS - 1]) * LOG2E
    sink = jnp.broadcast_to(sink.reshape(N_KV, 1, HPG, 1), (N_KV, 8, HPG, SWA_QB))
    sink = sink.reshape(N_KV, 8, HPG * SWA_QB)
    band = jnp.concatenate([_near_band(fb, SWA_WINDOW, SWA_QB), _masked_rows(SWA_QB, SWA_QB)], axis=1)
    return _swa_attention(qT, k_aug, vT, band, sink)


def kernel(x, c, nsa_w_in, cmp_k_w1, cmp_k_w2, cmp_v_w1, cmp_v_w2, cmp_k_pos, cmp_v_pos, nsa_w_out,
           swa_w_q, swa_sinks, swa_w_out, shared_w_kv, rel_bias, router_w, router_b, moe_w_gate,
           moe_w_up, moe_w_down, ada_w, ada_b, ln_g, ln_b):
    B, S, D = x.shape
    ada = _ada(c, ada_w, ada_b)
    fb = _shifted_bias(rel_bias)
    for layer in range(DEPTH):
        sh_a, sc_a, g_a, sh_f, sc_f, g_f = [ada[layer, :, i * D:(i + 1) * D] for i in range(6)]
        if layer == 0:
            o = _nsa_layer_attention(x, sh_a, sc_a, nsa_w_in[0], cmp_k_w1[0], cmp_k_w2[0], cmp_v_w1[0],
                                     cmp_v_w2[0], cmp_k_pos[0], cmp_v_pos[0], fb)
            w_out = nsa_w_out[0]
        else:
            o = _swa_layer_attention(x, sh_a, sc_a, swa_w_q[0], shared_w_kv, swa_sinks[0], rel_bias, fb)
            w_out = swa_w_out[0]
        x1, h, gates = _oproj(o, x, w_out, g_a, ln_g[layer, 0], ln_b[layer, 0], sh_f, sc_f,
                              router_w, router_b)
        x = _moe(h, gates, moe_w_gate[layer], moe_w_up[layer], moe_w_down[layer], x1, g_f,
                 ln_g[layer, 1], ln_b[layer, 1])
    return x
```

```python
import functools
import math

import numpy as np
import jax
import jax.numpy as jnp
from jax import lax
from jax.experimental import pallas as pl
from jax.experimental.pallas import tpu as pltpu

F32 = jnp.float32
BF16 = jnp.bfloat16

D_MODEL = 1024
HEAD_DIM = 64
N_HEADS = 16
N_KV = 4
HPG = 4
Q_WIDTH = N_HEADS * HEAD_DIM
KV_WIDTH = N_KV * HEAD_DIM
CMP_LEN = 32
CMP_STRIDE = 16
CMP_HIDDEN = 256
SLC_LEN = 64
N_SELECT = 16
NSA_WINDOW = 512
SWA_WINDOW = 128
N_BUCKETS = 32
MAX_DISTANCE = 128
N_EXPERTS = 16
N_GROUPS = 4
EXPERTS_PER_GROUP = 4
D_EXPERT = 256
DEPTH = 2
ALPHA = (2.0 * DEPTH) ** 0.25
LN_EPS = 1e-5
NEG = -1e30
ATTN_SCALE = HEAD_DIM ** -0.5
LOG2E = math.log2(math.e)
Q_SCALE = ATTN_SCALE * LOG2E

NSA_QB = 256
SWA_QB = 128
SWA_QPS = 8
AUG = 128
MASK_LANES = AUG - HEAD_DIM
V_ROWS = HEAD_DIM + 16
CMP_LEAD = 16
FAR_TK = 256
N_GATE = 3 * N_HEADS
VMEM_LIMIT = 56 * 1024 * 1024

_NT = (((1,), (1,)), ((), ()))


def _cparams(sem):
    return pltpu.CompilerParams(dimension_semantics=sem, vmem_limit_bytes=VMEM_LIMIT)


def _ada_kernel(c_ref, w_ref, b_ref, o_ref):
    c = c_ref[...]
    cond = c * jax.nn.sigmoid(c)
    o_ref[0] = jnp.dot(cond, w_ref[0], preferred_element_type=F32,
                       precision=lax.Precision.HIGHEST) + b_ref[0]


def _ada(c, ada_w, ada_b):
    B, D = c.shape
    n6 = ada_w.shape[-1]
    tn = 1536
    return pl.pallas_call(
        _ada_kernel,
        grid=(DEPTH, n6 // tn),
        in_specs=[pl.BlockSpec((B, D), lambda l, n: (0, 0)),
                  pl.BlockSpec((1, D, tn), lambda l, n: (l, 0, n)),
                  pl.BlockSpec((1, 1, tn), lambda l, n: (l, 0, n))],
        out_specs=pl.BlockSpec((1, B, tn), lambda l, n: (l, 0, n)),
        out_shape=jax.ShapeDtypeStruct((DEPTH, B, n6), F32),
        compiler_params=_cparams(("arbitrary", "arbitrary")),
        name="ada",
    )(c, ada_w, ada_b.reshape(DEPTH, 1, n6))


def _store_q_blocks(q_ref, qT, ts, qb):
    zeros = jnp.zeros((MASK_LANES, HPG * qb), BF16)
    for g in range(N_KV):
        for k in range(ts // qb):
            for j in range(HPG):
                r0 = (g * HPG + j) * HEAD_DIM
                q_ref[0, g, k, 0:HEAD_DIM, j * qb:(j + 1) * qb] = (
                    qT[r0:r0 + HEAD_DIM, k * qb:(k + 1) * qb].astype(BF16))
            q_ref[0, g, k, HEAD_DIM:AUG, :] = zeros


def _store_vt(vt_ref, vT, ts):
    ones = jnp.ones((V_ROWS - HEAD_DIM, ts), BF16)
    for g in range(N_KV):
        vt_ref[0, g, 0:HEAD_DIM, :] = vT[g * HEAD_DIM:(g + 1) * HEAD_DIM].astype(BF16)
        vt_ref[0, g, HEAD_DIM:V_ROWS, :] = ones


def _nsa_proj_kernel(x_ref, sh_ref, sc_ref, wt_ref, ws_ref, q_ref, gt_ref, ks_ref, vst_ref, kw_ref,
                     vwt_ref, ck_ref, cv_ref, cmp_scr, *, ts):
    s = pl.program_id(1)
    h = (x_ref[0] * (1.0 + sc_ref[0]) + sh_ref[0]).astype(BF16)
    yT = lax.dot_general(wt_ref[...], h, _NT, preferred_element_type=F32)
    y = jnp.dot(h, ws_ref[...], preferred_element_type=F32)
    _store_q_blocks(q_ref, yT[:Q_WIDTH], ts, NSA_QB)
    _store_vt(vst_ref, yT[Q_WIDTH:Q_WIDTH + KV_WIDTH], ts)
    _store_vt(vwt_ref, yT[Q_WIDTH + KV_WIDTH:Q_WIDTH + 2 * KV_WIDTH], ts)
    gates = jax.nn.sigmoid(yT[Q_WIDTH + 2 * KV_WIDTH:])
    for k in range(ts // NSA_QB):
        gt_ref[0, k] = gates[:, k * NSA_QB:(k + 1) * NSA_QB]
    lane = lax.broadcasted_iota(jnp.int32, (ts, AUG), 1)
    tok = s * ts + lax.broadcasted_iota(jnp.int32, (ts, AUG), 0)
    onehot = (lane - HEAD_DIM) == tok // SLC_LEN
    for g in range(N_KV):
        ks_ref[0, g] = jnp.where(onehot, 1.0, y[:, g * AUG:(g + 1) * AUG]).astype(BF16)
        kw_ref[0, g] = y[:, (N_KV + g) * AUG:(N_KV + g + 1) * AUG].astype(BF16)
    base = 2 * N_KV * AUG
    for c in range(2 * KV_WIDTH // 128):
        cmp_scr[c] = y[:, base + c * 128:base + (c + 1) * 128]
    for c, dst in enumerate([ck_ref, ck_ref, cv_ref, cv_ref]):
        for r in range(CMP_STRIDE):
            piece = cmp_scr[c, pl.ds(r, ts // CMP_STRIDE, stride=CMP_STRIDE), :]
            for half in range(2):
                g = 2 * (c % 2) + half
                dst[0, g, :, r * HEAD_DIM:(r + 1) * HEAD_DIM] = (
                    piece[:, half * HEAD_DIM:(half + 1) * HEAD_DIM].astype(BF16))


def _pad_heads(w):
    D = w.shape[0]
    w = w.reshape(D, N_KV, HEAD_DIM)
    return jnp.pad(w, ((0, 0), (0, 0), (0, MASK_LANES))).reshape(D, N_KV * AUG)


def _nsa_proj(x, shift, scale, w_in, ts=512):
    B, S, D = x.shape
    nqb, lq = S // NSA_QB, HPG * NSA_QB
    cols = lambda i: w_in[:, Q_WIDTH + i * KV_WIDTH:Q_WIDTH + (i + 1) * KV_WIDTH]
    w_kc, w_vc, w_ksl, w_vsl, w_kw, w_vw = [cols(i) for i in range(6)]
    wt = jnp.concatenate([w_in[:, :Q_WIDTH] * Q_SCALE, w_vsl, w_vw,
                          w_in[:, Q_WIDTH + 6 * KV_WIDTH:]], axis=1).T.astype(BF16)
    ws = jnp.concatenate([_pad_heads(w_ksl), _pad_heads(w_kw), w_kc, w_vc], axis=1).astype(BF16)
    per_b = pl.BlockSpec((1, 1, D), lambda b, s: (b, 0, 0))
    kspec = pl.BlockSpec((1, N_KV, ts, AUG), lambda b, s: (b, 0, s, 0))
    vspec = pl.BlockSpec((1, N_KV, V_ROWS, ts), lambda b, s: (b, 0, 0, s))
    k_shape = jax.ShapeDtypeStruct((B, N_KV, S, AUG), BF16)
    v_shape = jax.ShapeDtypeStruct((B, N_KV, V_ROWS, S), BF16)
    chunk = CMP_STRIDE * HEAD_DIM
    cspec = pl.BlockSpec((1, N_KV, ts // CMP_STRIDE, chunk), lambda b, s: (b, 0, s, 0))
    c_shape = jax.ShapeDtypeStruct((B, N_KV, S // CMP_STRIDE, chunk), BF16)
    return pl.pallas_call(
        functools.partial(_nsa_proj_kernel, ts=ts),
        grid=(B, S // ts),
        in_specs=[pl.BlockSpec((1, ts, D), lambda b, s: (b, s, 0)), per_b, per_b,
                  pl.BlockSpec(wt.shape, lambda b, s: (0, 0)), pl.BlockSpec(ws.shape, lambda b, s: (0, 0))],
        out_specs=[pl.BlockSpec((1, N_KV, ts // NSA_QB, AUG, lq), lambda b, s: (b, 0, s, 0, 0)),
                   pl.BlockSpec((1, ts // NSA_QB, N_GATE, NSA_QB), lambda b, s: (b, s, 0, 0)),
                   kspec, vspec, kspec, vspec,
                   cspec, cspec],
        out_shape=[jax.ShapeDtypeStruct((B, N_KV, nqb, AUG, lq), BF16),
                   jax.ShapeDtypeStruct((B, nqb, N_GATE, NSA_QB), F32),
                   k_shape, v_shape, k_shape, v_shape,
                   c_shape, c_shape],
        scratch_shapes=[pltpu.VMEM((2 * KV_WIDTH // 128, ts, 128), F32)],
        compiler_params=_cparams(("arbitrary", "arbitrary")),
        name="nsa_proj",
    )(x, shift.reshape(B, 1, D), scale.reshape(B, 1, D), wt, ws)


def _swa_proj_kernel(x_ref, sh_ref, sc_ref, wq_ref, wk_ref, wv_ref, q_ref, k_ref, vt_ref, *, ts):
    x = x_ref[0]
    h = (x * (1.0 + sc_ref[0]) + sh_ref[0]).astype(BF16)
    xb = x.astype(BF16)
    _store_q_blocks(q_ref, lax.dot_general(wq_ref[...], h, _NT, preferred_element_type=F32), ts, SWA_QB)
    _store_vt(vt_ref, lax.dot_general(wv_ref[...], xb, _NT, preferred_element_type=F32), ts)
    k = jnp.dot(xb, wk_ref[...], preferred_element_type=F32)
    for g in range(N_KV):
        k_ref[0, g] = k[:, g * AUG:(g + 1) * AUG].astype(BF16)


def _swa_proj(x, shift, scale, w_q, w_kv, ts=512):
    B, S, D = x.shape
    lq = HPG * SWA_QB
    wq = (w_q * Q_SCALE).T.astype(BF16)
    wk = _pad_heads(w_kv[:, :KV_WIDTH]).astype(BF16)
    wv = w_kv[:, KV_WIDTH:].T.astype(BF16)
    per_b = pl.BlockSpec((1, 1, D), lambda b, s: (b, 0, 0))
    full = lambda a: pl.BlockSpec(a.shape, lambda b, s: (0, 0))
    return pl.pallas_call(
        functools.partial(_swa_proj_kernel, ts=ts),
        grid=(B, S // ts),
        in_specs=[pl.BlockSpec((1, ts, D), lambda b, s: (b, s, 0)), per_b, per_b, full(wq), full(wk), full(wv)],
        out_specs=[pl.BlockSpec((1, N_KV, ts // SWA_QB, AUG, lq), lambda b, s: (b, 0, s, 0, 0)),
                   pl.BlockSpec((1, N_KV, ts, AUG), lambda b, s: (b, 0, s, 0)),
                   pl.BlockSpec((1, N_KV, V_ROWS, ts), lambda b, s: (b, 0, 0, s))],
        out_shape=[jax.ShapeDtypeStruct((B, N_KV, S // SWA_QB, AUG, lq), BF16),
                   jax.ShapeDtypeStruct((B, N_KV, S, AUG), BF16),
                   jax.ShapeDtypeStruct((B, N_KV, V_ROWS, S), BF16)],
        compiler_params=_cparams(("arbitrary", "arbitrary")),
        name="swa_proj",
    )(x, shift.reshape(B, 1, D), scale.reshape(B, 1, D), wq, wk, wv)


def _gelu_tanh(x):
    return 0.5 * x * (1.0 + jnp.tanh(math.sqrt(2.0 / math.pi) * (x + 0.044715 * (x * x * x))))


def _compress_kernel(ch_ref, w1_ref, w2_ref, pos_ref, o_ref, *, nc):
    half = CMP_STRIDE * HEAD_DIM
    ch = ch_ref[0]
    w1 = w1_ref[...]
    top = jnp.dot(ch, w1[:half], preferred_element_type=F32)
    bot = jnp.dot(ch, w1[half:], preferred_element_type=F32)
    posw = jnp.dot(pos_ref[...], w1, preferred_element_type=F32)[0:1]
    w2 = w2_ref[...]
    for g in range(N_KV):
        bot_g = pltpu.roll(bot[g * nc:(g + 1) * nc], nc - 1, 0)
        pre = top[g * nc:(g + 1) * nc] + bot_g + posw
        o_ref[0, g] = jnp.dot(_gelu_tanh(pre).astype(BF16), w2, preferred_element_type=F32)


def _compress(chunks, w1, w2, pos):
    B, gn, ck = chunks.shape
    nc = gn // N_KV
    pos8 = jnp.broadcast_to(pos.reshape(1, CMP_LEN * HEAD_DIM), (8, CMP_LEN * HEAD_DIM)).astype(BF16)
    return pl.pallas_call(
        functools.partial(_compress_kernel, nc=nc),
        grid=(B,),
        in_specs=[pl.BlockSpec((1, gn, ck), lambda b: (b, 0, 0)),
                  pl.BlockSpec(w1.shape, lambda b: (0, 0)),
                  pl.BlockSpec(w2.shape, lambda b: (0, 0)),
                  pl.BlockSpec(pos8.shape, lambda b: (0, 0))],
        out_specs=pl.BlockSpec((1, N_KV, nc, HEAD_DIM), lambda b: (b, 0, 0, 0)),
        out_shape=jax.ShapeDtypeStruct((B, N_KV, nc, HEAD_DIM), F32),
        compiler_params=_cparams(("arbitrary",)),
        name="compress",
    )(chunks, w1.astype(BF16), w2.astype(BF16), pos8)


def _bucket_np(d):
    d = np.maximum(np.asarray(d, np.int64), 0)
    max_exact = N_BUCKETS // 2
    large = max_exact + (np.log(np.maximum(d, 1).astype(np.float32) / np.float32(max_exact))
                         / np.float32(math.log(MAX_DISTANCE / max_exact))
                         * np.float32(N_BUCKETS - max_exact)).astype(np.int32)
    large = np.minimum(large, N_BUCKETS - 1)
    return np.where(d < max_exact, d, large).astype(np.int32)


def _shifted_bias(rel_table):
    t = rel_table.astype(F32)
    return ((t[_bucket_np(np.arange(MAX_DISTANCE))] - t[N_BUCKETS - 1][None, :]) * LOG2E).T


def _to_group_lanes(m):
    H, K, Q = m.shape
    return m.reshape(N_KV, HPG, K, Q).transpose(0, 2, 1, 3).reshape(N_KV, K, HPG * Q)


def _bias_of_dist(fb, dist, window):
    ok = (dist >= 0) & (dist < window)
    idx = np.clip(dist, 0, MAX_DISTANCE - 1)
    return jnp.where(ok[None], jnp.where((dist >= MAX_DISTANCE)[None], 0.0, fb[:, idx]), NEG)


def _toeplitz_pair(fb, window):
    H, t = fb.shape[0], MAX_DISTANCE
    nv = 3 * t - 1
    v = _bias_of_dist(fb, np.arange(nv) - (t - 1), window)
    rows = 2 * t
    flat = jnp.tile(v, (1, rows + 1))[:, :rows * (nv + 1)]
    return flat.reshape(H, rows, nv + 1)[:, ::-1, :t]


def _near_band(fb, window, qb):
    H, t = fb.shape[0], MAX_DISTANCE
    nb = qb // t
    assert qb % t == 0 and (nb == 1 or window >= 2 * qb)
    pair = _toeplitz_pair(fb, window)
    blocks = {1: pair[:, :t], 0: pair[:, t:]}
    zero, masked = jnp.zeros((H, t, t), F32), jnp.full((H, t, t), NEG, F32)
    rows = []
    for a in range(2 * nb):
        row = [blocks.get(b + nb - a, zero if b + nb - a > 1 else masked) for b in range(nb)]
        rows.append(jnp.concatenate(row, axis=2))
    return _to_group_lanes(jnp.concatenate(rows, axis=1))


def _masked_rows(n, qb):
    return jnp.full((N_KV, n, HPG * qb), NEG, F32)


def _window_table(fb, qb):
    kk = np.arange(qb)[:, None]
    qi = np.arange(HPG * qb)[None, :] % qb
    edge = np.where(kk > qi, 0.0, NEG).astype(np.float32)
    edge = jnp.broadcast_to(jnp.asarray(edge), (N_KV, qb, HPG * qb))
    mid = jnp.zeros((N_KV, NSA_WINDOW - 2 * qb, HPG * qb), F32)
    return jnp.concatenate([edge, mid, _near_band(fb, NSA_WINDOW, qb), _masked_rows(NSA_WINDOW, qb)], axis=1)


def _cmp_band_rows(qb):
    return CMP_LEAD + qb // CMP_STRIDE


def _cmp_band(fb, nc, qb):
    m = np.arange(_cmp_band_rows(qb))[:, None]
    i = np.arange(qb)[None, :]
    dist = i + CMP_STRIDE * CMP_LEAD - CMP_STRIDE * m - (CMP_LEN - 1)
    band = _to_group_lanes(_bias_of_dist(fb, dist, 1 << 30))
    return jnp.concatenate([jnp.zeros((N_KV, nc, HPG * qb), F32), band, _masked_rows(nc, qb)], axis=1)


def _to_token_rows(oT, qb):
    pair = lambda a: jnp.concatenate([oT[:, a * qb:(a + 1) * qb], oT[:, (a + 1) * qb:(a + 2) * qb]], axis=0)
    return jnp.concatenate([pair(0).T, pair(2).T], axis=1)


def _nsa_kernel(q_ref, gt_ref, kc_ref, vct_ref, ks_ref, vst_ref, kw_ref, vwt_ref, ovl_ref, cb_ref,
                wt_ref, o_ref, acc_ref, ow_ref, sa_ref, sb_ref, pa_ref, pb_ref, *, nc):
    QB, LQ = NSA_QB, HPG * NSA_QB
    g = pl.program_id(1)
    qb = pl.program_id(2)
    s0 = qb * QB
    qT = q_ref[0, 0, 0]

    def gate(branch):
        rows = [gt_ref[0, 0, pl.ds(branch * N_HEADS + g * HPG + j, 1), :] for j in range(HPG)]
        return jnp.concatenate(rows, axis=1)

    sc = jnp.dot(kc_ref[0, 0], qT, preferred_element_type=F32)
    cstart = pl.multiple_of(nc + CMP_LEAD - (QB // CMP_STRIDE) * qb, 8)
    sc = sc + cb_ref[0, pl.ds(cstart, nc), :]
    win_keys = NSA_WINDOW + QB
    w0 = pl.multiple_of(jnp.maximum(s0 - NSA_WINDOW, 0), QB)
    t0 = pl.multiple_of(w0 - (s0 - NSA_WINDOW), QB)
    sw = (jnp.dot(kw_ref[0, 0, pl.ds(w0, win_keys), :], qT, preferred_element_type=F32)
          + wt_ref[0, pl.ds(t0, win_keys), :])

    m_c = jnp.max(sc, axis=0, keepdims=True)
    e_c = jnp.exp2(sc - m_c)
    l_c = jnp.sum(e_c, axis=0, keepdims=True)
    p_c = e_c * jnp.where(m_c > 0.1 * NEG, 1.0 / l_c, 0.0)
    o_cmp = jnp.dot(vct_ref[0, 0], p_c.astype(BF16), preferred_element_type=F32)[:HEAD_DIM]
    p_sum = functools.reduce(jnp.add, [p_c[:, j * QB:(j + 1) * QB] for j in range(HPG)])
    imp = jnp.dot(ovl_ref[...], p_sum, preferred_element_type=F32,
                  precision=lax.Precision.HIGHEST)[HEAD_DIM:]

    m_w = jnp.max(sw, axis=0, keepdims=True)
    p_w = jnp.exp2(sw - m_w).astype(BF16)
    acc_w = jnp.dot(vwt_ref[0, 0, :, pl.ds(w0, win_keys)], p_w, preferred_element_type=F32)

    blk = lax.broadcasted_iota(jnp.int32, (SLC_LEN, QB), 0)
    tq = s0 + lax.broadcasted_iota(jnp.int32, (SLC_LEN, QB), 1)
    cur = tq // SLC_LEN
    forced = (blk == 0) | (blk == cur) | (blk == cur - 1)
    future = blk > cur
    val = jnp.where(forced, 1e9, jnp.where(future, -1e9, imp))
    vals = [val[8 * a:8 * a + 8] for a in range(8)]
    sub = lax.broadcasted_iota(jnp.int32, (8, QB), 0)
    cnts = [jnp.zeros((8, QB), F32) for _ in range(8)]
    for jp in range(SLC_LEN):
        ap, r = divmod(jp, 8)
        rv = jnp.broadcast_to(vals[ap][r:r + 1], (8, QB))
        for a in range(8):
            gt = jnp.where(rv > vals[a], 1.0, 0.0)
            ge = jnp.where(rv >= vals[a], 1.0, 0.0)
            if a < ap:
                inc = gt
            elif a > ap:
                inc = ge
            else:
                inc = jnp.where(sub > r, ge, gt)
            cnts[a] = cnts[a] + inc
    cnt = jnp.concatenate(cnts, axis=0)
    allowed = (cnt < float(N_SELECT)) & (blk <= cur)
    near_blk = blk >= (s0 - QB) // SLC_LEN
    neg_near = jnp.where(allowed, 0.0, NEG).astype(BF16)
    neg_far = jnp.where(allowed & jnp.logical_not(near_blk), 0.0, NEG).astype(BF16)
    q_near = jnp.concatenate([qT[:HEAD_DIM], jnp.tile(neg_near, (1, HPG))], axis=0)
    q_far = jnp.concatenate([qT[:HEAD_DIM], jnp.tile(neg_far, (1, HPG))], axis=0)

    ow_ref[...] = gate(2) * (acc_w[:HEAD_DIM] / acc_w[HEAD_DIM:HEAD_DIM + 1])

    last_tile = ks_ref.shape[2] // FAR_TK - 1

    def tile_start(t):
        return pl.multiple_of(jnp.minimum(t, last_tile) * FAR_TK, FAR_TK)

    def qk(t):
        return jnp.dot(ks_ref[0, 0, pl.ds(tile_start(t), FAR_TK), :], q_far, preferred_element_type=F32)

    def softmax_update(s, m_run):
        m_new = jnp.maximum(m_run, jnp.max(s, axis=0, keepdims=True))
        return jnp.exp2(s - m_new).astype(BF16), jnp.exp2(m_run - m_new), m_new

    def pv_update(t, p, alpha):
        pv = jnp.dot(vst_ref[0, 0, :, pl.ds(tile_start(t), FAR_TK)], p, preferred_element_type=F32)
        acc_ref[...] = acc_ref[...] * alpha + pv

    n0 = pl.multiple_of(jnp.maximum(s0 - QB, 0), QB)
    b0 = pl.multiple_of(NSA_WINDOW - QB + n0 - (s0 - QB), QB)
    s_near = (jnp.dot(ks_ref[0, 0, pl.ds(n0, 2 * QB), :], q_near, preferred_element_type=F32)
              + wt_ref[0, pl.ds(b0, 2 * QB), :])
    sa_ref[...] = qk(0)
    sb_ref[...] = qk(1)
    m_near = jnp.max(s_near, axis=0, keepdims=True)
    p_near = jnp.exp2(s_near - m_near).astype(BF16)
    acc_ref[...] = jnp.dot(vst_ref[0, 0, :, pl.ds(n0, 2 * QB)], p_near, preferred_element_type=F32)
    p0, alpha0, m0 = softmax_update(sa_ref[...], m_near)
    pa_ref[...] = p0

    def far_body(i, carry):
        alpha_a, m_run = carry
        t = 2 * i
        pv_update(t, pa_ref[...], alpha_a)
        p_b, alpha_b, m_run = softmax_update(sb_ref[...], m_run)
        pb_ref[...] = p_b
        sa_ref[...] = qk(t + 2)
        pv_update(t + 1, pb_ref[...], alpha_b)
        p_a, alpha_a, m_run = softmax_update(sa_ref[...], m_run)
        pa_ref[...] = p_a
        sb_ref[...] = qk(t + 3)
        return alpha_a, m_run

    n_far = (jnp.maximum(s0 - QB, 0) + FAR_TK - 1) // FAR_TK
    lax.fori_loop(0, (n_far + 1) // 2, far_body, (alpha0, m0))
    acc = acc_ref[...]
    o_sel = acc[:HEAD_DIM] / acc[HEAD_DIM:HEAD_DIM + 1]

    o = gate(0) * o_cmp + gate(1) * o_sel + ow_ref[...]
    o_ref[0] = _to_token_rows(o, QB).astype(o_ref.dtype)


def _nsa_attention(qT, gatesT, kc_aug, vcT, ks_aug, vsT, kw_aug, vwT, ovl, cband, wtable):
    B, G, nqb = qT.shape[:3]
    QB, LQ = NSA_QB, HPG * NSA_QB
    nc = kc_aug.shape[2]
    per_bg = lambda a: pl.BlockSpec((1, 1) + a.shape[2:], lambda b, g, q: (b, g) + (0,) * (a.ndim - 2))
    per_g = lambda a: pl.BlockSpec((1,) + a.shape[1:], lambda b, g, q: (g,) + (0,) * (a.ndim - 1))
    return pl.pallas_call(
        functools.partial(_nsa_kernel, nc=nc),
        grid=(B, G, nqb),
        in_specs=[pl.BlockSpec((1, 1, 1, AUG, LQ), lambda b, g, q: (b, g, q, 0, 0)),
                  pl.BlockSpec((1, 1, N_GATE, QB), lambda b, g, q: (b, q, 0, 0)),
                  per_bg(kc_aug), per_bg(vcT), per_bg(ks_aug), per_bg(vsT),
                  per_bg(kw_aug), per_bg(vwT), pl.BlockSpec(ovl.shape, lambda b, g, q: (0, 0)),
                  per_g(cband), per_g(wtable)],
        out_specs=pl.BlockSpec((1, QB, HPG * HEAD_DIM), lambda b, g, q: (b, q, g)),
        out_shape=jax.ShapeDtypeStruct((B, nqb * QB, Q_WIDTH), BF16),
        scratch_shapes=[pltpu.VMEM((V_ROWS, LQ), F32), pltpu.VMEM((HEAD_DIM, LQ), F32),
                        pltpu.VMEM((FAR_TK, LQ), F32), pltpu.VMEM((FAR_TK, LQ), F32),
                        pltpu.VMEM((FAR_TK, LQ), BF16), pltpu.VMEM((FAR_TK, LQ), BF16)],
        compiler_params=_cparams(("arbitrary", "arbitrary", "arbitrary")),
        name="nsa_attention",
    )(qT, gatesT, kc_aug, vcT, ks_aug, vsT, kw_aug, vwT, ovl, cband, wtable)


def _swa_kernel(q_ref, k_ref, vt_ref, band_ref, sink_ref, o_ref):
    QB = SWA_QB
    sink = sink_ref[0, 0:1]
    starts, scores = [], []
    for i in range(SWA_QPS):
        s0 = (pl.program_id(2) * SWA_QPS + i) * QB
        n0 = pl.multiple_of(jnp.maximum(s0 - QB, 0), QB)
        b0 = pl.multiple_of(n0 - (s0 - QB), QB)
        starts.append(n0)
        scores.append(jnp.dot(k_ref[0, 0, pl.ds(n0, 2 * QB), :], q_ref[0, 0, i], preferred_element_type=F32)
                      + band_ref[0, pl.ds(b0, 2 * QB), :])
    maxes = [jnp.maximum(jnp.max(s, axis=0, keepdims=True), sink) for s in scores]
    probs = [jnp.exp2(s - m).astype(BF16) for s, m in zip(scores, maxes)]
    accs = [jnp.dot(vt_ref[0, 0, :, pl.ds(n0, 2 * QB)], p, preferred_element_type=F32)
            for n0, p in zip(starts, probs)]
    for i, (acc, m) in enumerate(zip(accs, maxes)):
        denom = acc[HEAD_DIM:HEAD_DIM + 1] + jnp.exp2(sink - m)
        o_ref[0, i * QB:(i + 1) * QB, :] = _to_token_rows(acc[:HEAD_DIM] / denom, QB).astype(o_ref.dtype)


def _swa_attention(qT, k_aug, vT, band, sink):
    B, G, nqb = qT.shape[:3]
    QB, LQ = SWA_QB, HPG * SWA_QB
    per_bg = lambda a: pl.BlockSpec((1, 1) + a.shape[2:], lambda b, g, q: (b, g) + (0,) * (a.ndim - 2))
    per_g = lambda a: pl.BlockSpec((1,) + a.shape[1:], lambda b, g, q: (g,) + (0,) * (a.ndim - 1))
    return pl.pallas_call(
        _swa_kernel,
        grid=(B, G, nqb // SWA_QPS),
        in_specs=[pl.BlockSpec((1, 1, SWA_QPS, AUG, LQ), lambda b, g, q: (b, g, q, 0, 0)),
                  per_bg(k_aug), per_bg(vT), per_g(band), per_g(sink)],
        out_specs=pl.BlockSpec((1, SWA_QPS * QB, HPG * HEAD_DIM), lambda b, g, q: (b, q, g)),
        out_shape=jax.ShapeDtypeStruct((B, nqb * QB, Q_WIDTH), BF16),
        compiler_params=_cparams(("arbitrary", "arbitrary", "arbitrary")),
        name="swa_attention",
    )(qT, k_aug, vT, band, sink)


def _layer_norm(y, g, b):
    mu = jnp.mean(y, axis=-1, keepdims=True)
    yc = y - mu
    var = jnp.mean(yc * yc, axis=-1, keepdims=True)
    return yc * lax.rsqrt(var + LN_EPS) * g + b


def _top2_of4(a, b, c, d):
    hi1, lo1 = jnp.maximum(a, b), jnp.minimum(a, b)
    hi2, lo2 = jnp.maximum(c, d), jnp.minimum(c, d)
    return jnp.maximum(hi1, hi2) + jnp.maximum(jnp.minimum(hi1, hi2), jnp.maximum(lo1, lo2))


def _route(s, sb):
    n = EXPERTS_PER_GROUP
    score = [_top2_of4(*sb[n * r:n * r + n]) for r in range(N_GROUPS)]
    best = functools.reduce(jnp.maximum, score)
    taken = jnp.zeros_like(best) > 1.0
    in_grp = []
    for r in range(N_GROUPS):
        pick = (score[r] == best) & jnp.logical_not(taken)
        in_grp.append(pick)
        taken = taken | pick
    gates = []
    for e in range(N_EXPERTS):
        r = e // n
        ahead = jnp.zeros_like(best)
        for f in range(n * r, n * r + n):
            if f != e:
                beats = (sb[f] >= sb[e]) if f < e else (sb[f] > sb[e])
                ahead = ahead + jnp.where(beats, 1.0, 0.0)
        gates.append(jnp.where(in_grp[r] & (ahead < 2.0), s[e], 0.0))
    total = functools.reduce(jnp.add, gates)
    return [g / total for g in gates]


def _oproj_kernel(o_ref, x_ref, w_ref, ga_ref, lg_ref, lb_ref, shf_ref, scf_ref, rw_ref, rb_ref,
                  x1_ref, h_ref, gate_ref):
    mix = jnp.dot(o_ref[0], w_ref[...], preferred_element_type=F32)
    x1 = _layer_norm(ALPHA * x_ref[0] + (1.0 + ga_ref[0]) * mix, lg_ref[...], lb_ref[...])
    x1_ref[0] = x1
    h = x1 * (1.0 + scf_ref[0]) + shf_ref[0]
    h_ref[0] = h.astype(BF16)
    logit = lax.dot_general(rw_ref[...], h, _NT, preferred_element_type=F32,
                            precision=lax.Precision.HIGHEST)
    aff = jax.nn.sigmoid(logit)
    biased = aff + rb_ref[...][:, 0:1]
    s = [aff[e:e + 1] for e in range(N_EXPERTS)]
    sb = [biased[e:e + 1] for e in range(N_EXPERTS)]
    gate_ref[0] = jnp.concatenate(_route(s, sb), axis=0)


def _oproj(o, x, w_out, g_a, ln_g, ln_b, sh_f, sc_f, router_w, router_b, ts=512):
    B, S, D = x.shape
    row = lambda a: a.reshape(1, D)
    per_b = pl.BlockSpec((1, 1, D), lambda b, s: (b, 0, 0))
    full = lambda a: pl.BlockSpec(a.shape, lambda b, s: (0,) * a.ndim)
    rwT = router_w.T
    rb = jnp.broadcast_to(router_b.reshape(N_EXPERTS, 1), (N_EXPERTS, 128))
    args = (o, x, w_out.astype(BF16), g_a.reshape(B, 1, D), row(ln_g), row(ln_b),
            sh_f.reshape(B, 1, D), sc_f.reshape(B, 1, D), rwT, rb)
    tile = pl.BlockSpec((1, ts, D), lambda b, s: (b, s, 0))
    return pl.pallas_call(
        _oproj_kernel,
        grid=(B, S // ts),
        in_specs=[tile, tile, full(args[2]), per_b, full(args[4]), full(args[5]), per_b, per_b,
                  full(rwT), full(rb)],
        out_specs=[tile, tile, pl.BlockSpec((1, N_EXPERTS, ts), lambda b, s: (b, 0, s))],
        out_shape=[jax.ShapeDtypeStruct((B, S, D), F32), jax.ShapeDtypeStruct((B, S, D), BF16),
                   jax.ShapeDtypeStruct((B, N_EXPERTS, S), F32)],
        compiler_params=_cparams(("arbitrary", "arbitrary")),
        name="oproj_ln_router",
    )(*args)


def _moe_kernel(h_ref, gate_ref, wgu_ref, wd_ref, x_ref, gf_ref, lg_ref, lb_ref, o_ref, acc_ref):
    r = pl.program_id(1)
    n, width = EXPERTS_PER_GROUP, EXPERTS_PER_GROUP * D_EXPERT

    @pl.when(r == 0)
    def _():
        acc_ref[...] = jnp.zeros_like(acc_ref)

    gu = jnp.dot(h_ref[...], wgu_ref[0], preferred_element_type=F32)
    a, u = gu[:, :width], gu[:, width:]
    act = a * jax.nn.sigmoid(a) * u
    gates = gate_ref[...]
    lane = lax.broadcasted_iota(jnp.int32, gates.shape, 1)
    he = []
    for k in range(n):
        gcol = jnp.sum(jnp.where(lane == r * n + k, gates, 0.0), axis=1, keepdims=True)
        he.append((act[:, k * D_EXPERT:(k + 1) * D_EXPERT] * gcol).astype(BF16))
    acc_ref[...] += jnp.dot(jnp.concatenate(he, axis=1), wd_ref[0], preferred_element_type=F32)

    @pl.when(r == N_GROUPS - 1)
    def _():
        y = ALPHA * x_ref[...] + (1.0 + gf_ref[0]) * acc_ref[...]
        o_ref[...] = _layer_norm(y, lg_ref[...], lb_ref[...])


def _group_fused(w):
    E, D, F = w.shape
    w = w.reshape(N_GROUPS, EXPERTS_PER_GROUP, D, F).transpose(0, 2, 1, 3)
    return w.reshape(N_GROUPS, D, EXPERTS_PER_GROUP * F)


def _moe(h, gates, w_gate, w_up, w_down, x, g_f, ln_g, ln_b, ts=512):
    B, S, D = x.shape
    T = B * S
    width = EXPERTS_PER_GROUP * D_EXPERT
    wgu = jnp.concatenate([_group_fused(w_gate), _group_fused(w_up)], axis=-1).astype(BF16)
    wd = w_down.reshape(N_GROUPS, width, D).astype(BF16)
    gl = jnp.pad(gates.transpose(0, 2, 1).reshape(T, N_EXPERTS), ((0, 0), (0, 128 - N_EXPERTS)))
    tile = pl.BlockSpec((ts, D), lambda t, e: (t, 0))
    spt = S // ts
    out = pl.pallas_call(
        _moe_kernel,
        grid=(T // ts, N_GROUPS),
        in_specs=[tile, pl.BlockSpec((ts, 128), lambda t, e: (t, 0)),
                  pl.BlockSpec((1, D, 2 * width), lambda t, e: (e, 0, 0)),
                  pl.BlockSpec((1, width, D), lambda t, e: (e, 0, 0)),
                  tile, pl.BlockSpec((1, 1, D), lambda t, e: (t // spt, 0, 0)),
                  pl.BlockSpec((1, D), lambda t, e: (0, 0)), pl.BlockSpec((1, D), lambda t, e: (0, 0))],
        out_specs=tile,
        out_shape=jax.ShapeDtypeStruct((T, D), F32),
        scratch_shapes=[pltpu.VMEM((ts, D), F32)],
        compiler_params=_cparams(("arbitrary", "arbitrary")),
        name="moe_ln",
    )(h.reshape(T, D), gl, wgu, wd, x.reshape(T, D), g_f.reshape(B, 1, D), ln_g.reshape(1, D),
      ln_b.reshape(1, D))
    return out.reshape(B, S, D)


def _values_t(v):
    B, G, n, _ = v.shape
    return jnp.concatenate([v.transpose(0, 1, 3, 2), jnp.ones((B, G, V_ROWS - HEAD_DIM, n), v.dtype)], axis=2)


def _nsa_layer_attention(x, sh, sc, w_in, k_w1, k_w2, v_w1, v_w2, k_pos, v_pos, fb):
    B, S, _ = x.shape
    nc, n_slc = S // CMP_STRIDE, S // SLC_LEN
    qT, gatesT, ks_aug, vsT, kw_aug, vwT, ck, cv = _nsa_proj(x, sh, sc, w_in)
    merge = lambda t: t.reshape(B, N_KV * nc, CMP_STRIDE * HEAD_DIM)
    k_cmp = _compress(merge(ck), k_w1, k_w2, k_pos)
    v_cmp = _compress(merge(cv), v_w1, v_w2, v_pos)
    kc_aug = jnp.pad(k_cmp.astype(BF16), ((0, 0), (0, 0), (0, 0), (0, MASK_LANES)))
    vcT = _values_t(v_cmp.astype(BF16))

    c_start = np.arange(nc)[None, :] * CMP_STRIDE
    s_start = np.arange(MASK_LANES)[:, None] * SLC_LEN
    ovl = ((c_start < s_start + SLC_LEN) & (c_start + CMP_LEN > s_start)
           & (np.arange(nc)[None, :] < nc - 1) & (np.arange(MASK_LANES)[:, None] < n_slc))
    ovl = np.concatenate([np.zeros((HEAD_DIM, nc)), ovl.astype(np.float64)], axis=0).astype(np.float32)

    return _nsa_attention(qT, gatesT, kc_aug, vcT, ks_aug, vsT, kw_aug, vwT, jnp.asarray(ovl),
                          _cmp_band(fb, nc, NSA_QB), _window_table(fb, NSA_QB))


def _swa_layer_attention(x, sh, sc, w_q, w_kv, sinks, rel_bias, fb):
    qT, k_aug, vT = _swa_proj(x, sh, sc, w_q, w_kv)
    sink = (sinks.astype(F32) - rel_bias.astype(F32)[N_BUCKETS - 1]) * LOG2E
    sink = jnp.broadcast_to(sink.reshape(N_KV, 1, HPG, 1), (N_KV, 8, HPG, SWA_QB))
    sink = sink.reshape(N_KV, 8, HPG * SWA_QB)
    band = jnp.concatenate([_near_band(fb, SWA_WINDOW, SWA_QB), _masked_rows(SWA_QB, SWA_QB)], axis=1)
    return _swa_attention(qT, k_aug, vT, band, sink)


def kernel(x, c, nsa_w_in, cmp_k_w1, cmp_k_w2, cmp_v_w1, cmp_v_w2, cmp_k_pos, cmp_v_pos, nsa_w_out,
           swa_w_q, swa_sinks, swa_w_out, shared_w_kv, rel_bias, router_w, router_b, moe_w_gate,
           moe_w_up, moe_w_down, ada_w, ada_b, ln_g, ln_b):
    B, S, D = x.shape
    ada = _ada(c, ada_w, ada_b)
    fb = _shifted_bias(rel_bias)
    for layer in range(DEPTH):
        sh_a, sc_a, g_a, sh_f, sc_f, g_f = [ada[layer, :, i * D:(i + 1) * D] for i in range(6)]
        if layer == 0:
            o = _nsa_layer_attention(x, sh_a, sc_a, nsa_w_in[0], cmp_k_w1[0], cmp_k_w2[0], cmp_v_w1[0],
                                     cmp_v_w2[0], cmp_k_pos[0], cmp_v_pos[0], fb)
            w_out = nsa_w_out[0]
        else:
            o = _swa_layer_attention(x, sh_a, sc_a, swa_w_q[0], shared_w_kv, swa_sinks[0], rel_bias, fb)
            w_out = swa_w_out[0]
        x1, h, gates = _oproj(o, x, w_out, g_a, ln_g[layer, 0], ln_b[layer, 0], sh_f, sc_f,
                              router_w, router_b)
        x = _moe(h, gates, moe_w_gate[layer], moe_w_up[layer], moe_w_down[layer], x1, g_f,
                 ln_g[layer, 1], ln_b[layer, 1])
    return x
```

```python
import functools
import math

import numpy as np
import jax
import jax.numpy as jnp
from jax import lax
from jax.experimental import pallas as pl
from jax.experimental.pallas import tpu as pltpu

F32 = jnp.float32
BF16 = jnp.bfloat16

D_MODEL = 1024
HEAD_DIM = 64
N_HEADS = 16
N_KV = 4
HPG = 4
Q_WIDTH = N_HEADS * HEAD_DIM
KV_WIDTH = N_KV * HEAD_DIM
CMP_LEN = 32
CMP_STRIDE = 16
CMP_HIDDEN = 256
SLC_LEN = 64
N_SELECT = 16
NSA_WINDOW = 512
SWA_WINDOW = 128
N_BUCKETS = 32
MAX_DISTANCE = 128
N_EXPERTS = 16
N_GROUPS = 4
EXPERTS_PER_GROUP = 4
D_EXPERT = 256
DEPTH = 2
ALPHA = (2.0 * DEPTH) ** 0.25
LN_EPS = 1e-5
NEG = -1e30
ATTN_SCALE = HEAD_DIM ** -0.5
LOG2E = math.log2(math.e)
Q_SCALE = ATTN_SCALE * LOG2E

NSA_QB = 256
NSA_QPS = 2
SWA_QB = 128
SWA_QPS = 8
AUG = 128
MASK_LANES = AUG - HEAD_DIM
V_ROWS = HEAD_DIM + 16
CMP_LEAD = 16
FAR_TK = 256
N_GATE = 3 * N_HEADS
VMEM_LIMIT = 56 * 1024 * 1024

_NT = (((1,), (1,)), ((), ()))


def _cparams(sem):
    return pltpu.CompilerParams(dimension_semantics=sem, vmem_limit_bytes=VMEM_LIMIT)


def _ada_kernel(c_ref, w_ref, b_ref, o_ref):
    c = c_ref[...]
    cond = c * jax.nn.sigmoid(c)
    o_ref[0] = jnp.dot(cond, w_ref[0], preferred_element_type=F32,
                       precision=lax.Precision.HIGHEST) + b_ref[0]


def _ada(c, ada_w, ada_b):
    B, D = c.shape
    n6 = ada_w.shape[-1]
    tn = 1536
    return pl.pallas_call(
        _ada_kernel,
        grid=(DEPTH, n6 // tn),
        in_specs=[pl.BlockSpec((B, D), lambda l, n: (0, 0)),
                  pl.BlockSpec((1, D, tn), lambda l, n: (l, 0, n)),
                  pl.BlockSpec((1, 1, tn), lambda l, n: (l, 0, n))],
        out_specs=pl.BlockSpec((1, B, tn), lambda l, n: (l, 0, n)),
        out_shape=jax.ShapeDtypeStruct((DEPTH, B, n6), F32),
        compiler_params=_cparams(("arbitrary", "arbitrary")),
        name="ada",
    )(c, ada_w, ada_b.reshape(DEPTH, 1, n6))


def _store_q_blocks(q_ref, qT, ts, qb):
    zeros = jnp.zeros((MASK_LANES, HPG * qb), BF16)
    for g in range(N_KV):
        for k in range(ts // qb):
            for j in range(HPG):
                r0 = (g * HPG + j) * HEAD_DIM
                q_ref[0, g, k, 0:HEAD_DIM, j * qb:(j + 1) * qb] = (
                    qT[r0:r0 + HEAD_DIM, k * qb:(k + 1) * qb].astype(BF16))
            q_ref[0, g, k, HEAD_DIM:AUG, :] = zeros


def _store_vt(vt_ref, vT, ts):
    ones = jnp.ones((V_ROWS - HEAD_DIM, ts), BF16)
    for g in range(N_KV):
        vt_ref[0, g, 0:HEAD_DIM, :] = vT[g * HEAD_DIM:(g + 1) * HEAD_DIM].astype(BF16)
        vt_ref[0, g, HEAD_DIM:V_ROWS, :] = ones


def _nsa_proj_kernel(x_ref, sh_ref, sc_ref, wt_ref, ws_ref, q_ref, gt_ref, ks_ref, vst_ref, kw_ref,
                     vwt_ref, ck_ref, cv_ref, cmp_scr, *, ts):
    s = pl.program_id(1)
    h = (x_ref[0] * (1.0 + sc_ref[0]) + sh_ref[0]).astype(BF16)
    yT = lax.dot_general(wt_ref[...], h, _NT, preferred_element_type=F32)
    y = jnp.dot(h, ws_ref[...], preferred_element_type=F32)
    _store_q_blocks(q_ref, yT[:Q_WIDTH], ts, NSA_QB)
    _store_vt(vst_ref, yT[Q_WIDTH:Q_WIDTH + KV_WIDTH], ts)
    _store_vt(vwt_ref, yT[Q_WIDTH + KV_WIDTH:Q_WIDTH + 2 * KV_WIDTH], ts)
    gates = jax.nn.sigmoid(yT[Q_WIDTH + 2 * KV_WIDTH:])
    for k in range(ts // NSA_QB):
        gt_ref[0, k] = gates[:, k * NSA_QB:(k + 1) * NSA_QB]
    lane = lax.broadcasted_iota(jnp.int32, (ts, AUG), 1)
    tok = s * ts + lax.broadcasted_iota(jnp.int32, (ts, AUG), 0)
    onehot = (lane - HEAD_DIM) == tok // SLC_LEN
    for g in range(N_KV):
        ks_ref[0, g] = jnp.where(onehot, 1.0, y[:, g * AUG:(g + 1) * AUG]).astype(BF16)
        kw_ref[0, g] = y[:, (N_KV + g) * AUG:(N_KV + g + 1) * AUG].astype(BF16)
    base = 2 * N_KV * AUG
    for c in range(2 * KV_WIDTH // 128):
        cmp_scr[c] = y[:, base + c * 128:base + (c + 1) * 128]
    for c, dst in enumerate([ck_ref, ck_ref, cv_ref, cv_ref]):
        for r in range(CMP_STRIDE):
            piece = cmp_scr[c, pl.ds(r, ts // CMP_STRIDE, stride=CMP_STRIDE), :]
            for half in range(2):
                g = 2 * (c % 2) + half
                dst[0, g, :, r * HEAD_DIM:(r + 1) * HEAD_DIM] = (
                    piece[:, half * HEAD_DIM:(half + 1) * HEAD_DIM].astype(BF16))


def _pad_heads(w):
    D = w.shape[0]
    w = w.reshape(D, N_KV, HEAD_DIM)
    return jnp.pad(w, ((0, 0), (0, 0), (0, MASK_LANES))).reshape(D, N_KV * AUG)


def _nsa_proj(x, shift, scale, w_in, ts=512):
    B, S, D = x.shape
    nqb, lq = S // NSA_QB, HPG * NSA_QB
    cols = lambda i: w_in[:, Q_WIDTH + i * KV_WIDTH:Q_WIDTH + (i + 1) * KV_WIDTH]
    w_kc, w_vc, w_ksl, w_vsl, w_kw, w_vw = [cols(i) for i in range(6)]
    wt = jnp.concatenate([w_in[:, :Q_WIDTH] * Q_SCALE, w_vsl, w_vw,
                          w_in[:, Q_WIDTH + 6 * KV_WIDTH:]], axis=1).T.astype(BF16)
    ws = jnp.concatenate([_pad_heads(w_ksl), _pad_heads(w_kw), w_kc, w_vc], axis=1).astype(BF16)
    per_b = pl.BlockSpec((1, 1, D), lambda b, s: (b, 0, 0))
    kspec = pl.BlockSpec((1, N_KV, ts, AUG), lambda b, s: (b, 0, s, 0))
    vspec = pl.BlockSpec((1, N_KV, V_ROWS, ts), lambda b, s: (b, 0, 0, s))
    k_shape = jax.ShapeDtypeStruct((B, N_KV, S, AUG), BF16)
    v_shape = jax.ShapeDtypeStruct((B, N_KV, V_ROWS, S), BF16)
    chunk = CMP_STRIDE * HEAD_DIM
    cspec = pl.BlockSpec((1, N_KV, ts // CMP_STRIDE, chunk), lambda b, s: (b, 0, s, 0))
    c_shape = jax.ShapeDtypeStruct((B, N_KV, S // CMP_STRIDE, chunk), BF16)
    return pl.pallas_call(
        functools.partial(_nsa_proj_kernel, ts=ts),
        grid=(B, S // ts),
        in_specs=[pl.BlockSpec((1, ts, D), lambda b, s: (b, s, 0)), per_b, per_b,
                  pl.BlockSpec(wt.shape, lambda b, s: (0, 0)), pl.BlockSpec(ws.shape, lambda b, s: (0, 0))],
        out_specs=[pl.BlockSpec((1, N_KV, ts // NSA_QB, AUG, lq), lambda b, s: (b, 0, s, 0, 0)),
                   pl.BlockSpec((1, ts // NSA_QB, N_GATE, NSA_QB), lambda b, s: (b, s, 0, 0)),
                   kspec, vspec, kspec, vspec,
                   cspec, cspec],
        out_shape=[jax.ShapeDtypeStruct((B, N_KV, nqb, AUG, lq), BF16),
                   jax.ShapeDtypeStruct((B, nqb, N_GATE, NSA_QB), F32),
                   k_shape, v_shape, k_shape, v_shape,
                   c_shape, c_shape],
        scratch_shapes=[pltpu.VMEM((2 * KV_WIDTH // 128, ts, 128), F32)],
        compiler_params=_cparams(("arbitrary", "arbitrary")),
        name="nsa_proj",
    )(x, shift.reshape(B, 1, D), scale.reshape(B, 1, D), wt, ws)


def _swa_proj_kernel(x_ref, sh_ref, sc_ref, wq_ref, wk_ref, wv_ref, q_ref, k_ref, vt_ref, *, ts):
    x = x_ref[0]
    h = (x * (1.0 + sc_ref[0]) + sh_ref[0]).astype(BF16)
    xb = x.astype(BF16)
    _store_q_blocks(q_ref, lax.dot_general(wq_ref[...], h, _NT, preferred_element_type=F32), ts, SWA_QB)
    _store_vt(vt_ref, lax.dot_general(wv_ref[...], xb, _NT, preferred_element_type=F32), ts)
    k = jnp.dot(xb, wk_ref[...], preferred_element_type=F32)
    for g in range(N_KV):
        k_ref[0, g] = k[:, g * AUG:(g + 1) * AUG].astype(BF16)


def _swa_proj(x, shift, scale, w_q, w_kv, ts=512):
    B, S, D = x.shape
    lq = HPG * SWA_QB
    wq = (w_q * Q_SCALE).T.astype(BF16)
    wk = _pad_heads(w_kv[:, :KV_WIDTH]).astype(BF16)
    wv = w_kv[:, KV_WIDTH:].T.astype(BF16)
    per_b = pl.BlockSpec((1, 1, D), lambda b, s: (b, 0, 0))
    full = lambda a: pl.BlockSpec(a.shape, lambda b, s: (0, 0))
    return pl.pallas_call(
        functools.partial(_swa_proj_kernel, ts=ts),
        grid=(B, S // ts),
        in_specs=[pl.BlockSpec((1, ts, D), lambda b, s: (b, s, 0)), per_b, per_b, full(wq), full(wk), full(wv)],
        out_specs=[pl.BlockSpec((1, N_KV, ts // SWA_QB, AUG, lq), lambda b, s: (b, 0, s, 0, 0)),
                   pl.BlockSpec((1, N_KV, ts, AUG), lambda b, s: (b, 0, s, 0)),
                   pl.BlockSpec((1, N_KV, V_ROWS, ts), lambda b, s: (b, 0, 0, s))],
        out_shape=[jax.ShapeDtypeStruct((B, N_KV, S // SWA_QB, AUG, lq), BF16),
                   jax.ShapeDtypeStruct((B, N_KV, S, AUG), BF16),
                   jax.ShapeDtypeStruct((B, N_KV, V_ROWS, S), BF16)],
        compiler_params=_cparams(("arbitrary", "arbitrary")),
        name="swa_proj",
    )(x, shift.reshape(B, 1, D), scale.reshape(B, 1, D), wq, wk, wv)


def _gelu_tanh(x):
    return 0.5 * x * (1.0 + jnp.tanh(math.sqrt(2.0 / math.pi) * (x + 0.044715 * (x * x * x))))


def _compress_kernel(ch_ref, w1_ref, w2_ref, pos_ref, o_ref, *, nc):
    half = CMP_STRIDE * HEAD_DIM
    ch = ch_ref[0]
    w1 = w1_ref[...]
    top = jnp.dot(ch, w1[:half], preferred_element_type=F32)
    bot = jnp.dot(ch, w1[half:], preferred_element_type=F32)
    posw = jnp.dot(pos_ref[...], w1, preferred_element_type=F32)[0:1]
    w2 = w2_ref[...]
    for g in range(N_KV):
        bot_g = pltpu.roll(bot[g * nc:(g + 1) * nc], nc - 1, 0)
        pre = top[g * nc:(g + 1) * nc] + bot_g + posw
        o_ref[0, g] = jnp.dot(_gelu_tanh(pre).astype(BF16), w2, preferred_element_type=F32)


def _compress(chunks, w1, w2, pos):
    B, gn, ck = chunks.shape
    nc = gn // N_KV
    pos8 = jnp.broadcast_to(pos.reshape(1, CMP_LEN * HEAD_DIM), (8, CMP_LEN * HEAD_DIM)).astype(BF16)
    return pl.pallas_call(
        functools.partial(_compress_kernel, nc=nc),
        grid=(B,),
        in_specs=[pl.BlockSpec((1, gn, ck), lambda b: (b, 0, 0)),
                  pl.BlockSpec(w1.shape, lambda b: (0, 0)),
                  pl.BlockSpec(w2.shape, lambda b: (0, 0)),
                  pl.BlockSpec(pos8.shape, lambda b: (0, 0))],
        out_specs=pl.BlockSpec((1, N_KV, nc, HEAD_DIM), lambda b: (b, 0, 0, 0)),
        out_shape=jax.ShapeDtypeStruct((B, N_KV, nc, HEAD_DIM), F32),
        compiler_params=_cparams(("arbitrary",)),
        name="compress",
    )(chunks, w1.astype(BF16), w2.astype(BF16), pos8)


def _bucket_np(d):
    d = np.maximum(np.asarray(d, np.int64), 0)
    max_exact = N_BUCKETS // 2
    large = max_exact + (np.log(np.maximum(d, 1).astype(np.float32) / np.float32(max_exact))
                         / np.float32(math.log(MAX_DISTANCE / max_exact))
                         * np.float32(N_BUCKETS - max_exact)).astype(np.int32)
    large = np.minimum(large, N_BUCKETS - 1)
    return np.where(d < max_exact, d, large).astype(np.int32)


def _shifted_bias(rel_table):
    t = rel_table.astype(F32)
    return ((t[_bucket_np(np.arange(MAX_DISTANCE))] - t[N_BUCKETS - 1][None, :]) * LOG2E).T


def _to_group_lanes(m):
    H, K, Q = m.shape
    return m.reshape(N_KV, HPG, K, Q).transpose(0, 2, 1, 3).reshape(N_KV, K, HPG * Q)


def _bias_of_dist(fb, dist, window):
    ok = (dist >= 0) & (dist < window)
    idx = np.clip(dist, 0, MAX_DISTANCE - 1)
    return jnp.where(ok[None], jnp.where((dist >= MAX_DISTANCE)[None], 0.0, fb[:, idx]), NEG)


def _toeplitz_pair(fb, window):
    H, t = fb.shape[0], MAX_DISTANCE
    nv = 3 * t - 1
    v = _bias_of_dist(fb, np.arange(nv) - (t - 1), window)
    rows = 2 * t
    flat = jnp.tile(v, (1, rows + 1))[:, :rows * (nv + 1)]
    return flat.reshape(H, rows, nv + 1)[:, ::-1, :t]


def _near_band(fb, window, qb):
    H, t = fb.shape[0], MAX_DISTANCE
    nb = qb // t
    assert qb % t == 0 and (nb == 1 or window >= 2 * qb)
    pair = _toeplitz_pair(fb, window)
    blocks = {1: pair[:, :t], 0: pair[:, t:]}
    zero, masked = jnp.zeros((H, t, t), F32), jnp.full((H, t, t), NEG, F32)
    rows = []
    for a in range(2 * nb):
        row = [blocks.get(b + nb - a, zero if b + nb - a > 1 else masked) for b in range(nb)]
        rows.append(jnp.concatenate(row, axis=2))
    return _to_group_lanes(jnp.concatenate(rows, axis=1))


def _masked_rows(n, qb):
    return jnp.full((N_KV, n, HPG * qb), NEG, F32)


def _window_table(fb, qb):
    kk = np.arange(qb)[:, None]
    qi = np.arange(HPG * qb)[None, :] % qb
    edge = np.where(kk > qi, 0.0, NEG).astype(np.float32)
    edge = jnp.broadcast_to(jnp.asarray(edge), (N_KV, qb, HPG * qb))
    mid = jnp.zeros((N_KV, NSA_WINDOW - 2 * qb, HPG * qb), F32)
    return jnp.concatenate([edge, mid, _near_band(fb, NSA_WINDOW, qb), _masked_rows(NSA_WINDOW, qb)], axis=1)


def _cmp_band_rows(qb):
    return CMP_LEAD + qb // CMP_STRIDE


def _cmp_band(fb, nc, qb):
    m = np.arange(_cmp_band_rows(qb))[:, None]
    i = np.arange(qb)[None, :]
    dist = i + CMP_STRIDE * CMP_LEAD - CMP_STRIDE * m - (CMP_LEN - 1)
    band = _to_group_lanes(_bias_of_dist(fb, dist, 1 << 30))
    return jnp.concatenate([jnp.zeros((N_KV, nc, HPG * qb), F32), band, _masked_rows(nc, qb)], axis=1)


def _to_token_rows(oT, qb):
    pair = lambda a: jnp.concatenate([oT[:, a * qb:(a + 1) * qb], oT[:, (a + 1) * qb:(a + 2) * qb]], axis=0)
    return jnp.concatenate([pair(0).T, pair(2).T], axis=1)


def _nsa_kernel(q_ref, gt_ref, kc_ref, vct_ref, ks_ref, vst_ref, kw_ref, vwt_ref, ovl_ref, cb_ref,
                wt_ref, o_ref, acc_ref, ow_ref, sa_ref, sb_ref, pa_ref, pb_ref, *, nc):
    QB, LQ, C = NSA_QB, HPG * NSA_QB, NSA_QPS
    g = pl.program_id(1)
    chains = range(C)
    qbs = [pl.program_id(2) * C + c for c in chains]
    s0s = [qb * QB for qb in qbs]
    qTs = [q_ref[0, 0, c] for c in chains]

    def gate(c, branch):
        rows = [gt_ref[0, c, pl.ds(branch * N_HEADS + g * HPG + j, 1), :] for j in range(HPG)]
        return jnp.concatenate(rows, axis=1)

    scs, sws, w0s = [], [], []
    for c in chains:
        cstart = pl.multiple_of(nc + CMP_LEAD - (QB // CMP_STRIDE) * qbs[c], 8)
        scs.append(jnp.dot(kc_ref[0, 0], qTs[c], preferred_element_type=F32)
                   + cb_ref[0, pl.ds(cstart, nc), :])
    win_keys = NSA_WINDOW + QB
    for c in chains:
        w0 = pl.multiple_of(jnp.maximum(s0s[c] - NSA_WINDOW, 0), QB)
        t0 = pl.multiple_of(w0 - (s0s[c] - NSA_WINDOW), QB)
        w0s.append(w0)
        sws.append(jnp.dot(kw_ref[0, 0, pl.ds(w0, win_keys), :], qTs[c], preferred_element_type=F32)
                   + wt_ref[0, pl.ds(t0, win_keys), :])

    p_cs = []
    for c in chains:
        m_c = jnp.max(scs[c], axis=0, keepdims=True)
        e_c = jnp.exp2(scs[c] - m_c)
        l_c = jnp.sum(e_c, axis=0, keepdims=True)
        p_cs.append(e_c * jnp.where(m_c > 0.1 * NEG, 1.0 / l_c, 0.0))
    o_cmps, imps = [], []
    for c in chains:
        o_cmps.append(jnp.dot(vct_ref[0, 0], p_cs[c].astype(BF16), preferred_element_type=F32)[:HEAD_DIM])
        p_sum = functools.reduce(jnp.add, [p_cs[c][:, j * QB:(j + 1) * QB] for j in range(HPG)])
        imps.append(jnp.dot(ovl_ref[...], p_sum, preferred_element_type=F32,
                            precision=lax.Precision.HIGHEST)[HEAD_DIM:])

    p_ws = []
    for c in chains:
        m_w = jnp.max(sws[c], axis=0, keepdims=True)
        p_ws.append(jnp.exp2(sws[c] - m_w).astype(BF16))
    acc_ws = [jnp.dot(vwt_ref[0, 0, :, pl.ds(w0s[c], win_keys)], p_ws[c], preferred_element_type=F32)
              for c in chains]

    blk = lax.broadcasted_iota(jnp.int32, (SLC_LEN, QB), 0)
    sub = lax.broadcasted_iota(jnp.int32, (8, QB), 0)
    q_nears, q_fars = [], []
    for c in chains:
        tq = s0s[c] + lax.broadcasted_iota(jnp.int32, (SLC_LEN, QB), 1)
        cur = tq // SLC_LEN
        forced = (blk == 0) | (blk == cur) | (blk == cur - 1)
        future = blk > cur
        val = jnp.where(forced, 1e9, jnp.where(future, -1e9, imps[c]))
        vals = [val[8 * a:8 * a + 8] for a in range(8)]
        cnts = [jnp.zeros((8, QB), F32) for _ in range(8)]
        for jp in range(SLC_LEN):
            ap, r = divmod(jp, 8)
            rv = jnp.broadcast_to(vals[ap][r:r + 1], (8, QB))
            for a in range(8):
                gt = jnp.where(rv > vals[a], 1.0, 0.0)
                ge = jnp.where(rv >= vals[a], 1.0, 0.0)
                if a < ap:
                    inc = gt
                elif a > ap:
                    inc = ge
                else:
                    inc = jnp.where(sub > r, ge, gt)
                cnts[a] = cnts[a] + inc
        cnt = jnp.concatenate(cnts, axis=0)
        allowed = (cnt < float(N_SELECT)) & (blk <= cur)
        near_blk = blk >= (s0s[c] - QB) // SLC_LEN
        neg_near = jnp.where(allowed, 0.0, NEG).astype(BF16)
        neg_far = jnp.where(allowed & jnp.logical_not(near_blk), 0.0, NEG).astype(BF16)
        q_nears.append(jnp.concatenate([qTs[c][:HEAD_DIM], jnp.tile(neg_near, (1, HPG))], axis=0))
        q_fars.append(jnp.concatenate([qTs[c][:HEAD_DIM], jnp.tile(neg_far, (1, HPG))], axis=0))
    for c in chains:
        ow_ref[c] = gate(c, 2) * (acc_ws[c][:HEAD_DIM] / acc_ws[c][HEAD_DIM:HEAD_DIM + 1])

    last_tile = ks_ref.shape[2] // FAR_TK - 1

    def tile_start(t):
        return pl.multiple_of(jnp.minimum(t, last_tile) * FAR_TK, FAR_TK)

    def qk(c, t):
        return jnp.dot(ks_ref[0, 0, pl.ds(tile_start(t), FAR_TK), :], q_fars[c], preferred_element_type=F32)

    def softmax_update(s, m_run):
        m_new = jnp.maximum(m_run, jnp.max(s, axis=0, keepdims=True))
        return jnp.exp2(s - m_new).astype(BF16), jnp.exp2(m_run - m_new), m_new

    def pv_update(c, t, p, alpha):
        pv = jnp.dot(vst_ref[0, 0, :, pl.ds(tile_start(t), FAR_TK)], p, preferred_element_type=F32)
        acc_ref[c] = acc_ref[c] * alpha + pv

    n0s, s_nears = [], []
    for c in chains:
        n0 = pl.multiple_of(jnp.maximum(s0s[c] - QB, 0), QB)
        b0 = pl.multiple_of(NSA_WINDOW - QB + n0 - (s0s[c] - QB), QB)
        n0s.append(n0)
        s_nears.append(jnp.dot(ks_ref[0, 0, pl.ds(n0, 2 * QB), :], q_nears[c], preferred_element_type=F32)
                       + wt_ref[0, pl.ds(b0, 2 * QB), :])
    for c in chains:
        sa_ref[c] = qk(c, 0)
        sb_ref[c] = qk(c, 1)
    m_nears = [jnp.max(s, axis=0, keepdims=True) for s in s_nears]
    p_nears = [jnp.exp2(s - m).astype(BF16) for s, m in zip(s_nears, m_nears)]
    for c in chains:
        acc_ref[c] = jnp.dot(vst_ref[0, 0, :, pl.ds(n0s[c], 2 * QB)], p_nears[c],
                             preferred_element_type=F32)
    carries = []
    for c in chains:
        p0, alpha0, m0 = softmax_update(sa_ref[c], m_nears[c])
        pa_ref[c] = p0
        carries.append((alpha0, m0))

    for c in chains:
        def far_body(i, carry, c=c):
            alpha_a, m_run = carry
            t = 2 * i
            pv_update(c, t, pa_ref[c], alpha_a)
            p_b, alpha_b, m_run = softmax_update(sb_ref[c], m_run)
            pb_ref[c] = p_b
            sa_ref[c] = qk(c, t + 2)
            pv_update(c, t + 1, pb_ref[c], alpha_b)
            p_a, alpha_a, m_run = softmax_update(sa_ref[c], m_run)
            pa_ref[c] = p_a
            sb_ref[c] = qk(c, t + 3)
            return alpha_a, m_run

        n_far = (jnp.maximum(s0s[c] - QB, 0) + FAR_TK - 1) // FAR_TK
        lax.fori_loop(0, (n_far + 1) // 2, far_body, carries[c])

    for c in chains:
        a_c = acc_ref[c]
        o_sel = a_c[:HEAD_DIM] / a_c[HEAD_DIM:HEAD_DIM + 1]
        o = gate(c, 0) * o_cmps[c] + gate(c, 1) * o_sel + ow_ref[c]
        o_ref[0, c * QB:(c + 1) * QB, :] = _to_token_rows(o, QB).astype(o_ref.dtype)


def _nsa_attention(qT, gatesT, kc_aug, vcT, ks_aug, vsT, kw_aug, vwT, ovl, cband, wtable):
    B, G, nqb = qT.shape[:3]
    QB, LQ, C = NSA_QB, HPG * NSA_QB, NSA_QPS
    nc = kc_aug.shape[2]
    per_bg = lambda a: pl.BlockSpec((1, 1) + a.shape[2:], lambda b, g, q: (b, g) + (0,) * (a.ndim - 2))
    per_g = lambda a: pl.BlockSpec((1,) + a.shape[1:], lambda b, g, q: (g,) + (0,) * (a.ndim - 1))
    return pl.pallas_call(
        functools.partial(_nsa_kernel, nc=nc),
        grid=(B, G, nqb // C),
        in_specs=[pl.BlockSpec((1, 1, C, AUG, LQ), lambda b, g, q: (b, g, q, 0, 0)),
                  pl.BlockSpec((1, C, N_GATE, QB), lambda b, g, q: (b, q, 0, 0)),
                  per_bg(kc_aug), per_bg(vcT), per_bg(ks_aug), per_bg(vsT),
                  per_bg(kw_aug), per_bg(vwT), pl.BlockSpec(ovl.shape, lambda b, g, q: (0, 0)),
                  per_g(cband), per_g(wtable)],
        out_specs=pl.BlockSpec((1, C * QB, HPG * HEAD_DIM), lambda b, g, q: (b, q, g)),
        out_shape=jax.ShapeDtypeStruct((B, nqb * QB, Q_WIDTH), BF16),
        scratch_shapes=[pltpu.VMEM((C, V_ROWS, LQ), F32), pltpu.VMEM((C, HEAD_DIM, LQ), F32),
                        pltpu.VMEM((C, FAR_TK, LQ), F32), pltpu.VMEM((C, FAR_TK, LQ), F32),
                        pltpu.VMEM((C, FAR_TK, LQ), BF16), pltpu.VMEM((C, FAR_TK, LQ), BF16)],
        compiler_params=_cparams(("arbitrary", "arbitrary", "arbitrary")),
        name="nsa_attention",
    )(qT, gatesT, kc_aug, vcT, ks_aug, vsT, kw_aug, vwT, ovl, cband, wtable)


def _swa_kernel(q_ref, k_ref, vt_ref, band_ref, sink_ref, o_ref):
    QB = SWA_QB
    sink = sink_ref[0, 0:1]
    starts, scores = [], []
    for i in range(SWA_QPS):
        s0 = (pl.program_id(2) * SWA_QPS + i) * QB
        n0 = pl.multiple_of(jnp.maximum(s0 - QB, 0), QB)
        b0 = pl.multiple_of(n0 - (s0 - QB), QB)
        starts.append(n0)
        scores.append(jnp.dot(k_ref[0, 0, pl.ds(n0, 2 * QB), :], q_ref[0, 0, i], preferred_element_type=F32)
                      + band_ref[0, pl.ds(b0, 2 * QB), :])
    maxes = [jnp.maximum(jnp.max(s, axis=0, keepdims=True), sink) for s in scores]
    probs = [jnp.exp2(s - m).astype(BF16) for s, m in zip(scores, maxes)]
    accs = [jnp.dot(vt_ref[0, 0, :, pl.ds(n0, 2 * QB)], p, preferred_element_type=F32)
            for n0, p in zip(starts, probs)]
    for i, (acc, m) in enumerate(zip(accs, maxes)):
        denom = acc[HEAD_DIM:HEAD_DIM + 1] + jnp.exp2(sink - m)
        o_ref[0, i * QB:(i + 1) * QB, :] = _to_token_rows(acc[:HEAD_DIM] / denom, QB).astype(o_ref.dtype)


def _swa_attention(qT, k_aug, vT, band, sink):
    B, G, nqb = qT.shape[:3]
    QB, LQ = SWA_QB, HPG * SWA_QB
    per_bg = lambda a: pl.BlockSpec((1, 1) + a.shape[2:], lambda b, g, q: (b, g) + (0,) * (a.ndim - 2))
    per_g = lambda a: pl.BlockSpec((1,) + a.shape[1:], lambda b, g, q: (g,) + (0,) * (a.ndim - 1))
    return pl.pallas_call(
        _swa_kernel,
        grid=(B, G, nqb // SWA_QPS),
        in_specs=[pl.BlockSpec((1, 1, SWA_QPS, AUG, LQ), lambda b, g, q: (b, g, q, 0, 0)),
                  per_bg(k_aug), per_bg(vT), per_g(band), per_g(sink)],
        out_specs=pl.BlockSpec((1, SWA_QPS * QB, HPG * HEAD_DIM), lambda b, g, q: (b, q, g)),
        out_shape=jax.ShapeDtypeStruct((B, nqb * QB, Q_WIDTH), BF16),
        compiler_params=_cparams(("arbitrary", "arbitrary", "arbitrary")),
        name="swa_attention",
    )(qT, k_aug, vT, band, sink)


def _layer_norm(y, g, b):
    mu = jnp.mean(y, axis=-1, keepdims=True)
    yc = y - mu
    var = jnp.mean(yc * yc, axis=-1, keepdims=True)
    return yc * lax.rsqrt(var + LN_EPS) * g + b


def _top2_of4(a, b, c, d):
    hi1, lo1 = jnp.maximum(a, b), jnp.minimum(a, b)
    hi2, lo2 = jnp.maximum(c, d), jnp.minimum(c, d)
    return jnp.maximum(hi1, hi2) + jnp.maximum(jnp.minimum(hi1, hi2), jnp.maximum(lo1, lo2))


def _route(s, sb):
    n = EXPERTS_PER_GROUP
    score = [_top2_of4(*sb[n * r:n * r + n]) for r in range(N_GROUPS)]
    best = functools.reduce(jnp.maximum, score)
    taken = jnp.zeros_like(best) > 1.0
    in_grp = []
    for r in range(N_GROUPS):
        pick = (score[r] == best) & jnp.logical_not(taken)
        in_grp.append(pick)
        taken = taken | pick
    gates = []
    for e in range(N_EXPERTS):
        r = e // n
        ahead = jnp.zeros_like(best)
        for f in range(n * r, n * r + n):
            if f != e:
                beats = (sb[f] >= sb[e]) if f < e else (sb[f] > sb[e])
                ahead = ahead + jnp.where(beats, 1.0, 0.0)
        gates.append(jnp.where(in_grp[r] & (ahead < 2.0), s[e], 0.0))
    total = functools.reduce(jnp.add, gates)
    return [g / total for g in gates]


def _oproj_kernel(o_ref, x_ref, w_ref, ga_ref, lg_ref, lb_ref, shf_ref, scf_ref, rw_ref, rb_ref,
                  x1_ref, h_ref, gate_ref):
    mix = jnp.dot(o_ref[0], w_ref[...], preferred_element_type=F32)
    x1 = _layer_norm(ALPHA * x_ref[0] + (1.0 + ga_ref[0]) * mix, lg_ref[...], lb_ref[...])
    x1_ref[0] = x1
    h = x1 * (1.0 + scf_ref[0]) + shf_ref[0]
    hb = h.astype(BF16)
    h_ref[0] = hb
    logit = lax.dot_general(rw_ref[...], hb, _NT, preferred_element_type=F32)
    aff = jax.nn.sigmoid(logit)
    biased = aff + rb_ref[...][:, 0:1]
    s = [aff[e:e + 1] for e in range(N_EXPERTS)]
    sb = [biased[e:e + 1] for e in range(N_EXPERTS)]
    gate_ref[0] = jnp.concatenate(_route(s, sb), axis=0)


def _oproj(o, x, w_out, g_a, ln_g, ln_b, sh_f, sc_f, router_w, router_b, ts=512):
    B, S, D = x.shape
    row = lambda a: a.reshape(1, D)
    per_b = pl.BlockSpec((1, 1, D), lambda b, s: (b, 0, 0))
    full = lambda a: pl.BlockSpec(a.shape, lambda b, s: (0,) * a.ndim)
    rwT = router_w.T.astype(BF16)
    rb = jnp.broadcast_to(router_b.reshape(N_EXPERTS, 1), (N_EXPERTS, 128))
    args = (o, x, w_out.astype(BF16), g_a.reshape(B, 1, D), row(ln_g), row(ln_b),
            sh_f.reshape(B, 1, D), sc_f.reshape(B, 1, D), rwT, rb)
    tile = pl.BlockSpec((1, ts, D), lambda b, s: (b, s, 0))
    return pl.pallas_call(
        _oproj_kernel,
        grid=(B, S // ts),
        in_specs=[tile, tile, full(args[2]), per_b, full(args[4]), full(args[5]), per_b, per_b,
                  full(rwT), full(rb)],
        out_specs=[tile, tile, pl.BlockSpec((1, N_EXPERTS, ts), lambda b, s: (b, 0, s))],
        out_shape=[jax.ShapeDtypeStruct((B, S, D), F32), jax.ShapeDtypeStruct((B, S, D), BF16),
                   jax.ShapeDtypeStruct((B, N_EXPERTS, S), F32)],
        compiler_params=_cparams(("arbitrary", "arbitrary")),
        name="oproj_ln_router",
    )(*args)


def _moe_kernel(h_ref, gate_ref, wgu_ref, wd_ref, x_ref, gf_ref, lg_ref, lb_ref, o_ref, acc_ref):
    r = pl.program_id(1)
    n, width = EXPERTS_PER_GROUP, EXPERTS_PER_GROUP * D_EXPERT

    @pl.when(r == 0)
    def _():
        acc_ref[...] = jnp.zeros_like(acc_ref)

    gu = jnp.dot(h_ref[...], wgu_ref[0], preferred_element_type=F32)
    a, u = gu[:, :width], gu[:, width:]
    act = a * jax.nn.sigmoid(a) * u
    gates = gate_ref[...]
    lane = lax.broadcasted_iota(jnp.int32, gates.shape, 1)
    he = []
    for k in range(n):
        gcol = jnp.sum(jnp.where(lane == r * n + k, gates, 0.0), axis=1, keepdims=True)
        he.append((act[:, k * D_EXPERT:(k + 1) * D_EXPERT] * gcol).astype(BF16))
    acc_ref[...] += jnp.dot(jnp.concatenate(he, axis=1), wd_ref[0], preferred_element_type=F32)

    @pl.when(r == N_GROUPS - 1)
    def _():
        y = ALPHA * x_ref[...] + (1.0 + gf_ref[0]) * acc_ref[...]
        o_ref[...] = _layer_norm(y, lg_ref[...], lb_ref[...])


def _group_fused(w):
    E, D, F = w.shape
    w = w.reshape(N_GROUPS, EXPERTS_PER_GROUP, D, F).transpose(0, 2, 1, 3)
    return w.reshape(N_GROUPS, D, EXPERTS_PER_GROUP * F)


def _moe(h, gates, w_gate, w_up, w_down, x, g_f, ln_g, ln_b, ts=512):
    B, S, D = x.shape
    T = B * S
    width = EXPERTS_PER_GROUP * D_EXPERT
    wgu = jnp.concatenate([_group_fused(w_gate), _group_fused(w_up)], axis=-1).astype(BF16)
    wd = w_down.reshape(N_GROUPS, width, D).astype(BF16)
    gl = jnp.pad(gates.transpose(0, 2, 1).reshape(T, N_EXPERTS), ((0, 0), (0, 128 - N_EXPERTS)))
    tile = pl.BlockSpec((ts, D), lambda t, e: (t, 0))
    spt = S // ts
    out = pl.pallas_call(
        _moe_kernel,
        grid=(T // ts, N_GROUPS),
        in_specs=[tile, pl.BlockSpec((ts, 128), lambda t, e: (t, 0)),
                  pl.BlockSpec((1, D, 2 * width), lambda t, e: (e, 0, 0)),
                  pl.BlockSpec((1, width, D), lambda t, e: (e, 0, 0)),
                  tile, pl.BlockSpec((1, 1, D), lambda t, e: (t // spt, 0, 0)),
                  pl.BlockSpec((1, D), lambda t, e: (0, 0)), pl.BlockSpec((1, D), lambda t, e: (0, 0))],
        out_specs=tile,
        out_shape=jax.ShapeDtypeStruct((T, D), F32),
        scratch_shapes=[pltpu.VMEM((ts, D), F32)],
        compiler_params=_cparams(("arbitrary", "arbitrary")),
        name="moe_ln",
    )(h.reshape(T, D), gl, wgu, wd, x.reshape(T, D), g_f.reshape(B, 1, D), ln_g.reshape(1, D),
      ln_b.reshape(1, D))
    return out.reshape(B, S, D)


def _values_t(v):
    B, G, n, _ = v.shape
    return jnp.concatenate([v.transpose(0, 1, 3, 2), jnp.ones((B, G, V_ROWS - HEAD_DIM, n), v.dtype)], axis=2)


def _nsa_layer_attention(x, sh, sc, w_in, k_w1, k_w2, v_w1, v_w2, k_pos, v_pos, fb):
    B, S, _ = x.shape
    nc, n_slc = S // CMP_STRIDE, S // SLC_LEN
    qT, gatesT, ks_aug, vsT, kw_aug, vwT, ck, cv = _nsa_proj(x, sh, sc, w_in)
    merge = lambda t: t.reshape(B, N_KV * nc, CMP_STRIDE * HEAD_DIM)
    k_cmp = _compress(merge(ck), k_w1, k_w2, k_pos)
    v_cmp = _compress(merge(cv), v_w1, v_w2, v_pos)
    kc_aug = jnp.pad(k_cmp.astype(BF16), ((0, 0), (0, 0), (0, 0), (0, MASK_LANES)))
    vcT = _values_t(v_cmp.astype(BF16))

    c_start = np.arange(nc)[None, :] * CMP_STRIDE
    s_start = np.arange(MASK_LANES)[:, None] * SLC_LEN
    ovl = ((c_start < s_start + SLC_LEN) & (c_start + CMP_LEN > s_start)
           & (np.arange(nc)[None, :] < nc - 1) & (np.arange(MASK_LANES)[:, None] < n_slc))
    ovl = np.concatenate([np.zeros((HEAD_DIM, nc)), ovl.astype(np.float64)], axis=0).astype(np.float32)

    return _nsa_attention(qT, gatesT, kc_aug, vcT, ks_aug, vsT, kw_aug, vwT, jnp.asarray(ovl),
                          _cmp_band(fb, nc, NSA_QB), _window_table(fb, NSA_QB))


def _swa_layer_attention(x, sh, sc, w_q, w_kv, sinks, rel_bias, fb):
    qT, k_aug, vT = _swa_proj(x, sh, sc, w_q, w_kv)
    sink = (sinks.astype(F32) - rel_bias.astype(F32)[N_BUCKETS - 1]) * LOG2E
    sink = jnp.broadcast_to(sink.reshape(N_KV, 1, HPG, 1), (N_KV, 8, HPG, SWA_QB))
    sink = sink.reshape(N_KV, 8, HPG * SWA_QB)
    band = jnp.concatenate([_near_band(fb, SWA_WINDOW, SWA_QB), _masked_rows(SWA_QB, SWA_QB)], axis=1)
    return _swa_attention(qT, k_aug, vT, band, sink)


def kernel(x, c, nsa_w_in, cmp_k_w1, cmp_k_w2, cmp_v_w1, cmp_v_w2, cmp_k_pos, cmp_v_pos, nsa_w_out,
           swa_w_q, swa_sinks, swa_w_out, shared_w_kv, rel_bias, router_w, router_b, moe_w_gate,
           moe_w_up, moe_w_down, ada_w, ada_b, ln_g, ln_b):
    B, S, D = x.shape
    ada = _ada(c, ada_w, ada_b)
    fb = _shifted_bias(rel_bias)
    for layer in range(DEPTH):
        sh_a, sc_a, g_a, sh_f, sc_f, g_f = [ada[layer, :, i * D:(i + 1) * D] for i in range(6)]
        if layer == 0:
            o = _nsa_layer_attention(x, sh_a, sc_a, nsa_w_in[0], cmp_k_w1[0], cmp_k_w2[0], cmp_v_w1[0],
                                     cmp_v_w2[0], cmp_k_pos[0], cmp_v_pos[0], fb)
            w_out = nsa_w_out[0]
        else:
            o = _swa_layer_attention(x, sh_a, sc_a, swa_w_q[0], shared_w_kv, swa_sinks[0], rel_bias, fb)
            w_out = swa_w_out[0]
        x1, h, gates = _oproj(o, x, w_out, g_a, ln_g[layer, 0], ln_b[layer, 0], sh_f, sc_f,
                              router_w, router_b)
        x = _moe(h, gates, moe_w_gate[layer], moe_w_up[layer], moe_w_down[layer], x1, g_f,
                 ln_g[layer, 1], ln_b[layer, 1])
    return x
```

```python
import functools
import math

import numpy as np
import jax
import jax.numpy as jnp
from jax import lax
from jax.experimental import pallas as pl
from jax.experimental.pallas import tpu as pltpu

F32 = jnp.float32
BF16 = jnp.bfloat16

D_MODEL = 1024
HEAD_DIM = 64
N_HEADS = 16
N_KV = 4
HPG = 4
Q_WIDTH = N_HEADS * HEAD_DIM
KV_WIDTH = N_KV * HEAD_DIM
CMP_LEN = 32
CMP_STRIDE = 16
CMP_HIDDEN = 256
SLC_LEN = 64
N_SELECT = 16
NSA_WINDOW = 512
SWA_WINDOW = 128
N_BUCKETS = 32
MAX_DISTANCE = 128
N_EXPERTS = 16
N_GROUPS = 4
EXPERTS_PER_GROUP = 4
D_EXPERT = 256
DEPTH = 2
ALPHA = (2.0 * DEPTH) ** 0.25
LN_EPS = 1e-5
NEG = -1e30
ATTN_SCALE = HEAD_DIM ** -0.5
LOG2E = math.log2(math.e)
Q_SCALE = ATTN_SCALE * LOG2E

NSA_QB = 256
NSA_QPS = 2
NSA_CALLS = 4
SWA_QB = 128
SWA_QPS = 8
AUG = 128
MASK_LANES = AUG - HEAD_DIM
V_ROWS = HEAD_DIM + 16
CMP_LEAD = 16
FAR_TK = 256
N_GATE = 3 * N_HEADS
VMEM_LIMIT = 56 * 1024 * 1024

_NT = (((1,), (1,)), ((), ()))


def _cparams(sem):
    return pltpu.CompilerParams(dimension_semantics=sem, vmem_limit_bytes=VMEM_LIMIT)


def _ada_kernel(c_ref, w_ref, b_ref, o_ref):
    c = c_ref[...]
    cond = c * jax.nn.sigmoid(c)
    o_ref[0] = jnp.dot(cond, w_ref[0], preferred_element_type=F32,
                       precision=lax.Precision.HIGHEST) + b_ref[0]


def _ada(c, ada_w, ada_b):
    B, D = c.shape
    n6 = ada_w.shape[-1]
    tn = 1536
    return pl.pallas_call(
        _ada_kernel,
        grid=(DEPTH, n6 // tn),
        in_specs=[pl.BlockSpec((B, D), lambda l, n: (0, 0)),
                  pl.BlockSpec((1, D, tn), lambda l, n: (l, 0, n)),
                  pl.BlockSpec((1, 1, tn), lambda l, n: (l, 0, n))],
        out_specs=pl.BlockSpec((1, B, tn), lambda l, n: (l, 0, n)),
        out_shape=jax.ShapeDtypeStruct((DEPTH, B, n6), F32),
        compiler_params=_cparams(("arbitrary", "arbitrary")),
        name="ada",
    )(c, ada_w, ada_b.reshape(DEPTH, 1, n6))


def _store_q_blocks(q_ref, qT, ts, qb):
    zeros = jnp.zeros((MASK_LANES, HPG * qb), BF16)
    for g in range(N_KV):
        for k in range(ts // qb):
            for j in range(HPG):
                r0 = (g * HPG + j) * HEAD_DIM
                q_ref[0, g, k, 0:HEAD_DIM, j * qb:(j + 1) * qb] = (
                    qT[r0:r0 + HEAD_DIM, k * qb:(k + 1) * qb].astype(BF16))
            q_ref[0, g, k, HEAD_DIM:AUG, :] = zeros


def _store_vt(vt_ref, vT, ts):
    ones = jnp.ones((V_ROWS - HEAD_DIM, ts), BF16)
    for g in range(N_KV):
        vt_ref[0, g, 0:HEAD_DIM, :] = vT[g * HEAD_DIM:(g + 1) * HEAD_DIM].astype(BF16)
        vt_ref[0, g, HEAD_DIM:V_ROWS, :] = ones


def _nsa_proj_kernel(x_ref, sh_ref, sc_ref, wt_ref, ws_ref, q_ref, gt_ref, ks_ref, vst_ref, kw_ref,
                     vwt_ref, ck_ref, cv_ref, cmp_scr, *, ts):
    s = pl.program_id(1)
    h = (x_ref[0] * (1.0 + sc_ref[0]) + sh_ref[0]).astype(BF16)
    yT = lax.dot_general(wt_ref[...], h, _NT, preferred_element_type=F32)
    y = jnp.dot(h, ws_ref[...], preferred_element_type=F32)
    _store_q_blocks(q_ref, yT[:Q_WIDTH], ts, NSA_QB)
    _store_vt(vst_ref, yT[Q_WIDTH:Q_WIDTH + KV_WIDTH], ts)
    _store_vt(vwt_ref, yT[Q_WIDTH + KV_WIDTH:Q_WIDTH + 2 * KV_WIDTH], ts)
    gates = jax.nn.sigmoid(yT[Q_WIDTH + 2 * KV_WIDTH:])
    for k in range(ts // NSA_QB):
        gt_ref[0, k] = gates[:, k * NSA_QB:(k + 1) * NSA_QB]
    lane = lax.broadcasted_iota(jnp.int32, (ts, AUG), 1)
    tok = s * ts + lax.broadcasted_iota(jnp.int32, (ts, AUG), 0)
    onehot = (lane - HEAD_DIM) == tok // SLC_LEN
    for g in range(N_KV):
        ks_ref[0, g] = jnp.where(onehot, 1.0, y[:, g * AUG:(g + 1) * AUG]).astype(BF16)
        kw_ref[0, g] = y[:, (N_KV + g) * AUG:(N_KV + g + 1) * AUG].astype(BF16)
    base = 2 * N_KV * AUG
    for c in range(2 * KV_WIDTH // 128):
        cmp_scr[c] = y[:, base + c * 128:base + (c + 1) * 128]
    for c, dst in enumerate([ck_ref, ck_ref, cv_ref, cv_ref]):
        for r in range(CMP_STRIDE):
            piece = cmp_scr[c, pl.ds(r, ts // CMP_STRIDE, stride=CMP_STRIDE), :]
            for half in range(2):
                g = 2 * (c % 2) + half
                dst[0, g, :, r * HEAD_DIM:(r + 1) * HEAD_DIM] = (
                    piece[:, half * HEAD_DIM:(half + 1) * HEAD_DIM].astype(BF16))


def _pad_heads(w):
    D = w.shape[0]
    w = w.reshape(D, N_KV, HEAD_DIM)
    return jnp.pad(w, ((0, 0), (0, 0), (0, MASK_LANES))).reshape(D, N_KV * AUG)


def _nsa_proj(x, shift, scale, w_in, ts=512):
    B, S, D = x.shape
    nqb, lq = S // NSA_QB, HPG * NSA_QB
    cols = lambda i: w_in[:, Q_WIDTH + i * KV_WIDTH:Q_WIDTH + (i + 1) * KV_WIDTH]
    w_kc, w_vc, w_ksl, w_vsl, w_kw, w_vw = [cols(i) for i in range(6)]
    wt = jnp.concatenate([w_in[:, :Q_WIDTH] * Q_SCALE, w_vsl, w_vw,
                          w_in[:, Q_WIDTH + 6 * KV_WIDTH:]], axis=1).T.astype(BF16)
    ws = jnp.concatenate([_pad_heads(w_ksl), _pad_heads(w_kw), w_kc, w_vc], axis=1).astype(BF16)
    per_b = pl.BlockSpec((1, 1, D), lambda b, s: (b, 0, 0))
    kspec = pl.BlockSpec((1, N_KV, ts, AUG), lambda b, s: (b, 0, s, 0))
    vspec = pl.BlockSpec((1, N_KV, V_ROWS, ts), lambda b, s: (b, 0, 0, s))
    k_shape = jax.ShapeDtypeStruct((B, N_KV, S, AUG), BF16)
    v_shape = jax.ShapeDtypeStruct((B, N_KV, V_ROWS, S), BF16)
    chunk = CMP_STRIDE * HEAD_DIM
    cspec = pl.BlockSpec((1, N_KV, ts // CMP_STRIDE, chunk), lambda b, s: (b, 0, s, 0))
    c_shape = jax.ShapeDtypeStruct((B, N_KV, S // CMP_STRIDE, chunk), BF16)
    return pl.pallas_call(
        functools.partial(_nsa_proj_kernel, ts=ts),
        grid=(B, S // ts),
        in_specs=[pl.BlockSpec((1, ts, D), lambda b, s: (b, s, 0)), per_b, per_b,
                  pl.BlockSpec(wt.shape, lambda b, s: (0, 0)), pl.BlockSpec(ws.shape, lambda b, s: (0, 0))],
        out_specs=[pl.BlockSpec((1, N_KV, ts // NSA_QB, AUG, lq), lambda b, s: (b, 0, s, 0, 0)),
                   pl.BlockSpec((1, ts // NSA_QB, N_GATE, NSA_QB), lambda b, s: (b, s, 0, 0)),
                   kspec, vspec, kspec, vspec,
                   cspec, cspec],
        out_shape=[jax.ShapeDtypeStruct((B, N_KV, nqb, AUG, lq), BF16),
                   jax.ShapeDtypeStruct((B, nqb, N_GATE, NSA_QB), F32),
                   k_shape, v_shape, k_shape, v_shape,
                   c_shape, c_shape],
        scratch_shapes=[pltpu.VMEM((2 * KV_WIDTH // 128, ts, 128), F32)],
        compiler_params=_cparams(("arbitrary", "arbitrary")),
        name="nsa_proj",
    )(x, shift.reshape(B, 1, D), scale.reshape(B, 1, D), wt, ws)


def _swa_proj_kernel(x_ref, sh_ref, sc_ref, wq_ref, wk_ref, wv_ref, q_ref, k_ref, vt_ref, *, ts):
    x = x_ref[0]
    h = (x * (1.0 + sc_ref[0]) + sh_ref[0]).astype(BF16)
    xb = x.astype(BF16)
    _store_q_blocks(q_ref, lax.dot_general(wq_ref[...], h, _NT, preferred_element_type=F32), ts, SWA_QB)
    _store_vt(vt_ref, lax.dot_general(wv_ref[...], xb, _NT, preferred_element_type=F32), ts)
    k = jnp.dot(xb, wk_ref[...], preferred_element_type=F32)
    for g in range(N_KV):
        k_ref[0, g] = k[:, g * AUG:(g + 1) * AUG].astype(BF16)


def _swa_proj(x, shift, scale, w_q, w_kv, ts=512):
    B, S, D = x.shape
    lq = HPG * SWA_QB
    wq = (w_q * Q_SCALE).T.astype(BF16)
    wk = _pad_heads(w_kv[:, :KV_WIDTH]).astype(BF16)
    wv = w_kv[:, KV_WIDTH:].T.astype(BF16)
    per_b = pl.BlockSpec((1, 1, D), lambda b, s: (b, 0, 0))
    full = lambda a: pl.BlockSpec(a.shape, lambda b, s: (0, 0))
    return pl.pallas_call(
        functools.partial(_swa_proj_kernel, ts=ts),
        grid=(B, S // ts),
        in_specs=[pl.BlockSpec((1, ts, D), lambda b, s: (b, s, 0)), per_b, per_b, full(wq), full(wk), full(wv)],
        out_specs=[pl.BlockSpec((1, N_KV, ts // SWA_QB, AUG, lq), lambda b, s: (b, 0, s, 0, 0)),
                   pl.BlockSpec((1, N_KV, ts, AUG), lambda b, s: (b, 0, s, 0)),
                   pl.BlockSpec((1, N_KV, V_ROWS, ts), lambda b, s: (b, 0, 0, s))],
        out_shape=[jax.ShapeDtypeStruct((B, N_KV, S // SWA_QB, AUG, lq), BF16),
                   jax.ShapeDtypeStruct((B, N_KV, S, AUG), BF16),
                   jax.ShapeDtypeStruct((B, N_KV, V_ROWS, S), BF16)],
        compiler_params=_cparams(("arbitrary", "arbitrary")),
        name="swa_proj",
    )(x, shift.reshape(B, 1, D), scale.reshape(B, 1, D), wq, wk, wv)


def _gelu_tanh(x):
    return 0.5 * x * (1.0 + jnp.tanh(math.sqrt(2.0 / math.pi) * (x + 0.044715 * (x * x * x))))


def _compress_kernel(ch_ref, w1_ref, w2_ref, pos_ref, o_ref, *, nc):
    half = CMP_STRIDE * HEAD_DIM
    ch = ch_ref[0]
    w1 = w1_ref[...]
    top = jnp.dot(ch, w1[:half], preferred_element_type=F32)
    bot = jnp.dot(ch, w1[half:], preferred_element_type=F32)
    posw = jnp.dot(pos_ref[...], w1, preferred_element_type=F32)[0:1]
    w2 = w2_ref[...]
    for g in range(N_KV):
        bot_g = pltpu.roll(bot[g * nc:(g + 1) * nc], nc - 1, 0)
        pre = top[g * nc:(g + 1) * nc] + bot_g + posw
        o_ref[0, g] = jnp.dot(_gelu_tanh(pre).astype(BF16), w2, preferred_element_type=F32)


def _compress(chunks, w1, w2, pos):
    B, gn, ck = chunks.shape
    nc = gn // N_KV
    pos8 = jnp.broadcast_to(pos.reshape(1, CMP_LEN * HEAD_DIM), (8, CMP_LEN * HEAD_DIM)).astype(BF16)
    return pl.pallas_call(
        functools.partial(_compress_kernel, nc=nc),
        grid=(B,),
        in_specs=[pl.BlockSpec((1, gn, ck), lambda b: (b, 0, 0)),
                  pl.BlockSpec(w1.shape, lambda b: (0, 0)),
                  pl.BlockSpec(w2.shape, lambda b: (0, 0)),
                  pl.BlockSpec(pos8.shape, lambda b: (0, 0))],
        out_specs=pl.BlockSpec((1, N_KV, nc, HEAD_DIM), lambda b: (b, 0, 0, 0)),
        out_shape=jax.ShapeDtypeStruct((B, N_KV, nc, HEAD_DIM), F32),
        compiler_params=_cparams(("arbitrary",)),
        name="compress",
    )(chunks, w1.astype(BF16), w2.astype(BF16), pos8)


def _bucket_np(d):
    d = np.maximum(np.asarray(d, np.int64), 0)
    max_exact = N_BUCKETS // 2
    large = max_exact + (np.log(np.maximum(d, 1).astype(np.float32) / np.float32(max_exact))
                         / np.float32(math.log(MAX_DISTANCE / max_exact))
                         * np.float32(N_BUCKETS - max_exact)).astype(np.int32)
    large = np.minimum(large, N_BUCKETS - 1)
    return np.where(d < max_exact, d, large).astype(np.int32)


def _shifted_bias(rel_table):
    t = rel_table.astype(F32)
    return ((t[_bucket_np(np.arange(MAX_DISTANCE))] - t[N_BUCKETS - 1][None, :]) * LOG2E).T


def _to_group_lanes(m):
    H, K, Q = m.shape
    return m.reshape(N_KV, HPG, K, Q).transpose(0, 2, 1, 3).reshape(N_KV, K, HPG * Q)


def _bias_of_dist(fb, dist, window):
    ok = (dist >= 0) & (dist < window)
    idx = np.clip(dist, 0, MAX_DISTANCE - 1)
    return jnp.where(ok[None], jnp.where((dist >= MAX_DISTANCE)[None], 0.0, fb[:, idx]), NEG)


def _toeplitz_pair(fb, window):
    H, t = fb.shape[0], MAX_DISTANCE
    nv = 3 * t - 1
    v = _bias_of_dist(fb, np.arange(nv) - (t - 1), window)
    rows = 2 * t
    flat = jnp.tile(v, (1, rows + 1))[:, :rows * (nv + 1)]
    return flat.reshape(H, rows, nv + 1)[:, ::-1, :t]


def _near_band(fb, window, qb):
    H, t = fb.shape[0], MAX_DISTANCE
    nb = qb // t
    assert qb % t == 0 and (nb == 1 or window >= 2 * qb)
    pair = _toeplitz_pair(fb, window)
    blocks = {1: pair[:, :t], 0: pair[:, t:]}
    zero, masked = jnp.zeros((H, t, t), F32), jnp.full((H, t, t), NEG, F32)
    rows = []
    for a in range(2 * nb):
        row = [blocks.get(b + nb - a, zero if b + nb - a > 1 else masked) for b in range(nb)]
        rows.append(jnp.concatenate(row, axis=2))
    return _to_group_lanes(jnp.concatenate(rows, axis=1))


def _masked_rows(n, qb):
    return jnp.full((N_KV, n, HPG * qb), NEG, F32)


def _window_table(fb, qb):
    kk = np.arange(qb)[:, None]
    qi = np.arange(HPG * qb)[None, :] % qb
    edge = np.where(kk > qi, 0.0, NEG).astype(np.float32)
    edge = jnp.broadcast_to(jnp.asarray(edge), (N_KV, qb, HPG * qb))
    mid = jnp.zeros((N_KV, NSA_WINDOW - 2 * qb, HPG * qb), F32)
    return jnp.concatenate([edge, mid, _near_band(fb, NSA_WINDOW, qb), _masked_rows(NSA_WINDOW, qb)], axis=1)


def _cmp_band_rows(qb):
    return CMP_LEAD + qb // CMP_STRIDE


def _cmp_band(fb, nc, qb):
    m = np.arange(_cmp_band_rows(qb))[:, None]
    i = np.arange(qb)[None, :]
    dist = i + CMP_STRIDE * CMP_LEAD - CMP_STRIDE * m - (CMP_LEN - 1)
    band = _to_group_lanes(_bias_of_dist(fb, dist, 1 << 30))
    return jnp.concatenate([jnp.zeros((N_KV, nc, HPG * qb), F32), band, _masked_rows(nc, qb)], axis=1)


def _to_token_rows(oT, qb):
    pair = lambda a: jnp.concatenate([oT[:, a * qb:(a + 1) * qb], oT[:, (a + 1) * qb:(a + 2) * qb]], axis=0)
    return jnp.concatenate([pair(0).T, pair(2).T], axis=1)


def _nsa_kernel(q_ref, gt_ref, kc_ref, vct_ref, ks_ref, vst_ref, kw_ref, vwt_ref, ovl_ref, cb_ref,
                wt_ref, *rest, nc, step0, nblk):
    o_ref, acc_ref, ow_ref, sa_ref, sb_ref, pa_ref, pb_ref = rest[-7:]
    QB, LQ, C = NSA_QB, HPG * NSA_QB, NSA_QPS
    g = pl.program_id(1)
    chains = range(C)
    ncr = kc_ref.shape[2]
    qbs = [(step0 + pl.program_id(2)) * C + c for c in chains]
    s0s = [qb * QB for qb in qbs]
    qTs = [q_ref[0, 0, c] for c in chains]

    def gate(c, branch):
        rows = [gt_ref[0, c, pl.ds(branch * N_HEADS + g * HPG + j, 1), :] for j in range(HPG)]
        return jnp.concatenate(rows, axis=1)

    scs, sws, w0s = [], [], []
    for c in chains:
        cstart = pl.multiple_of(nc + CMP_LEAD - (QB // CMP_STRIDE) * qbs[c], 8)
        scs.append(jnp.dot(kc_ref[0, 0], qTs[c], preferred_element_type=F32)
                   + cb_ref[0, pl.ds(cstart, ncr), :])
    win_keys = NSA_WINDOW + QB
    for c in chains:
        w0 = pl.multiple_of(jnp.maximum(s0s[c] - NSA_WINDOW, 0), QB)
        t0 = pl.multiple_of(w0 - (s0s[c] - NSA_WINDOW), QB)
        w0s.append(w0)
        sws.append(jnp.dot(kw_ref[0, 0, pl.ds(w0, win_keys), :], qTs[c], preferred_element_type=F32)
                   + wt_ref[0, pl.ds(t0, win_keys), :])

    p_cs = []
    for c in chains:
        m_c = jnp.max(scs[c], axis=0, keepdims=True)
        e_c = jnp.exp2(scs[c] - m_c)
        l_c = jnp.sum(e_c, axis=0, keepdims=True)
        p_cs.append(e_c * jnp.where(m_c > 0.1 * NEG, 1.0 / l_c, 0.0))
    o_cmps, imps = [], []
    for c in chains:
        o_cmps.append(jnp.dot(vct_ref[0, 0], p_cs[c].astype(BF16), preferred_element_type=F32)[:HEAD_DIM])
        p_sum = functools.reduce(jnp.add, [p_cs[c][:, j * QB:(j + 1) * QB] for j in range(HPG)])
        imps.append(jnp.dot(ovl_ref[...], p_sum, preferred_element_type=F32,
                            precision=lax.Precision.HIGHEST)[HEAD_DIM:HEAD_DIM + nblk])

    p_ws = []
    for c in chains:
        m_w = jnp.max(sws[c], axis=0, keepdims=True)
        p_ws.append(jnp.exp2(sws[c] - m_w).astype(BF16))
    acc_ws = [jnp.dot(vwt_ref[0, 0, :, pl.ds(w0s[c], win_keys)], p_ws[c], preferred_element_type=F32)
              for c in chains]

    blk = lax.broadcasted_iota(jnp.int32, (nblk, QB), 0)
    sub = lax.broadcasted_iota(jnp.int32, (8, QB), 0)
    unseen = [jnp.full((MASK_LANES - nblk, LQ), NEG, BF16)] if nblk < MASK_LANES else []
    q_nears, q_fars = [], []
    for c in chains:
        tq = s0s[c] + lax.broadcasted_iota(jnp.int32, (nblk, QB), 1)
        cur = tq // SLC_LEN
        forced = (blk == 0) | (blk == cur) | (blk == cur - 1)
        future = blk > cur
        val = jnp.where(forced, 1e9, jnp.where(future, -1e9, imps[c]))
        vals = [val[8 * a:8 * a + 8] for a in range(nblk // 8)]
        cnts = [jnp.zeros((8, QB), F32) for _ in range(nblk // 8)]
        for jp in range(nblk):
            ap, r = divmod(jp, 8)
            rv = jnp.broadcast_to(vals[ap][r:r + 1], (8, QB))
            for a in range(nblk // 8):
                gt = jnp.where(rv > vals[a], 1.0, 0.0)
                ge = jnp.where(rv >= vals[a], 1.0, 0.0)
                if a < ap:
                    inc = gt
                elif a > ap:
                    inc = ge
                else:
                    inc = jnp.where(sub > r, ge, gt)
                cnts[a] = cnts[a] + inc
        cnt = jnp.concatenate(cnts, axis=0)
        allowed = (cnt < float(N_SELECT)) & (blk <= cur)
        near_blk = blk >= (s0s[c] - QB) // SLC_LEN
        neg_near = jnp.where(allowed, 0.0, NEG).astype(BF16)
        neg_far = jnp.where(allowed & jnp.logical_not(near_blk), 0.0, NEG).astype(BF16)
        q_nears.append(jnp.concatenate([qTs[c][:HEAD_DIM], jnp.tile(neg_near, (1, HPG))] + unseen, axis=0))
        q_fars.append(jnp.concatenate([qTs[c][:HEAD_DIM], jnp.tile(neg_far, (1, HPG))] + unseen, axis=0))
    for c in chains:
        ow_ref[c] = gate(c, 2) * (acc_ws[c][:HEAD_DIM] / acc_ws[c][HEAD_DIM:HEAD_DIM + 1])

    last_tile = ks_ref.shape[2] // FAR_TK - 1

    def tile_start(t):
        return pl.multiple_of(jnp.minimum(t, last_tile) * FAR_TK, FAR_TK)

    def qk(c, t):
        return jnp.dot(ks_ref[0, 0, pl.ds(tile_start(t), FAR_TK), :], q_fars[c], preferred_element_type=F32)

    def softmax_update(s, m_run):
        m_new = jnp.maximum(m_run, jnp.max(s, axis=0, keepdims=True))
        return jnp.exp2(s - m_new).astype(BF16), jnp.exp2(m_run - m_new), m_new

    def pv_update(c, t, p, alpha):
        pv = jnp.dot(vst_ref[0, 0, :, pl.ds(tile_start(t), FAR_TK)], p, preferred_element_type=F32)
        acc_ref[c] = acc_ref[c] * alpha + pv

    n0s, s_nears = [], []
    for c in chains:
        n0 = pl.multiple_of(jnp.maximum(s0s[c] - QB, 0), QB)
        b0 = pl.multiple_of(NSA_WINDOW - QB + n0 - (s0s[c] - QB), QB)
        n0s.append(n0)
        s_nears.append(jnp.dot(ks_ref[0, 0, pl.ds(n0, 2 * QB), :], q_nears[c], preferred_element_type=F32)
                       + wt_ref[0, pl.ds(b0, 2 * QB), :])
    for c in chains:
        sa_ref[c] = qk(c, 0)
        sb_ref[c] = qk(c, 1)
    m_nears = [jnp.max(s, axis=0, keepdims=True) for s in s_nears]
    p_nears = [jnp.exp2(s - m).astype(BF16) for s, m in zip(s_nears, m_nears)]
    for c in chains:
        acc_ref[c] = jnp.dot(vst_ref[0, 0, :, pl.ds(n0s[c], 2 * QB)], p_nears[c],
                             preferred_element_type=F32)
    carries = []
    for c in chains:
        p0, alpha0, m0 = softmax_update(sa_ref[c], m_nears[c])
        pa_ref[c] = p0
        carries.append((alpha0, m0))

    for c in chains:
        def far_body(i, carry, c=c):
            alpha_a, m_run = carry
            t = 2 * i
            pv_update(c, t, pa_ref[c], alpha_a)
            p_b, alpha_b, m_run = softmax_update(sb_ref[c], m_run)
            pb_ref[c] = p_b
            sa_ref[c] = qk(c, t + 2)
            pv_update(c, t + 1, pb_ref[c], alpha_b)
            p_a, alpha_a, m_run = softmax_update(sa_ref[c], m_run)
            pa_ref[c] = p_a
            sb_ref[c] = qk(c, t + 3)
            return alpha_a, m_run

        n_far = (jnp.maximum(s0s[c] - QB, 0) + FAR_TK - 1) // FAR_TK
        lax.fori_loop(0, (n_far + 1) // 2, far_body, carries[c])

    for c in chains:
        a_c = acc_ref[c]
        o_sel = a_c[:HEAD_DIM] / a_c[HEAD_DIM:HEAD_DIM + 1]
        o = gate(c, 0) * o_cmps[c] + gate(c, 1) * o_sel + ow_ref[c]
        o_ref[0, c * QB:(c + 1) * QB, :] = _to_token_rows(o, QB).astype(o_ref.dtype)


def _nsa_attention(qT, gatesT, kc_aug, vcT, ks_aug, vsT, kw_aug, vwT, ovl, cband, wtable):
    B, G, nqb = qT.shape[:3]
    QB, LQ, C = NSA_QB, HPG * NSA_QB, NSA_QPS
    nc = kc_aug.shape[2]
    steps = nqb // C
    n_calls = max(1, min(NSA_CALLS, steps // 2))
    per_call = steps // n_calls
    assert steps % n_calls == 0
    per_g = lambda a: pl.BlockSpec((1,) + a.shape[1:], lambda b, g, q: (g,) + (0,) * (a.ndim - 1))
    rows = lambda a, n: pl.BlockSpec((1, 1, n, a.shape[3]), lambda b, g, q: (b, g, 0, 0))
    cols = lambda a, n: pl.BlockSpec((1, 1, a.shape[2], n), lambda b, g, q: (b, g, 0, 0))
    out = None
    for r in range(n_calls):
        step0 = r * per_call
        seen = (r + 1) * per_call * C * QB
        ncr = min(nc, -(-(seen // CMP_STRIDE) // 128) * 128)
        ovl_r = ovl[:, :ncr]
        in_specs = [pl.BlockSpec((1, 1, C, AUG, LQ), lambda b, g, q, s=step0: (b, g, s + q, 0, 0)),
                    pl.BlockSpec((1, C, N_GATE, QB), lambda b, g, q, s=step0: (b, s + q, 0, 0)),
                    rows(kc_aug, ncr), cols(vcT, ncr), rows(ks_aug, seen), cols(vsT, seen),
                    rows(kw_aug, seen), cols(vwT, seen), pl.BlockSpec(ovl_r.shape, lambda b, g, q: (0, 0)),
                    per_g(cband), per_g(wtable)]
        args = [qT, gatesT, kc_aug, vcT, ks_aug, vsT, kw_aug, vwT, ovl_r, cband, wtable]
        aliases = {}
        if out is not None:
            in_specs.append(pl.BlockSpec(memory_space=pl.ANY))
            args.append(out)
            aliases = {len(args) - 1: 0}
        out = pl.pallas_call(
            functools.partial(_nsa_kernel, nc=nc, step0=step0, nblk=seen // SLC_LEN),
            grid=(B, G, per_call),
            in_specs=in_specs,
            out_specs=pl.BlockSpec((1, C * QB, HPG * HEAD_DIM), lambda b, g, q, s=step0: (b, s + q, g)),
            out_shape=jax.ShapeDtypeStruct((B, nqb * QB, Q_WIDTH), BF16),
            scratch_shapes=[pltpu.VMEM((C, V_ROWS, LQ), F32), pltpu.VMEM((C, HEAD_DIM, LQ), F32),
                            pltpu.VMEM((C, FAR_TK, LQ), F32), pltpu.VMEM((C, FAR_TK, LQ), F32),
                            pltpu.VMEM((C, FAR_TK, LQ), BF16), pltpu.VMEM((C, FAR_TK, LQ), BF16)],
            input_output_aliases=aliases,
            compiler_params=_cparams(("arbitrary", "arbitrary", "arbitrary")),
            name=f"nsa_attention_{r}",
        )(*args)
    return out


def _swa_kernel(q_ref, k_ref, vt_ref, band_ref, sink_ref, o_ref):
    QB = SWA_QB
    sink = sink_ref[0, 0:1]
    starts, scores = [], []
    for i in range(SWA_QPS):
        s0 = (pl.program_id(2) * SWA_QPS + i) * QB
        n0 = pl.multiple_of(jnp.maximum(s0 - QB, 0), QB)
        b0 = pl.multiple_of(n0 - (s0 - QB), QB)
        starts.append(n0)
        scores.append(jnp.dot(k_ref[0, 0, pl.ds(n0, 2 * QB), :], q_ref[0, 0, i], preferred_element_type=F32)
                      + band_ref[0, pl.ds(b0, 2 * QB), :])
    maxes = [jnp.maximum(jnp.max(s, axis=0, keepdims=True), sink) for s in scores]
    probs = [jnp.exp2(s - m).astype(BF16) for s, m in zip(scores, maxes)]
    accs = [jnp.dot(vt_ref[0, 0, :, pl.ds(n0, 2 * QB)], p, preferred_element_type=F32)
            for n0, p in zip(starts, probs)]
    for i, (acc, m) in enumerate(zip(accs, maxes)):
        denom = acc[HEAD_DIM:HEAD_DIM + 1] + jnp.exp2(sink - m)
        o_ref[0, i * QB:(i + 1) * QB, :] = _to_token_rows(acc[:HEAD_DIM] / denom, QB).astype(o_ref.dtype)


def _swa_attention(qT, k_aug, vT, band, sink):
    B, G, nqb = qT.shape[:3]
    QB, LQ = SWA_QB, HPG * SWA_QB
    per_bg = lambda a: pl.BlockSpec((1, 1) + a.shape[2:], lambda b, g, q: (b, g) + (0,) * (a.ndim - 2))
    per_g = lambda a: pl.BlockSpec((1,) + a.shape[1:], lambda b, g, q: (g,) + (0,) * (a.ndim - 1))
    return pl.pallas_call(
        _swa_kernel,
        grid=(B, G, nqb // SWA_QPS),
        in_specs=[pl.BlockSpec((1, 1, SWA_QPS, AUG, LQ), lambda b, g, q: (b, g, q, 0, 0)),
                  per_bg(k_aug), per_bg(vT), per_g(band), per_g(sink)],
        out_specs=pl.BlockSpec((1, SWA_QPS * QB, HPG * HEAD_DIM), lambda b, g, q: (b, q, g)),
        out_shape=jax.ShapeDtypeStruct((B, nqb * QB, Q_WIDTH), BF16),
        compiler_params=_cparams(("arbitrary", "arbitrary", "arbitrary")),
        name="swa_attention",
    )(qT, k_aug, vT, band, sink)


def _layer_norm(y, g, b):
    mu = jnp.mean(y, axis=-1, keepdims=True)
    yc = y - mu
    var = jnp.mean(yc * yc, axis=-1, keepdims=True)
    return yc * lax.rsqrt(var + LN_EPS) * g + b


def _top2_of4(a, b, c, d):
    hi1, lo1 = jnp.maximum(a, b), jnp.minimum(a, b)
    hi2, lo2 = jnp.maximum(c, d), jnp.minimum(c, d)
    return jnp.maximum(hi1, hi2) + jnp.maximum(jnp.minimum(hi1, hi2), jnp.maximum(lo1, lo2))


def _route(s, sb):
    n = EXPERTS_PER_GROUP
    score = [_top2_of4(*sb[n * r:n * r + n]) for r in range(N_GROUPS)]
    best = functools.reduce(jnp.maximum, score)
    taken = jnp.zeros_like(best) > 1.0
    in_grp = []
    for r in range(N_GROUPS):
        pick = (score[r] == best) & jnp.logical_not(taken)
        in_grp.append(pick)
        taken = taken | pick
    gates = []
    for e in range(N_EXPERTS):
        r = e // n
        ahead = jnp.zeros_like(best)
        for f in range(n * r, n * r + n):
            if f != e:
                beats = (sb[f] >= sb[e]) if f < e else (sb[f] > sb[e])
                ahead = ahead + jnp.where(beats, 1.0, 0.0)
        gates.append(jnp.where(in_grp[r] & (ahead < 2.0), s[e], 0.0))
    total = functools.reduce(jnp.add, gates)
    return [g / total for g in gates]


def _oproj_kernel(o_ref, x_ref, w_ref, ga_ref, lg_ref, lb_ref, shf_ref, scf_ref, rw_ref, rb_ref,
                  x1_ref, h_ref, gate_ref):
    mix = jnp.dot(o_ref[0], w_ref[...], preferred_element_type=F32)
    x1 = _layer_norm(ALPHA * x_ref[0] + (1.0 + ga_ref[0]) * mix, lg_ref[...], lb_ref[...])
    x1_ref[0] = x1
    h = x1 * (1.0 + scf_ref[0]) + shf_ref[0]
    hb = h.astype(BF16)
    h_ref[0] = hb
    logit = lax.dot_general(rw_ref[...], hb, _NT, preferred_element_type=F32)
    aff = jax.nn.sigmoid(logit)
    biased = aff + rb_ref[...][:, 0:1]
    s = [aff[e:e + 1] for e in range(N_EXPERTS)]
    sb = [biased[e:e + 1] for e in range(N_EXPERTS)]
    gate_ref[0] = jnp.concatenate(_route(s, sb), axis=0)


def _oproj(o, x, w_out, g_a, ln_g, ln_b, sh_f, sc_f, router_w, router_b, ts=512):
    B, S, D = x.shape
    row = lambda a: a.reshape(1, D)
    per_b = pl.BlockSpec((1, 1, D), lambda b, s: (b, 0, 0))
    full = lambda a: pl.BlockSpec(a.shape, lambda b, s: (0,) * a.ndim)
    rwT = router_w.T.astype(BF16)
    rb = jnp.broadcast_to(router_b.reshape(N_EXPERTS, 1), (N_EXPERTS, 128))
    args = (o, x, w_out.astype(BF16), g_a.reshape(B, 1, D), row(ln_g), row(ln_b),
            sh_f.reshape(B, 1, D), sc_f.reshape(B, 1, D), rwT, rb)
    tile = pl.BlockSpec((1, ts, D), lambda b, s: (b, s, 0))
    return pl.pallas_call(
        _oproj_kernel,
        grid=(B, S // ts),
        in_specs=[tile, tile, full(args[2]), per_b, full(args[4]), full(args[5]), per_b, per_b,
                  full(rwT), full(rb)],
        out_specs=[tile, tile, pl.BlockSpec((1, N_EXPERTS, ts), lambda b, s: (b, 0, s))],
        out_shape=[jax.ShapeDtypeStruct((B, S, D), F32), jax.ShapeDtypeStruct((B, S, D), BF16),
                   jax.ShapeDtypeStruct((B, N_EXPERTS, S), F32)],
        compiler_params=_cparams(("arbitrary", "arbitrary")),
        name="oproj_ln_router",
    )(*args)


def _moe_kernel(h_ref, gate_ref, wgu_ref, wd_ref, x_ref, gf_ref, lg_ref, lb_ref, o_ref, acc_ref):
    r = pl.program_id(1)
    n, width = EXPERTS_PER_GROUP, EXPERTS_PER_GROUP * D_EXPERT

    @pl.when(r == 0)
    def _():
        acc_ref[...] = jnp.zeros_like(acc_ref)

    gu = jnp.dot(h_ref[...], wgu_ref[0], preferred_element_type=F32)
    a, u = gu[:, :width], gu[:, width:]
    act = a * jax.nn.sigmoid(a) * u
    gates = gate_ref[...]
    lane = lax.broadcasted_iota(jnp.int32, gates.shape, 1)
    he = []
    for k in range(n):
        gcol = jnp.sum(jnp.where(lane == r * n + k, gates, 0.0), axis=1, keepdims=True)
        he.append((act[:, k * D_EXPERT:(k + 1) * D_EXPERT] * gcol).astype(BF16))
    acc_ref[...] += jnp.dot(jnp.concatenate(he, axis=1), wd_ref[0], preferred_element_type=F32)

    @pl.when(r == N_GROUPS - 1)
    def _():
        y = ALPHA * x_ref[...] + (1.0 + gf_ref[0]) * acc_ref[...]
        o_ref[...] = _layer_norm(y, lg_ref[...], lb_ref[...])


def _group_fused(w):
    E, D, F = w.shape
    w = w.reshape(N_GROUPS, EXPERTS_PER_GROUP, D, F).transpose(0, 2, 1, 3)
    return w.reshape(N_GROUPS, D, EXPERTS_PER_GROUP * F)


def _moe(h, gates, w_gate, w_up, w_down, x, g_f, ln_g, ln_b, ts=512):
    B, S, D = x.shape
    T = B * S
    width = EXPERTS_PER_GROUP * D_EXPERT
    wgu = jnp.concatenate([_group_fused(w_gate), _group_fused(w_up)], axis=-1).astype(BF16)
    wd = w_down.reshape(N_GROUPS, width, D).astype(BF16)
    gl = jnp.pad(gates.transpose(0, 2, 1).reshape(T, N_EXPERTS), ((0, 0), (0, 128 - N_EXPERTS)))
    tile = pl.BlockSpec((ts, D), lambda t, e: (t, 0))
    spt = S // ts
    out = pl.pallas_call(
        _moe_kernel,
        grid=(T // ts, N_GROUPS),
        in_specs=[tile, pl.BlockSpec((ts, 128), lambda t, e: (t, 0)),
                  pl.BlockSpec((1, D, 2 * width), lambda t, e: (e, 0, 0)),
                  pl.BlockSpec((1, width, D), lambda t, e: (e, 0, 0)),
                  tile, pl.BlockSpec((1, 1, D), lambda t, e: (t // spt, 0, 0)),
                  pl.BlockSpec((1, D), lambda t, e: (0, 0)), pl.BlockSpec((1, D), lambda t, e: (0, 0))],
        out_specs=tile,
        out_shape=jax.ShapeDtypeStruct((T, D), F32),
        scratch_shapes=[pltpu.VMEM((ts, D), F32)],
        compiler_params=_cparams(("arbitrary", "arbitrary")),
        name="moe_ln",
    )(h.reshape(T, D), gl, wgu, wd, x.reshape(T, D), g_f.reshape(B, 1, D), ln_g.reshape(1, D),
      ln_b.reshape(1, D))
    return out.reshape(B, S, D)


def _values_t(v):
    B, G, n, _ = v.shape
    return jnp.concatenate([v.transpose(0, 1, 3, 2), jnp.ones((B, G, V_ROWS - HEAD_DIM, n), v.dtype)], axis=2)


def _nsa_layer_attention(x, sh, sc, w_in, k_w1, k_w2, v_w1, v_w2, k_pos, v_pos, fb):
    B, S, _ = x.shape
    nc, n_slc = S // CMP_STRIDE, S // SLC_LEN
    qT, gatesT, ks_aug, vsT, kw_aug, vwT, ck, cv = _nsa_proj(x, sh, sc, w_in)
    merge = lambda t: t.reshape(B, N_KV * nc, CMP_STRIDE * HEAD_DIM)
    k_cmp = _compress(merge(ck), k_w1, k_w2, k_pos)
    v_cmp = _compress(merge(cv), v_w1, v_w2, v_pos)
    kc_aug = jnp.pad(k_cmp.astype(BF16), ((0, 0), (0, 0), (0, 0), (0, MASK_LANES)))
    vcT = _values_t(v_cmp.astype(BF16))

    c_start = np.arange(nc)[None, :] * CMP_STRIDE
    s_start = np.arange(MASK_LANES)[:, None] * SLC_LEN
    ovl = ((c_start < s_start + SLC_LEN) & (c_start + CMP_LEN > s_start)
           & (np.arange(nc)[None, :] < nc - 1) & (np.arange(MASK_LANES)[:, None] < n_slc))
    ovl = np.concatenate([np.zeros((HEAD_DIM, nc)), ovl.astype(np.float64)], axis=0).astype(np.float32)

    return _nsa_attention(qT, gatesT, kc_aug, vcT, ks_aug, vsT, kw_aug, vwT, jnp.asarray(ovl),
                          _cmp_band(fb, nc, NSA_QB), _window_table(fb, NSA_QB))


def _swa_layer_attention(x, sh, sc, w_q, w_kv, sinks, rel_bias, fb):
    qT, k_aug, vT = _swa_proj(x, sh, sc, w_q, w_kv)
    sink = (sinks.astype(F32) - rel_bias.astype(F32)[N_BUCKETS - 1]) * LOG2E
    sink = jnp.broadcast_to(sink.reshape(N_KV, 1, HPG, 1), (N_KV, 8, HPG, SWA_QB))
    sink = sink.reshape(N_KV, 8, HPG * SWA_QB)
    band = jnp.concatenate([_near_band(fb, SWA_WINDOW, SWA_QB), _masked_rows(SWA_QB, SWA_QB)], axis=1)
    return _swa_attention(qT, k_aug, vT, band, sink)


def kernel(x, c, nsa_w_in, cmp_k_w1, cmp_k_w2, cmp_v_w1, cmp_v_w2, cmp_k_pos, cmp_v_pos, nsa_w_out,
           swa_w_q, swa_sinks, swa_w_out, shared_w_kv, rel_bias, router_w, router_b, moe_w_gate,
           moe_w_up, moe_w_down, ada_w, ada_b, ln_g, ln_b):
    B, S, D = x.shape
    ada = _ada(c, ada_w, ada_b)
    fb = _shifted_bias(rel_bias)
    for layer in range(DEPTH):
        sh_a, sc_a, g_a, sh_f, sc_f, g_f = [ada[layer, :, i * D:(i + 1) * D] for i in range(6)]
        if layer == 0:
            o = _nsa_layer_attention(x, sh_a, sc_a, nsa_w_in[0], cmp_k_w1[0], cmp_k_w2[0], cmp_v_w1[0],
                                     cmp_v_w2[0], cmp_k_pos[0], cmp_v_pos[0], fb)
            w_out = nsa_w_out[0]
        else:
            o = _swa_layer_attention(x, sh_a, sc_a, swa_w_q[0], shared_w_kv, swa_sinks[0], rel_bias, fb)
            w_out = swa_w_out[0]
        x1, h, gates = _oproj(o, x, w_out, g_a, ln_g[layer, 0], ln_b[layer, 0], sh_f, sc_f,
                              router_w, router_b)
        x = _moe(h, gates, moe_w_gate[layer], moe_w_up[layer], moe_w_down[layer], x1, g_f,
                 ln_g[layer, 1], ln_b[layer, 1])
    return x
```

```python
import functools
import math

import numpy as np
import jax
import jax.numpy as jnp
from jax import lax
from jax.experimental import pallas as pl
from jax.experimental.pallas import tpu as pltpu

F32 = jnp.float32
BF16 = jnp.bfloat16

D_MODEL = 1024
HEAD_DIM = 64
N_HEADS = 16
N_KV = 4
HPG = 4
Q_WIDTH = N_HEADS * HEAD_DIM
KV_WIDTH = N_KV * HEAD_DIM
CMP_LEN = 32
CMP_STRIDE = 16
CMP_HIDDEN = 256
SLC_LEN = 64
N_SELECT = 16
NSA_WINDOW = 512
SWA_WINDOW = 128
N_BUCKETS = 32
MAX_DISTANCE = 128
N_EXPERTS = 16
N_GROUPS = 4
EXPERTS_PER_GROUP = 4
D_EXPERT = 256
DEPTH = 2
ALPHA = (2.0 * DEPTH) ** 0.25
LN_EPS = 1e-5
NEG = -1e30
ATTN_SCALE = HEAD_DIM ** -0.5
LOG2E = math.log2(math.e)
Q_SCALE = ATTN_SCALE * LOG2E

NSA_QB = 256
NSA_QPS = 2
SWA_QB = 128
SWA_QPS = 8
AUG = 128
MASK_LANES = AUG - HEAD_DIM
V_ROWS = HEAD_DIM + 16
CMP_LEAD = 16
FAR_TK = 256
N_GATE = 3 * N_HEADS
VMEM_LIMIT = 56 * 1024 * 1024

_NT = (((1,), (1,)), ((), ()))


def _cparams(sem):
    return pltpu.CompilerParams(dimension_semantics=sem, vmem_limit_bytes=VMEM_LIMIT)


def _ada_kernel(c_ref, w_ref, b_ref, o_ref):
    c = c_ref[...]
    cond = c * jax.nn.sigmoid(c)
    o_ref[0] = jnp.dot(cond, w_ref[0], preferred_element_type=F32,
                       precision=lax.Precision.HIGHEST) + b_ref[0]


def _ada(c, ada_w, ada_b):
    B, D = c.shape
    n6 = ada_w.shape[-1]
    tn = 1536
    return pl.pallas_call(
        _ada_kernel,
        grid=(DEPTH, n6 // tn),
        in_specs=[pl.BlockSpec((B, D), lambda l, n: (0, 0)),
                  pl.BlockSpec((1, D, tn), lambda l, n: (l, 0, n)),
                  pl.BlockSpec((1, 1, tn), lambda l, n: (l, 0, n))],
        out_specs=pl.BlockSpec((1, B, tn), lambda l, n: (l, 0, n)),
        out_shape=jax.ShapeDtypeStruct((DEPTH, B, n6), F32),
        compiler_params=_cparams(("arbitrary", "arbitrary")),
        name="ada",
    )(c, ada_w, ada_b.reshape(DEPTH, 1, n6))


def _store_q_blocks(q_ref, qT, ts, qb):
    zeros = jnp.zeros((MASK_LANES, HPG * qb), BF16)
    for g in range(N_KV):
        for k in range(ts // qb):
            for j in range(HPG):
                r0 = (g * HPG + j) * HEAD_DIM
                q_ref[0, g, k, 0:HEAD_DIM, j * qb:(j + 1) * qb] = (
                    qT[r0:r0 + HEAD_DIM, k * qb:(k + 1) * qb].astype(BF16))
            q_ref[0, g, k, HEAD_DIM:AUG, :] = zeros


def _store_vt(vt_ref, vT, ts):
    ones = jnp.ones((V_ROWS - HEAD_DIM, ts), BF16)
    for g in range(N_KV):
        vt_ref[0, g, 0:HEAD_DIM, :] = vT[g * HEAD_DIM:(g + 1) * HEAD_DIM].astype(BF16)
        vt_ref[0, g, HEAD_DIM:V_ROWS, :] = ones


def _nsa_proj_kernel(x_ref, sh_ref, sc_ref, wt_ref, ws_ref, q_ref, gt_ref, ks_ref, vst_ref, kw_ref,
                     vwt_ref, ck_ref, cv_ref, cmp_scr, *, ts):
    s = pl.program_id(1)
    h = (x_ref[0] * (1.0 + sc_ref[0]) + sh_ref[0]).astype(BF16)
    yT = lax.dot_general(wt_ref[...], h, _NT, preferred_element_type=F32)
    y = jnp.dot(h, ws_ref[...], preferred_element_type=F32)
    _store_q_blocks(q_ref, yT[:Q_WIDTH], ts, NSA_QB)
    _store_vt(vst_ref, yT[Q_WIDTH:Q_WIDTH + KV_WIDTH], ts)
    _store_vt(vwt_ref, yT[Q_WIDTH + KV_WIDTH:Q_WIDTH + 2 * KV_WIDTH], ts)
    gates = jax.nn.sigmoid(yT[Q_WIDTH + 2 * KV_WIDTH:])
    for k in range(ts // NSA_QB):
        gt_ref[0, k] = gates[:, k * NSA_QB:(k + 1) * NSA_QB]
    lane = lax.broadcasted_iota(jnp.int32, (ts, AUG), 1)
    tok = s * ts + lax.broadcasted_iota(jnp.int32, (ts, AUG), 0)
    onehot = (lane - HEAD_DIM) == tok // SLC_LEN
    for g in range(N_KV):
        ks_ref[0, g] = jnp.where(onehot, 1.0, y[:, g * AUG:(g + 1) * AUG]).astype(BF16)
        kw_ref[0, g] = y[:, (N_KV + g) * AUG:(N_KV + g + 1) * AUG].astype(BF16)
    base = 2 * N_KV * AUG
    for c in range(2 * KV_WIDTH // 128):
        cmp_scr[c] = y[:, base + c * 128:base + (c + 1) * 128]
    for c, dst in enumerate([ck_ref, ck_ref, cv_ref, cv_ref]):
        for r in range(CMP_STRIDE):
            piece = cmp_scr[c, pl.ds(r, ts // CMP_STRIDE, stride=CMP_STRIDE), :]
            for half in range(2):
                g = 2 * (c % 2) + half
                dst[0, g, :, r * HEAD_DIM:(r + 1) * HEAD_DIM] = (
                    piece[:, half * HEAD_DIM:(half + 1) * HEAD_DIM].astype(BF16))


def _pad_heads(w):
    D = w.shape[0]
    w = w.reshape(D, N_KV, HEAD_DIM)
    return jnp.pad(w, ((0, 0), (0, 0), (0, MASK_LANES))).reshape(D, N_KV * AUG)


def _nsa_proj(x, shift, scale, w_in, ts=512):
    B, S, D = x.shape
    nqb, lq = S // NSA_QB, HPG * NSA_QB
    cols = lambda i: w_in[:, Q_WIDTH + i * KV_WIDTH:Q_WIDTH + (i + 1) * KV_WIDTH]
    w_kc, w_vc, w_ksl, w_vsl, w_kw, w_vw = [cols(i) for i in range(6)]
    wt = jnp.concatenate([w_in[:, :Q_WIDTH] * Q_SCALE, w_vsl, w_vw,
                          w_in[:, Q_WIDTH + 6 * KV_WIDTH:]], axis=1).T.astype(BF16)
    ws = jnp.concatenate([_pad_heads(w_ksl), _pad_heads(w_kw), w_kc, w_vc], axis=1).astype(BF16)
    per_b = pl.BlockSpec((1, 1, D), lambda b, s: (b, 0, 0))
    kspec = pl.BlockSpec((1, N_KV, ts, AUG), lambda b, s: (b, 0, s, 0))
    vspec = pl.BlockSpec((1, N_KV, V_ROWS, ts), lambda b, s: (b, 0, 0, s))
    k_shape = jax.ShapeDtypeStruct((B, N_KV, S, AUG), BF16)
    v_shape = jax.ShapeDtypeStruct((B, N_KV, V_ROWS, S), BF16)
    chunk = CMP_STRIDE * HEAD_DIM
    cspec = pl.BlockSpec((1, N_KV, ts // CMP_STRIDE, chunk), lambda b, s: (b, 0, s, 0))
    c_shape = jax.ShapeDtypeStruct((B, N_KV, S // CMP_STRIDE, chunk), BF16)
    return pl.pallas_call(
        functools.partial(_nsa_proj_kernel, ts=ts),
        grid=(B, S // ts),
        in_specs=[pl.BlockSpec((1, ts, D), lambda b, s: (b, s, 0)), per_b, per_b,
                  pl.BlockSpec(wt.shape, lambda b, s: (0, 0)), pl.BlockSpec(ws.shape, lambda b, s: (0, 0))],
        out_specs=[pl.BlockSpec((1, N_KV, ts // NSA_QB, AUG, lq), lambda b, s: (b, 0, s, 0, 0)),
                   pl.BlockSpec((1, ts // NSA_QB, N_GATE, NSA_QB), lambda b, s: (b, s, 0, 0)),
                   kspec, vspec, kspec, vspec,
                   cspec, cspec],
        out_shape=[jax.ShapeDtypeStruct((B, N_KV, nqb, AUG, lq), BF16),
                   jax.ShapeDtypeStruct((B, nqb, N_GATE, NSA_QB), F32),
                   k_shape, v_shape, k_shape, v_shape,
                   c_shape, c_shape],
        scratch_shapes=[pltpu.VMEM((2 * KV_WIDTH // 128, ts, 128), F32)],
        compiler_params=_cparams(("arbitrary", "arbitrary")),
        name="nsa_proj",
    )(x, shift.reshape(B, 1, D), scale.reshape(B, 1, D), wt, ws)


def _swa_proj_kernel(x_ref, sh_ref, sc_ref, wq_ref, wk_ref, wv_ref, q_ref, k_ref, vt_ref, *, ts):
    x = x_ref[0]
    h = (x * (1.0 + sc_ref[0]) + sh_ref[0]).astype(BF16)
    xb = x.astype(BF16)
    _store_q_blocks(q_ref, lax.dot_general(wq_ref[...], h, _NT, preferred_element_type=F32), ts, SWA_QB)
    _store_vt(vt_ref, lax.dot_general(wv_ref[...], xb, _NT, preferred_element_type=F32), ts)
    k = jnp.dot(xb, wk_ref[...], preferred_element_type=F32)
    for g in range(N_KV):
        k_ref[0, g] = k[:, g * AUG:(g + 1) * AUG].astype(BF16)


def _swa_proj(x, shift, scale, w_q, w_kv, ts=512):
    B, S, D = x.shape
    lq = HPG * SWA_QB
    wq = (w_q * Q_SCALE).T.astype(BF16)
    wk = _pad_heads(w_kv[:, :KV_WIDTH]).astype(BF16)
    wv = w_kv[:, KV_WIDTH:].T.astype(BF16)
    per_b = pl.BlockSpec((1, 1, D), lambda b, s: (b, 0, 0))
    full = lambda a: pl.BlockSpec(a.shape, lambda b, s: (0, 0))
    return pl.pallas_call(
        functools.partial(_swa_proj_kernel, ts=ts),
        grid=(B, S // ts),
        in_specs=[pl.BlockSpec((1, ts, D), lambda b, s: (b, s, 0)), per_b, per_b, full(wq), full(wk), full(wv)],
        out_specs=[pl.BlockSpec((1, N_KV, ts // SWA_QB, AUG, lq), lambda b, s: (b, 0, s, 0, 0)),
                   pl.BlockSpec((1, N_KV, ts, AUG), lambda b, s: (b, 0, s, 0)),
                   pl.BlockSpec((1, N_KV, V_ROWS, ts), lambda b, s: (b, 0, 0, s))],
        out_shape=[jax.ShapeDtypeStruct((B, N_KV, S // SWA_QB, AUG, lq), BF16),
                   jax.ShapeDtypeStruct((B, N_KV, S, AUG), BF16),
                   jax.ShapeDtypeStruct((B, N_KV, V_ROWS, S), BF16)],
        compiler_params=_cparams(("arbitrary", "arbitrary")),
        name="swa_proj",
    )(x, shift.reshape(B, 1, D), scale.reshape(B, 1, D), wq, wk, wv)


def _gelu_tanh(x):
    return 0.5 * x * (1.0 + jnp.tanh(math.sqrt(2.0 / math.pi) * (x + 0.044715 * (x * x * x))))


def _compress_kernel(ch_ref, w1_ref, w2_ref, pos_ref, o_ref, *, nc):
    half = CMP_STRIDE * HEAD_DIM
    ch = ch_ref[0]
    w1 = w1_ref[...]
    top = jnp.dot(ch, w1[:half], preferred_element_type=F32)
    bot = jnp.dot(ch, w1[half:], preferred_element_type=F32)
    posw = jnp.dot(pos_ref[...], w1, preferred_element_type=F32)[0:1]
    w2 = w2_ref[...]
    for g in range(N_KV):
        bot_g = pltpu.roll(bot[g * nc:(g + 1) * nc], nc - 1, 0)
        pre = top[g * nc:(g + 1) * nc] + bot_g + posw
        o_ref[0, g] = jnp.dot(_gelu_tanh(pre).astype(BF16), w2, preferred_element_type=F32)


def _compress(chunks, w1, w2, pos):
    B, gn, ck = chunks.shape
    nc = gn // N_KV
    pos8 = jnp.broadcast_to(pos.reshape(1, CMP_LEN * HEAD_DIM), (8, CMP_LEN * HEAD_DIM)).astype(BF16)
    return pl.pallas_call(
        functools.partial(_compress_kernel, nc=nc),
        grid=(B,),
        in_specs=[pl.BlockSpec((1, gn, ck), lambda b: (b, 0, 0)),
                  pl.BlockSpec(w1.shape, lambda b: (0, 0)),
                  pl.BlockSpec(w2.shape, lambda b: (0, 0)),
                  pl.BlockSpec(pos8.shape, lambda b: (0, 0))],
        out_specs=pl.BlockSpec((1, N_KV, nc, HEAD_DIM), lambda b: (b, 0, 0, 0)),
        out_shape=jax.ShapeDtypeStruct((B, N_KV, nc, HEAD_DIM), F32),
        compiler_params=_cparams(("arbitrary",)),
        name="compress",
    )(chunks, w1.astype(BF16), w2.astype(BF16), pos8)


def _bucket_np(d):
    d = np.maximum(np.asarray(d, np.int64), 0)
    max_exact = N_BUCKETS // 2
    large = max_exact + (np.log(np.maximum(d, 1).astype(np.float32) / np.float32(max_exact))
                         / np.float32(math.log(MAX_DISTANCE / max_exact))
                         * np.float32(N_BUCKETS - max_exact)).astype(np.int32)
    large = np.minimum(large, N_BUCKETS - 1)
    return np.where(d < max_exact, d, large).astype(np.int32)


def _shifted_bias(rel_table):
    t = rel_table.astype(F32)
    return ((t[_bucket_np(np.arange(MAX_DISTANCE))] - t[N_BUCKETS - 1][None, :]) * LOG2E).T


def _to_group_lanes(m):
    H, K, Q = m.shape
    return m.reshape(N_KV, HPG, K, Q).transpose(0, 2, 1, 3).reshape(N_KV, K, HPG * Q)


def _bias_of_dist(fb, dist, window):
    ok = (dist >= 0) & (dist < window)
    idx = np.clip(dist, 0, MAX_DISTANCE - 1)
    return jnp.where(ok[None], jnp.where((dist >= MAX_DISTANCE)[None], 0.0, fb[:, idx]), NEG)


def _toeplitz_pair(fb, window):
    H, t = fb.shape[0], MAX_DISTANCE
    nv = 3 * t - 1
    v = _bias_of_dist(fb, np.arange(nv) - (t - 1), window)
    rows = 2 * t
    flat = jnp.tile(v, (1, rows + 1))[:, :rows * (nv + 1)]
    return flat.reshape(H, rows, nv + 1)[:, ::-1, :t]


def _near_band(fb, window, qb):
    H, t = fb.shape[0], MAX_DISTANCE
    nb = qb // t
    assert qb % t == 0 and (nb == 1 or window >= 2 * qb)
    pair = _toeplitz_pair(fb, window)
    blocks = {1: pair[:, :t], 0: pair[:, t:]}
    zero, masked = jnp.zeros((H, t, t), F32), jnp.full((H, t, t), NEG, F32)
    rows = []
    for a in range(2 * nb):
        row = [blocks.get(b + nb - a, zero if b + nb - a > 1 else masked) for b in range(nb)]
        rows.append(jnp.concatenate(row, axis=2))
    return _to_group_lanes(jnp.concatenate(rows, axis=1))


def _masked_rows(n, qb):
    return jnp.full((N_KV, n, HPG * qb), NEG, F32)


def _window_table(fb, qb):
    kk = np.arange(qb)[:, None]
    qi = np.arange(HPG * qb)[None, :] % qb
    edge = np.where(kk > qi, 0.0, NEG).astype(np.float32)
    edge = jnp.broadcast_to(jnp.asarray(edge), (N_KV, qb, HPG * qb))
    mid = jnp.zeros((N_KV, NSA_WINDOW - 2 * qb, HPG * qb), F32)
    return jnp.concatenate([edge, mid, _near_band(fb, NSA_WINDOW, qb), _masked_rows(NSA_WINDOW, qb)], axis=1)


def _cmp_band_rows(qb):
    return CMP_LEAD + qb // CMP_STRIDE


def _cmp_band(fb, nc, qb):
    m = np.arange(_cmp_band_rows(qb))[:, None]
    i = np.arange(qb)[None, :]
    dist = i + CMP_STRIDE * CMP_LEAD - CMP_STRIDE * m - (CMP_LEN - 1)
    band = _to_group_lanes(_bias_of_dist(fb, dist, 1 << 30))
    return jnp.concatenate([jnp.zeros((N_KV, nc, HPG * qb), F32), band, _masked_rows(nc, qb)], axis=1)


def _to_token_rows(oT, qb):
    pair = lambda a: jnp.concatenate([oT[:, a * qb:(a + 1) * qb], oT[:, (a + 1) * qb:(a + 2) * qb]], axis=0)
    return jnp.concatenate([pair(0).T, pair(2).T], axis=1)


def _nsa_kernel(q_ref, gt_ref, kc_ref, vct_ref, ks_ref, vst_ref, kw_ref, vwt_ref, ovl_ref, cb_ref,
                wt_ref, *rest, nc, step):
    o_ref, acc_ref, ow_ref, sa_ref, sb_ref, pa_ref, pb_ref = rest[-7:]
    QB, LQ, C = NSA_QB, HPG * NSA_QB, NSA_QPS
    g = pl.program_id(0)
    chains = range(C)
    ncr = kc_ref.shape[2]
    qbs = [step * C + c for c in chains]
    s0s = [qb * QB for qb in qbs]
    nblk = (step + 1) * C * QB // SLC_LEN
    qTs = [q_ref[0, 0, c] for c in chains]

    def gate(c, branch):
        rows = [gt_ref[0, c, pl.ds(branch * N_HEADS + g * HPG + j, 1), :] for j in range(HPG)]
        return jnp.concatenate(rows, axis=1)

    scs, sws, w0s = [], [], []
    for c in chains:
        cstart = nc + CMP_LEAD - (QB // CMP_STRIDE) * qbs[c]
        scs.append(jnp.dot(kc_ref[0, 0], qTs[c], preferred_element_type=F32)
                   + cb_ref[0, pl.ds(cstart, ncr), :])
    win_keys = NSA_WINDOW + QB
    for c in chains:
        w0 = max(s0s[c] - NSA_WINDOW, 0)
        t0 = w0 - (s0s[c] - NSA_WINDOW)
        w0s.append(w0)
        sws.append(jnp.dot(kw_ref[0, 0, pl.ds(w0, win_keys), :], qTs[c], preferred_element_type=F32)
                   + wt_ref[0, pl.ds(t0, win_keys), :])

    p_cs = []
    for c in chains:
        m_c = jnp.max(scs[c], axis=0, keepdims=True)
        e_c = jnp.exp2(scs[c] - m_c)
        l_c = jnp.sum(e_c, axis=0, keepdims=True)
        p_cs.append(e_c * jnp.where(m_c > 0.1 * NEG, 1.0 / l_c, 0.0))
    o_cmps, imps = [], []
    for c in chains:
        o_cmps.append(jnp.dot(vct_ref[0, 0], p_cs[c].astype(BF16), preferred_element_type=F32)[:HEAD_DIM])
        p_sum = functools.reduce(jnp.add, [p_cs[c][:, j * QB:(j + 1) * QB] for j in range(HPG)])
        imps.append(jnp.dot(ovl_ref[...], p_sum, preferred_element_type=F32,
                            precision=lax.Precision.HIGHEST)[HEAD_DIM:HEAD_DIM + nblk])

    p_ws = []
    for c in chains:
        m_w = jnp.max(sws[c], axis=0, keepdims=True)
        p_ws.append(jnp.exp2(sws[c] - m_w).astype(BF16))
    acc_ws = [jnp.dot(vwt_ref[0, 0, :, pl.ds(w0s[c], win_keys)], p_ws[c], preferred_element_type=F32)
              for c in chains]

    blk = lax.broadcasted_iota(jnp.int32, (nblk, QB), 0)
    sub = lax.broadcasted_iota(jnp.int32, (8, QB), 0)
    unseen = [jnp.full((MASK_LANES - nblk, LQ), NEG, BF16)] if nblk < MASK_LANES else []
    q_nears, q_fars = [], []
    for c in chains:
        tq = s0s[c] + lax.broadcasted_iota(jnp.int32, (nblk, QB), 1)
        cur = tq // SLC_LEN
        forced = (blk == 0) | (blk == cur) | (blk == cur - 1)
        future = blk > cur
        val = jnp.where(forced, 1e9, jnp.where(future, -1e9, imps[c]))
        vals = [val[8 * a:8 * a + 8] for a in range(nblk // 8)]
        cnts = [jnp.zeros((8, QB), F32) for _ in range(nblk // 8)]
        for jp in range(nblk):
            ap, r = divmod(jp, 8)
            rv = jnp.broadcast_to(vals[ap][r:r + 1], (8, QB))
            for a in range(nblk // 8):
                gt = jnp.where(rv > vals[a], 1.0, 0.0)
                ge = jnp.where(rv >= vals[a], 1.0, 0.0)
                if a < ap:
                    inc = gt
                elif a > ap:
                    inc = ge
                else:
                    inc = jnp.where(sub > r, ge, gt)
                cnts[a] = cnts[a] + inc
        cnt = jnp.concatenate(cnts, axis=0)
        allowed = (cnt < float(N_SELECT)) & (blk <= cur)
        near_blk = blk >= (s0s[c] - QB) // SLC_LEN
        neg_near = jnp.where(allowed, 0.0, NEG).astype(BF16)
        neg_far = jnp.where(allowed & jnp.logical_not(near_blk), 0.0, NEG).astype(BF16)
        q_nears.append(jnp.concatenate([qTs[c][:HEAD_DIM], jnp.tile(neg_near, (1, HPG))] + unseen, axis=0))
        q_fars.append(jnp.concatenate([qTs[c][:HEAD_DIM], jnp.tile(neg_far, (1, HPG))] + unseen, axis=0))
    for c in chains:
        ow_ref[c] = gate(c, 2) * (acc_ws[c][:HEAD_DIM] / acc_ws[c][HEAD_DIM:HEAD_DIM + 1])

    s_bufs, p_bufs = (sa_ref, sb_ref), (pa_ref, pb_ref)
    n_far = [max(s0s[c] - QB, 0) // FAR_TK for c in chains]

    def qk(c, t):
        return jnp.dot(ks_ref[0, 0, pl.ds(t * FAR_TK, FAR_TK), :], q_fars[c], preferred_element_type=F32)

    def softmax_update(s, m_run):
        m_new = jnp.maximum(m_run, jnp.max(s, axis=0, keepdims=True))
        return jnp.exp2(s - m_new).astype(BF16), jnp.exp2(m_run - m_new), m_new

    n0s, s_nears = [], []
    for c in chains:
        n0 = max(s0s[c] - QB, 0)
        b0 = NSA_WINDOW - QB + n0 - (s0s[c] - QB)
        n0s.append(n0)
        s_nears.append(jnp.dot(ks_ref[0, 0, pl.ds(n0, 2 * QB), :], q_nears[c], preferred_element_type=F32)
                       + wt_ref[0, pl.ds(b0, 2 * QB), :])
    for c in chains:
        for t in range(min(2, n_far[c])):
            s_bufs[t][c] = qk(c, t)
    m_runs = [jnp.max(s, axis=0, keepdims=True) for s in s_nears]
    p_nears = [jnp.exp2(s - m).astype(BF16) for s, m in zip(s_nears, m_runs)]
    for c in chains:
        acc_ref[c] = jnp.dot(vst_ref[0, 0, :, pl.ds(n0s[c], 2 * QB)], p_nears[c],
                             preferred_element_type=F32)
    for t in range(max(n_far)):
        for c in chains:
            if t < n_far[c]:
                s_buf, p_buf = s_bufs[t % 2], p_bufs[t % 2]
                p, alpha, m_runs[c] = softmax_update(s_buf[c], m_runs[c])
                p_buf[c] = p
                if t + 2 < n_far[c]:
                    s_buf[c] = qk(c, t + 2)
                pv = jnp.dot(vst_ref[0, 0, :, pl.ds(t * FAR_TK, FAR_TK)], p_buf[c], preferred_element_type=F32)
                acc_ref[c] = acc_ref[c] * alpha + pv

    for c in chains:
        a_c = acc_ref[c]
        o_sel = a_c[:HEAD_DIM] / a_c[HEAD_DIM:HEAD_DIM + 1]
        o = gate(c, 0) * o_cmps[c] + gate(c, 1) * o_sel + ow_ref[c]
        o_ref[0, c * QB:(c + 1) * QB, :] = _to_token_rows(o, QB).astype(o_ref.dtype)


def _nsa_attention(qT, gatesT, kc_aug, vcT, ks_aug, vsT, kw_aug, vwT, ovl, cband, wtable):
    B, G, nqb = qT.shape[:3]
    QB, LQ, C = NSA_QB, HPG * NSA_QB, NSA_QPS
    nc, S = kc_aug.shape[2], ks_aug.shape[2]
    per_g = lambda a: pl.BlockSpec((1,) + a.shape[1:], lambda g, b: (g,) + (0,) * (a.ndim - 1))
    rows = lambda a, n: pl.BlockSpec((1, 1, n, a.shape[3]), lambda g, b: (b, g, 0, 0))
    cols = lambda a, n: pl.BlockSpec((1, 1, a.shape[2], n), lambda g, b: (b, g, 0, 0))
    out = None
    for step in range(nqb // C):
        seen = (step + 1) * C * QB
        held = min(S, max(seen, 2 * NSA_WINDOW))
        ncr = min(nc, -(-(seen // CMP_STRIDE) // 128) * 128)
        ovl_r = ovl[:, :ncr]
        in_specs = [pl.BlockSpec((1, 1, C, AUG, LQ), lambda g, b, s=step: (b, g, s, 0, 0)),
                    pl.BlockSpec((1, C, N_GATE, QB), lambda g, b, s=step: (b, s, 0, 0)),
                    rows(kc_aug, ncr), cols(vcT, ncr), rows(ks_aug, held), cols(vsT, held),
                    rows(kw_aug, held), cols(vwT, held), pl.BlockSpec(ovl_r.shape, lambda g, b: (0, 0)),
                    per_g(cband), per_g(wtable)]
        args = [qT, gatesT, kc_aug, vcT, ks_aug, vsT, kw_aug, vwT, ovl_r, cband, wtable]
        aliases = {}
        if out is not None:
            in_specs.append(pl.BlockSpec(memory_space=pl.ANY))
            args.append(out)
            aliases = {len(args) - 1: 0}
        out = pl.pallas_call(
            functools.partial(_nsa_kernel, nc=nc, step=step),
            grid=(G, B),
            in_specs=in_specs,
            out_specs=pl.BlockSpec((1, C * QB, HPG * HEAD_DIM), lambda g, b, s=step: (b, s, g)),
            out_shape=jax.ShapeDtypeStruct((B, nqb * QB, Q_WIDTH), BF16),
            scratch_shapes=[pltpu.VMEM((C, V_ROWS, LQ), F32), pltpu.VMEM((C, HEAD_DIM, LQ), F32),
                            pltpu.VMEM((C, FAR_TK, LQ), F32), pltpu.VMEM((C, FAR_TK, LQ), F32),
                            pltpu.VMEM((C, FAR_TK, LQ), BF16), pltpu.VMEM((C, FAR_TK, LQ), BF16)],
            input_output_aliases=aliases,
            compiler_params=_cparams(("arbitrary", "arbitrary")),
            name=f"nsa_attention_{step}",
        )(*args)
    return out


def _swa_kernel(q_ref, k_ref, vt_ref, band_ref, sink_ref, o_ref):
    QB = SWA_QB
    sink = sink_ref[0, 0:1]
    starts, scores = [], []
    for i in range(SWA_QPS):
        s0 = (pl.program_id(2) * SWA_QPS + i) * QB
        n0 = pl.multiple_of(jnp.maximum(s0 - QB, 0), QB)
        b0 = pl.multiple_of(n0 - (s0 - QB), QB)
        starts.append(n0)
        scores.append(jnp.dot(k_ref[0, 0, pl.ds(n0, 2 * QB), :], q_ref[0, 0, i], preferred_element_type=F32)
                      + band_ref[0, pl.ds(b0, 2 * QB), :])
    maxes = [jnp.maximum(jnp.max(s, axis=0, keepdims=True), sink) for s in scores]
    probs = [jnp.exp2(s - m).astype(BF16) for s, m in zip(scores, maxes)]
    accs = [jnp.dot(vt_ref[0, 0, :, pl.ds(n0, 2 * QB)], p, preferred_element_type=F32)
            for n0, p in zip(starts, probs)]
    for i, (acc, m) in enumerate(zip(accs, maxes)):
        denom = acc[HEAD_DIM:HEAD_DIM + 1] + jnp.exp2(sink - m)
        o_ref[0, i * QB:(i + 1) * QB, :] = _to_token_rows(acc[:HEAD_DIM] / denom, QB).astype(o_ref.dtype)


def _swa_attention(qT, k_aug, vT, band, sink):
    B, G, nqb = qT.shape[:3]
    QB, LQ = SWA_QB, HPG * SWA_QB
    per_bg = lambda a: pl.BlockSpec((1, 1) + a.shape[2:], lambda b, g, q: (b, g) + (0,) * (a.ndim - 2))
    per_g = lambda a: pl.BlockSpec((1,) + a.shape[1:], lambda b, g, q: (g,) + (0,) * (a.ndim - 1))
    return pl.pallas_call(
        _swa_kernel,
        grid=(B, G, nqb // SWA_QPS),
        in_specs=[pl.BlockSpec((1, 1, SWA_QPS, AUG, LQ), lambda b, g, q: (b, g, q, 0, 0)),
                  per_bg(k_aug), per_bg(vT), per_g(band), per_g(sink)],
        out_specs=pl.BlockSpec((1, SWA_QPS * QB, HPG * HEAD_DIM), lambda b, g, q: (b, q, g)),
        out_shape=jax.ShapeDtypeStruct((B, nqb * QB, Q_WIDTH), BF16),
        compiler_params=_cparams(("arbitrary", "arbitrary", "arbitrary")),
        name="swa_attention",
    )(qT, k_aug, vT, band, sink)


def _layer_norm(y, g, b):
    mu = jnp.mean(y, axis=-1, keepdims=True)
    yc = y - mu
    var = jnp.mean(yc * yc, axis=-1, keepdims=True)
    return yc * lax.rsqrt(var + LN_EPS) * g + b


def _top2_of4(a, b, c, d):
    hi1, lo1 = jnp.maximum(a, b), jnp.minimum(a, b)
    hi2, lo2 = jnp.maximum(c, d), jnp.minimum(c, d)
    return jnp.maximum(hi1, hi2) + jnp.maximum(jnp.minimum(hi1, hi2), jnp.maximum(lo1, lo2))


def _route(s, sb):
    n = EXPERTS_PER_GROUP
    score = [_top2_of4(*sb[n * r:n * r + n]) for r in range(N_GROUPS)]
    best = functools.reduce(jnp.maximum, score)
    taken = jnp.zeros_like(best) > 1.0
    in_grp = []
    for r in range(N_GROUPS):
        pick = (score[r] == best) & jnp.logical_not(taken)
        in_grp.append(pick)
        taken = taken | pick
    gates = []
    for e in range(N_EXPERTS):
        r = e // n
        ahead = jnp.zeros_like(best)
        for f in range(n * r, n * r + n):
            if f != e:
                beats = (sb[f] >= sb[e]) if f < e else (sb[f] > sb[e])
                ahead = ahead + jnp.where(beats, 1.0, 0.0)
        gates.append(jnp.where(in_grp[r] & (ahead < 2.0), s[e], 0.0))
    total = functools.reduce(jnp.add, gates)
    return [g / total for g in gates]


def _oproj_kernel(o_ref, x_ref, w_ref, ga_ref, lg_ref, lb_ref, shf_ref, scf_ref, rw_ref, rb_ref,
                  x1_ref, h_ref, gate_ref):
    mix = jnp.dot(o_ref[0], w_ref[...], preferred_element_type=F32)
    x1 = _layer_norm(ALPHA * x_ref[0] + (1.0 + ga_ref[0]) * mix, lg_ref[...], lb_ref[...])
    x1_ref[0] = x1
    h = x1 * (1.0 + scf_ref[0]) + shf_ref[0]
    hb = h.astype(BF16)
    h_ref[0] = hb
    logit = lax.dot_general(rw_ref[...], hb, _NT, preferred_element_type=F32)
    aff = jax.nn.sigmoid(logit)
    biased = aff + rb_ref[...][:, 0:1]
    s = [aff[e:e + 1] for e in range(N_EXPERTS)]
    sb = [biased[e:e + 1] for e in range(N_EXPERTS)]
    gate_ref[0] = jnp.concatenate(_route(s, sb), axis=0)


def _oproj(o, x, w_out, g_a, ln_g, ln_b, sh_f, sc_f, router_w, router_b, ts=512):
    B, S, D = x.shape
    row = lambda a: a.reshape(1, D)
    per_b = pl.BlockSpec((1, 1, D), lambda b, s: (b, 0, 0))
    full = lambda a: pl.BlockSpec(a.shape, lambda b, s: (0,) * a.ndim)
    rwT = router_w.T.astype(BF16)
    rb = jnp.broadcast_to(router_b.reshape(N_EXPERTS, 1), (N_EXPERTS, 128))
    args = (o, x, w_out.astype(BF16), g_a.reshape(B, 1, D), row(ln_g), row(ln_b),
            sh_f.reshape(B, 1, D), sc_f.reshape(B, 1, D), rwT, rb)
    tile = pl.BlockSpec((1, ts, D), lambda b, s: (b, s, 0))
    return pl.pallas_call(
        _oproj_kernel,
        grid=(B, S // ts),
        in_specs=[tile, tile, full(args[2]), per_b, full(args[4]), full(args[5]), per_b, per_b,
                  full(rwT), full(rb)],
        out_specs=[tile, tile, pl.BlockSpec((1, N_EXPERTS, ts), lambda b, s: (b, 0, s))],
        out_shape=[jax.ShapeDtypeStruct((B, S, D), F32), jax.ShapeDtypeStruct((B, S, D), BF16),
                   jax.ShapeDtypeStruct((B, N_EXPERTS, S), F32)],
        compiler_params=_cparams(("arbitrary", "arbitrary")),
        name="oproj_ln_router",
    )(*args)


def _moe_kernel(h_ref, gate_ref, wg_ref, wu_ref, wd_ref, x_ref, gf_ref, lg_ref, lb_ref, o_ref, acc_ref):
    r = pl.program_id(1)
    n = EXPERTS_PER_GROUP

    @pl.when(r == 0)
    def _():
        acc_ref[...] = jnp.zeros_like(acc_ref)

    h = h_ref[...]
    gates = gate_ref[...]
    lane = lax.broadcasted_iota(jnp.int32, gates.shape, 1)
    he = []
    for k in range(n):
        a = jnp.dot(h, wg_ref[k], preferred_element_type=F32)
        u = jnp.dot(h, wu_ref[k], preferred_element_type=F32)
        gcol = jnp.sum(jnp.where(lane == r * n + k, gates, 0.0), axis=1, keepdims=True)
        he.append((a * jax.nn.sigmoid(a) * u * gcol).astype(BF16))
    wd = wd_ref[...].reshape(n * D_EXPERT, wd_ref.shape[-1])
    acc_ref[...] += jnp.dot(jnp.concatenate(he, axis=1), wd, preferred_element_type=F32)

    @pl.when(r == N_GROUPS - 1)
    def _():
        y = ALPHA * x_ref[...] + (1.0 + gf_ref[0]) * acc_ref[...]
        o_ref[...] = _layer_norm(y, lg_ref[...], lb_ref[...])


def _moe(h, gates, w_gate, w_up, w_down, x, g_f, ln_g, ln_b, ts=512):
    B, S, D = x.shape
    T = B * S
    n = EXPERTS_PER_GROUP
    gl = jnp.pad(gates.transpose(0, 2, 1).reshape(T, N_EXPERTS), ((0, 0), (0, 128 - N_EXPERTS)))
    tile = pl.BlockSpec((ts, D), lambda t, e: (t, 0))
    spt = S // ts
    out = pl.pallas_call(
        _moe_kernel,
        grid=(T // ts, N_GROUPS),
        in_specs=[tile, pl.BlockSpec((ts, 128), lambda t, e: (t, 0)),
                  pl.BlockSpec((n, D, D_EXPERT), lambda t, e: (e, 0, 0)),
                  pl.BlockSpec((n, D, D_EXPERT), lambda t, e: (e, 0, 0)),
                  pl.BlockSpec((n, D_EXPERT, D), lambda t, e: (e, 0, 0)),
                  tile, pl.BlockSpec((1, 1, D), lambda t, e: (t // spt, 0, 0)),
                  pl.BlockSpec((1, D), lambda t, e: (0, 0)), pl.BlockSpec((1, D), lambda t, e: (0, 0))],
        out_specs=tile,
        out_shape=jax.ShapeDtypeStruct((T, D), F32),
        scratch_shapes=[pltpu.VMEM((ts, D), F32)],
        compiler_params=_cparams(("arbitrary", "arbitrary")),
        name="moe_ln",
    )(h.reshape(T, D), gl, w_gate.astype(BF16), w_up.astype(BF16), w_down.astype(BF16), x.reshape(T, D),
      g_f.reshape(B, 1, D), ln_g.reshape(1, D), ln_b.reshape(1, D))
    return out.reshape(B, S, D)


def _values_t(v):
    B, G, n, _ = v.shape
    return jnp.concatenate([v.transpose(0, 1, 3, 2), jnp.ones((B, G, V_ROWS - HEAD_DIM, n), v.dtype)], axis=2)


def _nsa_layer_attention(x, sh, sc, w_in, k_w1, k_w2, v_w1, v_w2, k_pos, v_pos, fb):
    B, S, _ = x.shape
    nc, n_slc = S // CMP_STRIDE, S // SLC_LEN
    qT, gatesT, ks_aug, vsT, kw_aug, vwT, ck, cv = _nsa_proj(x, sh, sc, w_in)
    merge = lambda t: t.reshape(B, N_KV * nc, CMP_STRIDE * HEAD_DIM)
    k_cmp = _compress(merge(ck), k_w1, k_w2, k_pos)
    v_cmp = _compress(merge(cv), v_w1, v_w2, v_pos)
    kc_aug = jnp.pad(k_cmp.astype(BF16), ((0, 0), (0, 0), (0, 0), (0, MASK_LANES)))
    vcT = _values_t(v_cmp.astype(BF16))

    c_start = np.arange(nc)[None, :] * CMP_STRIDE
    s_start = np.arange(MASK_LANES)[:, None] * SLC_LEN
    ovl = ((c_start < s_start + SLC_LEN) & (c_start + CMP_LEN > s_start)
           & (np.arange(nc)[None, :] < nc - 1) & (np.arange(MASK_LANES)[:, None] < n_slc))
    ovl = np.concatenate([np.zeros((HEAD_DIM, nc)), ovl.astype(np.float64)], axis=0).astype(np.float32)

    return _nsa_attention(qT, gatesT, kc_aug, vcT, ks_aug, vsT, kw_aug, vwT, jnp.asarray(ovl),
                          _cmp_band(fb, nc, NSA_QB), _window_table(fb, NSA_QB))


def _swa_layer_attention(x, sh, sc, w_q, w_kv, sinks, rel_bias, fb):
    qT, k_aug, vT = _swa_proj(x, sh, sc, w_q, w_kv)
    sink = (sinks.astype(F32) - rel_bias.astype(F32)[N_BUCKETS - 1]) * LOG2E
    sink = jnp.broadcast_to(sink.reshape(N_KV, 1, HPG, 1), (N_KV, 8, HPG, SWA_QB))
    sink = sink.reshape(N_KV, 8, HPG * SWA_QB)
    band = jnp.concatenate([_near_band(fb, SWA_WINDOW, SWA_QB), _masked_rows(SWA_QB, SWA_QB)], axis=1)
    return _swa_attention(qT, k_aug, vT, band, sink)


def kernel(x, c, nsa_w_in, cmp_k_w1, cmp_k_w2, cmp_v_w1, cmp_v_w2, cmp_k_pos, cmp_v_pos, nsa_w_out,
           swa_w_q, swa_sinks, swa_w_out, shared_w_kv, rel_bias, router_w, router_b, moe_w_gate,
           moe_w_up, moe_w_down, ada_w, ada_b, ln_g, ln_b):
    B, S, D = x.shape
    ada = _ada(c, ada_w, ada_b)
    fb = _shifted_bias(rel_bias)
    for layer in range(DEPTH):
        sh_a, sc_a, g_a, sh_f, sc_f, g_f = [ada[layer, :, i * D:(i + 1) * D] for i in range(6)]
        if layer == 0:
            o = _nsa_layer_attention(x, sh_a, sc_a, nsa_w_in[0], cmp_k_w1[0], cmp_k_w2[0], cmp_v_w1[0],
                                     cmp_v_w2[0], cmp_k_pos[0], cmp_v_pos[0], fb)
            w_out = nsa_w_out[0]
        else:
            o = _swa_layer_attention(x, sh_a, sc_a, swa_w_q[0], shared_w_kv, swa_sinks[0], rel_bias, fb)
            w_out = swa_w_out[0]
        x1, h, gates = _oproj(o, x, w_out, g_a, ln_g[layer, 0], ln_b[layer, 0], sh_f, sc_f,
                              router_w, router_b)
        x = _moe(h, gates, moe_w_gate[layer], moe_w_up[layer], moe_w_down[layer], x1, g_f,
                 ln_g[layer, 1], ln_b[layer, 1])
    return x
```

```python
import functools
import math

import numpy as np
import jax
import jax.numpy as jnp
from jax import lax
from jax.experimental import pallas as pl
from jax.experimental.pallas import tpu as pltpu

F32 = jnp.float32
BF16 = jnp.bfloat16

D_MODEL = 1024
HEAD_DIM = 64
N_HEADS = 16
N_KV = 4
HPG = 4
Q_WIDTH = N_HEADS * HEAD_DIM
KV_WIDTH = N_KV * HEAD_DIM
CMP_LEN = 32
CMP_STRIDE = 16
CMP_HIDDEN = 256
SLC_LEN = 64
N_SELECT = 16
NSA_WINDOW = 512
SWA_WINDOW = 128
N_BUCKETS = 32
MAX_DISTANCE = 128
N_EXPERTS = 16
N_GROUPS = 4
EXPERTS_PER_GROUP = 4
D_EXPERT = 256
DEPTH = 2
ALPHA = (2.0 * DEPTH) ** 0.25
LN_EPS = 1e-5
NEG = -1e30
ATTN_SCALE = HEAD_DIM ** -0.5
LOG2E = math.log2(math.e)
Q_SCALE = ATTN_SCALE * LOG2E

NSA_QB = 256
NSA_QPS = 2
SWA_QB = 128
SWA_QPS = 8
AUG = 128
MASK_LANES = AUG - HEAD_DIM
V_ROWS = HEAD_DIM + 16
CMP_LEAD = 16
FAR_TK = 256
N_GATE = 3 * N_HEADS
VMEM_LIMIT = 56 * 1024 * 1024

_NT = (((1,), (1,)), ((), ()))


def _cparams(sem):
    return pltpu.CompilerParams(dimension_semantics=sem, vmem_limit_bytes=VMEM_LIMIT)


def _ada_kernel(c_ref, w_ref, b_ref, o_ref):
    c = c_ref[...]
    cond = c * jax.nn.sigmoid(c)
    o_ref[0] = jnp.dot(cond, w_ref[0], preferred_element_type=F32,
                       precision=lax.Precision.HIGHEST) + b_ref[0]


def _ada(c, ada_w, ada_b):
    B, D = c.shape
    n6 = ada_w.shape[-1]
    tn = 1536
    return pl.pallas_call(
        _ada_kernel,
        grid=(DEPTH, n6 // tn),
        in_specs=[pl.BlockSpec((B, D), lambda l, n: (0, 0)),
                  pl.BlockSpec((1, D, tn), lambda l, n: (l, 0, n)),
                  pl.BlockSpec((1, 1, tn), lambda l, n: (l, 0, n))],
        out_specs=pl.BlockSpec((1, B, tn), lambda l, n: (l, 0, n)),
        out_shape=jax.ShapeDtypeStruct((DEPTH, B, n6), F32),
        compiler_params=_cparams(("arbitrary", "arbitrary")),
        name="ada",
    )(c, ada_w, ada_b.reshape(DEPTH, 1, n6))


def _store_q_blocks(q_ref, qT, ts, qb):
    zeros = jnp.zeros((MASK_LANES, HPG * qb), BF16)
    for g in range(N_KV):
        for k in range(ts // qb):
            for j in range(HPG):
                r0 = (g * HPG + j) * HEAD_DIM
                q_ref[0, g, k, 0:HEAD_DIM, j * qb:(j + 1) * qb] = (
                    qT[r0:r0 + HEAD_DIM, k * qb:(k + 1) * qb].astype(BF16))
            q_ref[0, g, k, HEAD_DIM:AUG, :] = zeros


def _store_vt(vt_ref, vT, ts):
    ones = jnp.ones((V_ROWS - HEAD_DIM, ts), BF16)
    for g in range(N_KV):
        vt_ref[0, g, 0:HEAD_DIM, :] = vT[g * HEAD_DIM:(g + 1) * HEAD_DIM].astype(BF16)
        vt_ref[0, g, HEAD_DIM:V_ROWS, :] = ones


def _nsa_proj_kernel(x_ref, sh_ref, sc_ref, wt_ref, ws_ref, q_ref, gt_ref, ks_ref, vst_ref, kw_ref,
                     vwt_ref, ck_ref, cv_ref, cmp_scr, *, ts):
    s = pl.program_id(1)
    h = (x_ref[0] * (1.0 + sc_ref[0]) + sh_ref[0]).astype(BF16)
    yT = lax.dot_general(wt_ref[...], h, _NT, preferred_element_type=F32)
    y = jnp.dot(h, ws_ref[...], preferred_element_type=F32)
    _store_q_blocks(q_ref, yT[:Q_WIDTH], ts, NSA_QB)
    _store_vt(vst_ref, yT[Q_WIDTH:Q_WIDTH + KV_WIDTH], ts)
    _store_vt(vwt_ref, yT[Q_WIDTH + KV_WIDTH:Q_WIDTH + 2 * KV_WIDTH], ts)
    gates = jax.nn.sigmoid(yT[Q_WIDTH + 2 * KV_WIDTH:])
    for k in range(ts // NSA_QB):
        gt_ref[0, k] = gates[:, k * NSA_QB:(k + 1) * NSA_QB]
    lane = lax.broadcasted_iota(jnp.int32, (ts, AUG), 1)
    tok = s * ts + lax.broadcasted_iota(jnp.int32, (ts, AUG), 0)
    onehot = (lane - HEAD_DIM) == tok // SLC_LEN
    for g in range(N_KV):
        ks_ref[0, g] = jnp.where(onehot, 1.0, y[:, g * AUG:(g + 1) * AUG]).astype(BF16)
        kw_ref[0, g] = y[:, (N_KV + g) * AUG:(N_KV + g + 1) * AUG].astype(BF16)
    base = 2 * N_KV * AUG
    for c in range(2 * KV_WIDTH // 128):
        cmp_scr[c] = y[:, base + c * 128:base + (c + 1) * 128]
    for c, dst in enumerate([ck_ref, ck_ref, cv_ref, cv_ref]):
        for r in range(CMP_STRIDE):
            piece = cmp_scr[c, pl.ds(r, ts // CMP_STRIDE, stride=CMP_STRIDE), :]
            for half in range(2):
                g = 2 * (c % 2) + half
                dst[0, g, :, r * HEAD_DIM:(r + 1) * HEAD_DIM] = (
                    piece[:, half * HEAD_DIM:(half + 1) * HEAD_DIM].astype(BF16))


def _pad_heads(w):
    D = w.shape[0]
    w = w.reshape(D, N_KV, HEAD_DIM)
    return jnp.pad(w, ((0, 0), (0, 0), (0, MASK_LANES))).reshape(D, N_KV * AUG)


def _nsa_proj(x, shift, scale, w_in, ts=512):
    B, S, D = x.shape
    nqb, lq = S // NSA_QB, HPG * NSA_QB
    cols = lambda i: w_in[:, Q_WIDTH + i * KV_WIDTH:Q_WIDTH + (i + 1) * KV_WIDTH]
    w_kc, w_vc, w_ksl, w_vsl, w_kw, w_vw = [cols(i) for i in range(6)]
    wt = jnp.concatenate([w_in[:, :Q_WIDTH] * Q_SCALE, w_vsl, w_vw,
                          w_in[:, Q_WIDTH + 6 * KV_WIDTH:]], axis=1).T.astype(BF16)
    ws = jnp.concatenate([_pad_heads(w_ksl), _pad_heads(w_kw), w_kc, w_vc], axis=1).astype(BF16)
    per_b = pl.BlockSpec((1, 1, D), lambda b, s: (b, 0, 0))
    kspec = pl.BlockSpec((1, N_KV, ts, AUG), lambda b, s: (b, 0, s, 0))
    vspec = pl.BlockSpec((1, N_KV, V_ROWS, ts), lambda b, s: (b, 0, 0, s))
    k_shape = jax.ShapeDtypeStruct((B, N_KV, S, AUG), BF16)
    v_shape = jax.ShapeDtypeStruct((B, N_KV, V_ROWS, S), BF16)
    chunk = CMP_STRIDE * HEAD_DIM
    cspec = pl.BlockSpec((1, N_KV, ts // CMP_STRIDE, chunk), lambda b, s: (b, 0, s, 0))
    c_shape = jax.ShapeDtypeStruct((B, N_KV, S // CMP_STRIDE, chunk), BF16)
    return pl.pallas_call(
        functools.partial(_nsa_proj_kernel, ts=ts),
        grid=(B, S // ts),
        in_specs=[pl.BlockSpec((1, ts, D), lambda b, s: (b, s, 0)), per_b, per_b,
                  pl.BlockSpec(wt.shape, lambda b, s: (0, 0)), pl.BlockSpec(ws.shape, lambda b, s: (0, 0))],
        out_specs=[pl.BlockSpec((1, N_KV, ts // NSA_QB, AUG, lq), lambda b, s: (b, 0, s, 0, 0)),
                   pl.BlockSpec((1, ts // NSA_QB, N_GATE, NSA_QB), lambda b, s: (b, s, 0, 0)),
                   kspec, vspec, kspec, vspec,
                   cspec, cspec],
        out_shape=[jax.ShapeDtypeStruct((B, N_KV, nqb, AUG, lq), BF16),
                   jax.ShapeDtypeStruct((B, nqb, N_GATE, NSA_QB), F32),
                   k_shape, v_shape, k_shape, v_shape,
                   c_shape, c_shape],
        scratch_shapes=[pltpu.VMEM((2 * KV_WIDTH // 128, ts, 128), F32)],
        compiler_params=_cparams(("arbitrary", "arbitrary")),
        name="nsa_proj",
    )(x, shift.reshape(B, 1, D), scale.reshape(B, 1, D), wt, ws)


def _swa_proj_kernel(x_ref, sh_ref, sc_ref, wq_ref, wk_ref, wv_ref, q_ref, k_ref, vt_ref, *, ts):
    x = x_ref[0]
    h = (x * (1.0 + sc_ref[0]) + sh_ref[0]).astype(BF16)
    xb = x.astype(BF16)
    _store_q_blocks(q_ref, lax.dot_general(wq_ref[...], h, _NT, preferred_element_type=F32), ts, SWA_QB)
    _store_vt(vt_ref, lax.dot_general(wv_ref[...], xb, _NT, preferred_element_type=F32), ts)
    k = jnp.dot(xb, wk_ref[...], preferred_element_type=F32)
    for g in range(N_KV):
        k_ref[0, g] = k[:, g * AUG:(g + 1) * AUG].astype(BF16)


def _swa_proj(x, shift, scale, w_q, w_kv, ts=512):
    B, S, D = x.shape
    lq = HPG * SWA_QB
    wq = (w_q * Q_SCALE).T.astype(BF16)
    wk = _pad_heads(w_kv[:, :KV_WIDTH]).astype(BF16)
    wv = w_kv[:, KV_WIDTH:].T.astype(BF16)
    per_b = pl.BlockSpec((1, 1, D), lambda b, s: (b, 0, 0))
    full = lambda a: pl.BlockSpec(a.shape, lambda b, s: (0, 0))
    return pl.pallas_call(
        functools.partial(_swa_proj_kernel, ts=ts),
        grid=(B, S // ts),
        in_specs=[pl.BlockSpec((1, ts, D), lambda b, s: (b, s, 0)), per_b, per_b, full(wq), full(wk), full(wv)],
        out_specs=[pl.BlockSpec((1, N_KV, ts // SWA_QB, AUG, lq), lambda b, s: (b, 0, s, 0, 0)),
                   pl.BlockSpec((1, N_KV, ts, AUG), lambda b, s: (b, 0, s, 0)),
                   pl.BlockSpec((1, N_KV, V_ROWS, ts), lambda b, s: (b, 0, 0, s))],
        out_shape=[jax.ShapeDtypeStruct((B, N_KV, S // SWA_QB, AUG, lq), BF16),
                   jax.ShapeDtypeStruct((B, N_KV, S, AUG), BF16),
                   jax.ShapeDtypeStruct((B, N_KV, V_ROWS, S), BF16)],
        compiler_params=_cparams(("arbitrary", "arbitrary")),
        name="swa_proj",
    )(x, shift.reshape(B, 1, D), scale.reshape(B, 1, D), wq, wk, wv)


def _gelu_tanh(x):
    return 0.5 * x * (1.0 + jnp.tanh(math.sqrt(2.0 / math.pi) * (x + 0.044715 * (x * x * x))))


def _compress_kernel(ch_ref, w1_ref, w2_ref, pos_ref, o_ref, *, nc):
    half = CMP_STRIDE * HEAD_DIM
    ch = ch_ref[0]
    w1 = w1_ref[...]
    top = jnp.dot(ch, w1[:half], preferred_element_type=F32)
    bot = jnp.dot(ch, w1[half:], preferred_element_type=F32)
    posw = jnp.dot(pos_ref[...], w1, preferred_element_type=F32)[0:1]
    w2 = w2_ref[...]
    for g in range(N_KV):
        bot_g = pltpu.roll(bot[g * nc:(g + 1) * nc], nc - 1, 0)
        pre = top[g * nc:(g + 1) * nc] + bot_g + posw
        o_ref[0, g] = jnp.dot(_gelu_tanh(pre).astype(BF16), w2, preferred_element_type=F32)


def _compress(chunks, w1, w2, pos):
    B, gn, ck = chunks.shape
    nc = gn // N_KV
    pos8 = jnp.broadcast_to(pos.reshape(1, CMP_LEN * HEAD_DIM), (8, CMP_LEN * HEAD_DIM)).astype(BF16)
    return pl.pallas_call(
        functools.partial(_compress_kernel, nc=nc),
        grid=(B,),
        in_specs=[pl.BlockSpec((1, gn, ck), lambda b: (b, 0, 0)),
                  pl.BlockSpec(w1.shape, lambda b: (0, 0)),
                  pl.BlockSpec(w2.shape, lambda b: (0, 0)),
                  pl.BlockSpec(pos8.shape, lambda b: (0, 0))],
        out_specs=pl.BlockSpec((1, N_KV, nc, HEAD_DIM), lambda b: (b, 0, 0, 0)),
        out_shape=jax.ShapeDtypeStruct((B, N_KV, nc, HEAD_DIM), F32),
        compiler_params=_cparams(("arbitrary",)),
        name="compress",
    )(chunks, w1.astype(BF16), w2.astype(BF16), pos8)


def _bucket_np(d):
    d = np.maximum(np.asarray(d, np.int64), 0)
    max_exact = N_BUCKETS // 2
    large = max_exact + (np.log(np.maximum(d, 1).astype(np.float32) / np.float32(max_exact))
                         / np.float32(math.log(MAX_DISTANCE / max_exact))
                         * np.float32(N_BUCKETS - max_exact)).astype(np.int32)
    large = np.minimum(large, N_BUCKETS - 1)
    return np.where(d < max_exact, d, large).astype(np.int32)


def _shifted_bias(rel_table):
    t = rel_table.astype(F32)
    return ((t[_bucket_np(np.arange(MAX_DISTANCE))] - t[N_BUCKETS - 1][None, :]) * LOG2E).T


def _to_group_lanes(m):
    H, K, Q = m.shape
    return m.reshape(N_KV, HPG, K, Q).transpose(0, 2, 1, 3).reshape(N_KV, K, HPG * Q)


def _bias_of_dist(fb, dist, window):
    ok = (dist >= 0) & (dist < window)
    idx = np.clip(dist, 0, MAX_DISTANCE - 1)
    return jnp.where(ok[None], jnp.where((dist >= MAX_DISTANCE)[None], 0.0, fb[:, idx]), NEG)


def _toeplitz_pair(fb, window):
    H, t = fb.shape[0], MAX_DISTANCE
    nv = 3 * t - 1
    v = _bias_of_dist(fb, np.arange(nv) - (t - 1), window)
    rows = 2 * t
    flat = jnp.tile(v, (1, rows + 1))[:, :rows * (nv + 1)]
    return flat.reshape(H, rows, nv + 1)[:, ::-1, :t]


def _near_band(fb, window, qb):
    H, t = fb.shape[0], MAX_DISTANCE
    nb = qb // t
    assert qb % t == 0 and (nb == 1 or window >= 2 * qb)
    pair = _toeplitz_pair(fb, window)
    blocks = {1: pair[:, :t], 0: pair[:, t:]}
    zero, masked = jnp.zeros((H, t, t), F32), jnp.full((H, t, t), NEG, F32)
    rows = []
    for a in range(2 * nb):
        row = [blocks.get(b + nb - a, zero if b + nb - a > 1 else masked) for b in range(nb)]
        rows.append(jnp.concatenate(row, axis=2))
    return _to_group_lanes(jnp.concatenate(rows, axis=1))


def _masked_rows(n, qb):
    return jnp.full((N_KV, n, HPG * qb), NEG, F32)


def _window_table(fb, qb):
    kk = np.arange(qb)[:, None]
    qi = np.arange(HPG * qb)[None, :] % qb
    edge = np.where(kk > qi, 0.0, NEG).astype(np.float32)
    edge = jnp.broadcast_to(jnp.asarray(edge), (N_KV, qb, HPG * qb))
    mid = jnp.zeros((N_KV, NSA_WINDOW - 2 * qb, HPG * qb), F32)
    return jnp.concatenate([edge, mid, _near_band(fb, NSA_WINDOW, qb), _masked_rows(NSA_WINDOW, qb)], axis=1)


def _cmp_band_rows(qb):
    return CMP_LEAD + qb // CMP_STRIDE


def _cmp_band(fb, nc, qb):
    m = np.arange(_cmp_band_rows(qb))[:, None]
    i = np.arange(qb)[None, :]
    dist = i + CMP_STRIDE * CMP_LEAD - CMP_STRIDE * m - (CMP_LEN - 1)
    band = _to_group_lanes(_bias_of_dist(fb, dist, 1 << 30))
    return jnp.concatenate([jnp.zeros((N_KV, nc, HPG * qb), F32), band, _masked_rows(nc, qb)], axis=1)


def _to_token_rows(oT, qb):
    pair = lambda a: jnp.concatenate([oT[:, a * qb:(a + 1) * qb], oT[:, (a + 1) * qb:(a + 2) * qb]], axis=0)
    return jnp.concatenate([pair(0).T, pair(2).T], axis=1)


def _nsa_kernel(q_ref, gt_ref, kc_ref, vct_ref, ks_ref, vst_ref, kw_ref, vwt_ref, ovl_ref, cb_ref,
                wt_ref, *rest, nc, step):
    o_ref, acc_ref, ow_ref, sa_ref, sb_ref, pa_ref, pb_ref = rest[-7:]
    QB, LQ, C = NSA_QB, HPG * NSA_QB, NSA_QPS
    g = pl.program_id(0)
    chains = range(C)
    ncr = kc_ref.shape[2]
    qbs = [step * C + c for c in chains]
    s0s = [qb * QB for qb in qbs]
    nblk = (step + 1) * C * QB // SLC_LEN
    qTs = [q_ref[0, 0, c] for c in chains]

    def gate(c, branch):
        rows = [gt_ref[0, c, pl.ds(branch * N_HEADS + g * HPG + j, 1), :] for j in range(HPG)]
        return jnp.concatenate(rows, axis=1)

    scs, sws, w0s = [], [], []
    for c in chains:
        cstart = nc + CMP_LEAD - (QB // CMP_STRIDE) * qbs[c]
        scs.append(jnp.dot(kc_ref[0, 0], qTs[c], preferred_element_type=F32)
                   + cb_ref[0, pl.ds(cstart, ncr), :])
    win_keys = NSA_WINDOW + QB
    for c in chains:
        w0 = max(s0s[c] - NSA_WINDOW, 0)
        t0 = w0 - (s0s[c] - NSA_WINDOW)
        w0s.append(w0)
        sws.append(jnp.dot(kw_ref[0, 0, pl.ds(w0, win_keys), :], qTs[c], preferred_element_type=F32)
                   + wt_ref[0, pl.ds(t0, win_keys), :])

    p_cs = []
    for c in chains:
        m_c = jnp.max(scs[c], axis=0, keepdims=True)
        e_c = jnp.exp2(scs[c] - m_c)
        l_c = jnp.sum(e_c, axis=0, keepdims=True)
        p_cs.append(e_c * jnp.where(m_c > 0.1 * NEG, 1.0 / l_c, 0.0))
    o_cmps, imps = [], []
    for c in chains:
        o_cmps.append(jnp.dot(vct_ref[0, 0], p_cs[c].astype(BF16), preferred_element_type=F32)[:HEAD_DIM])
        p_sum = functools.reduce(jnp.add, [p_cs[c][:, j * QB:(j + 1) * QB] for j in range(HPG)])
        imps.append(jnp.dot(ovl_ref[...], p_sum, preferred_element_type=F32,
                            precision=lax.Precision.HIGHEST)[HEAD_DIM:HEAD_DIM + nblk])

    p_ws = []
    for c in chains:
        m_w = jnp.max(sws[c], axis=0, keepdims=True)
        p_ws.append(jnp.exp2(sws[c] - m_w).astype(BF16))
    acc_ws = [jnp.dot(vwt_ref[0, 0, :, pl.ds(w0s[c], win_keys)], p_ws[c], preferred_element_type=F32)
              for c in chains]

    blk = lax.broadcasted_iota(jnp.int32, (nblk, QB), 0)
    sub = lax.broadcasted_iota(jnp.int32, (8, QB), 0)
    unseen = [jnp.full((MASK_LANES - nblk, LQ), NEG, BF16)] if nblk < MASK_LANES else []
    q_nears, q_fars = [], []
    for c in chains:
        tq = s0s[c] + lax.broadcasted_iota(jnp.int32, (nblk, QB), 1)
        cur = tq // SLC_LEN
        forced = (blk == 0) | (blk == cur) | (blk == cur - 1)
        future = blk > cur
        val = jnp.where(forced, 1e9, jnp.where(future, -1e9, imps[c]))
        vals = [val[8 * a:8 * a + 8] for a in range(nblk // 8)]
        cnts = [jnp.zeros((8, QB), F32) for _ in range(nblk // 8)]
        for jp in range(nblk):
            ap, r = divmod(jp, 8)
            rv = jnp.broadcast_to(vals[ap][r:r + 1], (8, QB))
            for a in range(nblk // 8):
                gt = jnp.where(rv > vals[a], 1.0, 0.0)
                ge = jnp.where(rv >= vals[a], 1.0, 0.0)
                if a < ap:
                    inc = gt
                elif a > ap:
                    inc = ge
                else:
                    inc = jnp.where(sub > r, ge, gt)
                cnts[a] = cnts[a] + inc
        cnt = jnp.concatenate(cnts, axis=0)
        allowed = (cnt < float(N_SELECT)) & (blk <= cur)
        near_blk = blk >= (s0s[c] - QB) // SLC_LEN
        neg_near = jnp.where(allowed, 0.0, NEG).astype(BF16)
        neg_far = jnp.where(allowed & jnp.logical_not(near_blk), 0.0, NEG).astype(BF16)
        q_nears.append(jnp.concatenate([qTs[c][:HEAD_DIM], jnp.tile(neg_near, (1, HPG))] + unseen, axis=0))
        q_fars.append(jnp.concatenate([qTs[c][:HEAD_DIM], jnp.tile(neg_far, (1, HPG))] + unseen, axis=0))
    for c in chains:
        ow_ref[c] = gate(c, 2) * (acc_ws[c][:HEAD_DIM] / acc_ws[c][HEAD_DIM:HEAD_DIM + 1])

    s_bufs, p_bufs = (sa_ref, sb_ref), (pa_ref, pb_ref)
    n_far = [max(s0s[c] - QB, 0) // FAR_TK for c in chains]

    def qk(c, t):
        return jnp.dot(ks_ref[0, 0, pl.ds(t * FAR_TK, FAR_TK), :], q_fars[c], preferred_element_type=F32)

    def softmax_update(s, m_run):
        m_new = jnp.maximum(m_run, jnp.max(s, axis=0, keepdims=True))
        return jnp.exp2(s - m_new).astype(BF16), jnp.exp2(m_run - m_new), m_new

    n0s, s_nears = [], []
    for c in chains:
        n0 = max(s0s[c] - QB, 0)
        b0 = NSA_WINDOW - QB + n0 - (s0s[c] - QB)
        n0s.append(n0)
        s_nears.append(jnp.dot(ks_ref[0, 0, pl.ds(n0, 2 * QB), :], q_nears[c], preferred_element_type=F32)
                       + wt_ref[0, pl.ds(b0, 2 * QB), :])
    for c in chains:
        for t in range(min(2, n_far[c])):
            s_bufs[t][c] = qk(c, t)
    m_runs = [jnp.max(s, axis=0, keepdims=True) for s in s_nears]
    p_nears = [jnp.exp2(s - m).astype(BF16) for s, m in zip(s_nears, m_runs)]
    for c in chains:
        acc_ref[c] = jnp.dot(vst_ref[0, 0, :, pl.ds(n0s[c], 2 * QB)], p_nears[c],
                             preferred_element_type=F32)
    for t in range(max(n_far)):
        for c in chains:
            if t < n_far[c]:
                s_buf, p_buf = s_bufs[t % 2], p_bufs[t % 2]
                p, alpha, m_runs[c] = softmax_update(s_buf[c], m_runs[c])
                p_buf[c] = p
                if t + 2 < n_far[c]:
                    s_buf[c] = qk(c, t + 2)
                pv = jnp.dot(vst_ref[0, 0, :, pl.ds(t * FAR_TK, FAR_TK)], p_buf[c], preferred_element_type=F32)
                acc_ref[c] = acc_ref[c] * alpha + pv

    for c in chains:
        a_c = acc_ref[c]
        o_sel = a_c[:HEAD_DIM] / a_c[HEAD_DIM:HEAD_DIM + 1]
        o = gate(c, 0) * o_cmps[c] + gate(c, 1) * o_sel + ow_ref[c]
        o_ref[0, c * QB:(c + 1) * QB, :] = _to_token_rows(o, QB).astype(o_ref.dtype)


def _nsa_attention(qT, gatesT, kc_aug, vcT, ks_aug, vsT, kw_aug, vwT, ovl, cband, wtable):
    B, G, nqb = qT.shape[:3]
    QB, LQ, C = NSA_QB, HPG * NSA_QB, NSA_QPS
    nc, S = kc_aug.shape[2], ks_aug.shape[2]
    per_g = lambda a: pl.BlockSpec((1,) + a.shape[1:], lambda g, b: (g,) + (0,) * (a.ndim - 1))
    rows = lambda a, n: pl.BlockSpec((1, 1, n, a.shape[3]), lambda g, b: (b, g, 0, 0))
    cols = lambda a, n: pl.BlockSpec((1, 1, a.shape[2], n), lambda g, b: (b, g, 0, 0))
    out = None
    for step in range(nqb // C):
        seen = (step + 1) * C * QB
        held = min(S, max(seen, 2 * NSA_WINDOW))
        ncr = min(nc, -(-(seen // CMP_STRIDE) // 128) * 128)
        ovl_r = ovl[:, :ncr]
        in_specs = [pl.BlockSpec((1, 1, C, AUG, LQ), lambda g, b, s=step: (b, g, s, 0, 0)),
                    pl.BlockSpec((1, C, N_GATE, QB), lambda g, b, s=step: (b, s, 0, 0)),
                    rows(kc_aug, ncr), cols(vcT, ncr), rows(ks_aug, held), cols(vsT, held),
                    rows(kw_aug, held), cols(vwT, held), pl.BlockSpec(ovl_r.shape, lambda g, b: (0, 0)),
                    per_g(cband), per_g(wtable)]
        args = [qT, gatesT, kc_aug, vcT, ks_aug, vsT, kw_aug, vwT, ovl_r, cband, wtable]
        aliases = {}
        if out is not None:
            in_specs.append(pl.BlockSpec(memory_space=pl.ANY))
            args.append(out)
            aliases = {len(args) - 1: 0}
        out = pl.pallas_call(
            functools.partial(_nsa_kernel, nc=nc, step=step),
            grid=(G, B),
            in_specs=in_specs,
            out_specs=pl.BlockSpec((1, C * QB, HPG * HEAD_DIM), lambda g, b, s=step: (b, s, g)),
            out_shape=jax.ShapeDtypeStruct((B, nqb * QB, Q_WIDTH), BF16),
            scratch_shapes=[pltpu.VMEM((C, V_ROWS, LQ), F32), pltpu.VMEM((C, HEAD_DIM, LQ), F32),
                            pltpu.VMEM((C, FAR_TK, LQ), F32), pltpu.VMEM((C, FAR_TK, LQ), F32),
                            pltpu.VMEM((C, FAR_TK, LQ), BF16), pltpu.VMEM((C, FAR_TK, LQ), BF16)],
            input_output_aliases=aliases,
            compiler_params=_cparams(("arbitrary", "arbitrary")),
            name=f"nsa_attention_{step}",
        )(*args)
    return out


def _swa_kernel(q_ref, k_ref, vt_ref, band_ref, sink_ref, o_ref):
    QB = SWA_QB
    sink = sink_ref[0, 0:1]
    starts, scores = [], []
    for i in range(SWA_QPS):
        s0 = (pl.program_id(2) * SWA_QPS + i) * QB
        n0 = pl.multiple_of(jnp.maximum(s0 - QB, 0), QB)
        b0 = pl.multiple_of(n0 - (s0 - QB), QB)
        starts.append(n0)
        scores.append(jnp.dot(k_ref[0, 0, pl.ds(n0, 2 * QB), :], q_ref[0, 0, i], preferred_element_type=F32)
                      + band_ref[0, pl.ds(b0, 2 * QB), :])
    maxes = [jnp.maximum(jnp.max(s, axis=0, keepdims=True), sink) for s in scores]
    probs = [jnp.exp2(s - m).astype(BF16) for s, m in zip(scores, maxes)]
    accs = [jnp.dot(vt_ref[0, 0, :, pl.ds(n0, 2 * QB)], p, preferred_element_type=F32)
            for n0, p in zip(starts, probs)]
    for i, (acc, m) in enumerate(zip(accs, maxes)):
        denom = acc[HEAD_DIM:HEAD_DIM + 1] + jnp.exp2(sink - m)
        o_ref[0, i * QB:(i + 1) * QB, :] = _to_token_rows(acc[:HEAD_DIM] / denom, QB).astype(o_ref.dtype)


def _swa_attention(qT, k_aug, vT, band, sink):
    B, G, nqb = qT.shape[:3]
    QB, LQ = SWA_QB, HPG * SWA_QB
    per_bg = lambda a: pl.BlockSpec((1, 1) + a.shape[2:], lambda b, g, q: (b, g) + (0,) * (a.ndim - 2))
    per_g = lambda a: pl.BlockSpec((1,) + a.shape[1:], lambda b, g, q: (g,) + (0,) * (a.ndim - 1))
    return pl.pallas_call(
        _swa_kernel,
        grid=(B, G, nqb // SWA_QPS),
        in_specs=[pl.BlockSpec((1, 1, SWA_QPS, AUG, LQ), lambda b, g, q: (b, g, q, 0, 0)),
                  per_bg(k_aug), per_bg(vT), per_g(band), per_g(sink)],
        out_specs=pl.BlockSpec((1, SWA_QPS * QB, HPG * HEAD_DIM), lambda b, g, q: (b, q, g)),
        out_shape=jax.ShapeDtypeStruct((B, nqb * QB, Q_WIDTH), BF16),
        compiler_params=_cparams(("arbitrary", "arbitrary", "arbitrary")),
        name="swa_attention",
    )(qT, k_aug, vT, band, sink)


def _layer_norm(y, g, b):
    mu = jnp.mean(y, axis=-1, keepdims=True)
    yc = y - mu
    var = jnp.mean(yc * yc, axis=-1, keepdims=True)
    return yc * lax.rsqrt(var + LN_EPS) * g + b


def _top2_of4(a, b, c, d):
    hi1, lo1 = jnp.maximum(a, b), jnp.minimum(a, b)
    hi2, lo2 = jnp.maximum(c, d), jnp.minimum(c, d)
    return jnp.maximum(hi1, hi2) + jnp.maximum(jnp.minimum(hi1, hi2), jnp.maximum(lo1, lo2))


def _route(s, sb):
    n = EXPERTS_PER_GROUP
    score = [_top2_of4(*sb[n * r:n * r + n]) for r in range(N_GROUPS)]
    best = functools.reduce(jnp.maximum, score)
    taken = jnp.zeros_like(best) > 1.0
    in_grp = []
    for r in range(N_GROUPS):
        pick = (score[r] == best) & jnp.logical_not(taken)
        in_grp.append(pick)
        taken = taken | pick
    gates = []
    for e in range(N_EXPERTS):
        r = e // n
        ahead = jnp.zeros_like(best)
        for f in range(n * r, n * r + n):
            if f != e:
                beats = (sb[f] >= sb[e]) if f < e else (sb[f] > sb[e])
                ahead = ahead + jnp.where(beats, 1.0, 0.0)
        gates.append(jnp.where(in_grp[r] & (ahead < 2.0), s[e], 0.0))
    total = functools.reduce(jnp.add, gates)
    return [g / total for g in gates]


def _oproj_kernel(o_ref, x_ref, w_ref, ga_ref, lg_ref, lb_ref, shf_ref, scf_ref, rw_ref, rb_ref,
                  x1_ref, h_ref, gate_ref):
    mix = jnp.dot(o_ref[0], w_ref[...], preferred_element_type=F32)
    x1 = _layer_norm(ALPHA * x_ref[0] + (1.0 + ga_ref[0]) * mix, lg_ref[...], lb_ref[...])
    x1_ref[0] = x1
    h = x1 * (1.0 + scf_ref[0]) + shf_ref[0]
    hb = h.astype(BF16)
    h_ref[0] = hb
    logit = lax.dot_general(rw_ref[...], hb, _NT, preferred_element_type=F32)
    aff = jax.nn.sigmoid(logit)
    biased = aff + rb_ref[...][:, 0:1]
    s = [aff[e:e + 1] for e in range(N_EXPERTS)]
    sb = [biased[e:e + 1] for e in range(N_EXPERTS)]
    gate_ref[0] = jnp.concatenate(_route(s, sb), axis=0)


def _oproj(o, x, w_out, g_a, ln_g, ln_b, sh_f, sc_f, router_w, router_b, ts=512):
    B, S, D = x.shape
    row = lambda a: a.reshape(1, D)
    per_b = pl.BlockSpec((1, 1, D), lambda b, s: (b, 0, 0))
    full = lambda a: pl.BlockSpec(a.shape, lambda b, s: (0,) * a.ndim)
    rwT = router_w.T.astype(BF16)
    rb = jnp.broadcast_to(router_b.reshape(N_EXPERTS, 1), (N_EXPERTS, 128))
    args = (o, x, w_out.astype(BF16), g_a.reshape(B, 1, D), row(ln_g), row(ln_b),
            sh_f.reshape(B, 1, D), sc_f.reshape(B, 1, D), rwT, rb)
    tile = pl.BlockSpec((1, ts, D), lambda b, s: (b, s, 0))
    return pl.pallas_call(
        _oproj_kernel,
        grid=(B, S // ts),
        in_specs=[tile, tile, full(args[2]), per_b, full(args[4]), full(args[5]), per_b, per_b,
                  full(rwT), full(rb)],
        out_specs=[tile, tile, pl.BlockSpec((1, N_EXPERTS, ts), lambda b, s: (b, 0, s))],
        out_shape=[jax.ShapeDtypeStruct((B, S, D), F32), jax.ShapeDtypeStruct((B, S, D), BF16),
                   jax.ShapeDtypeStruct((B, N_EXPERTS, S), F32)],
        compiler_params=_cparams(("arbitrary", "arbitrary")),
        name="oproj_ln_router",
    )(*args)


def _moe_kernel(h_ref, gate_ref, wg_ref, wu_ref, wd_ref, x_ref, gf_ref, lg_ref, lb_ref, o_ref, acc_ref):
    n = EXPERTS_PER_GROUP
    h = h_ref[...]
    gates = gate_ref[...]
    lane = lax.broadcasted_iota(jnp.int32, gates.shape, 1)
    for r in range(N_GROUPS):
        he = []
        for k in range(n):
            e = r * n + k
            a = jnp.dot(h, wg_ref[e], preferred_element_type=F32)
            u = jnp.dot(h, wu_ref[e], preferred_element_type=F32)
            gcol = jnp.sum(jnp.where(lane == e, gates, 0.0), axis=1, keepdims=True)
            he.append((a * jax.nn.sigmoid(a) * u * gcol).astype(BF16))
        wd = wd_ref[r * n:(r + 1) * n].reshape(n * D_EXPERT, wd_ref.shape[-1])
        ffn = jnp.dot(jnp.concatenate(he, axis=1), wd, preferred_element_type=F32)
        if r == 0:
            acc_ref[...] = ffn
        else:
            acc_ref[...] += ffn
    y = ALPHA * x_ref[...] + (1.0 + gf_ref[0]) * acc_ref[...]
    o_ref[...] = _layer_norm(y, lg_ref[...], lb_ref[...])


def _moe(h, gates, w_gate, w_up, w_down, x, g_f, ln_g, ln_b, ts=512):
    B, S, D = x.shape
    T = B * S
    gl = jnp.pad(gates.transpose(0, 2, 1).reshape(T, N_EXPERTS), ((0, 0), (0, 128 - N_EXPERTS)))
    tile = pl.BlockSpec((ts, D), lambda t: (t, 0))
    spt = S // ts
    resident = lambda a: pl.BlockSpec(a.shape, lambda t: (0,) * a.ndim, pipeline_mode=pl.Buffered(1))
    wg, wu, wd = w_gate.astype(BF16), w_up.astype(BF16), w_down.astype(BF16)
    out = pl.pallas_call(
        _moe_kernel,
        grid=(T // ts,),
        in_specs=[tile, pl.BlockSpec((ts, 128), lambda t: (t, 0)), resident(wg), resident(wu), resident(wd),
                  tile, pl.BlockSpec((1, 1, D), lambda t: (t // spt, 0, 0)),
                  pl.BlockSpec((1, D), lambda t: (0, 0)), pl.BlockSpec((1, D), lambda t: (0, 0))],
        out_specs=tile,
        out_shape=jax.ShapeDtypeStruct((T, D), F32),
        scratch_shapes=[pltpu.VMEM((ts, D), F32)],
        compiler_params=_cparams(("arbitrary",)),
        name="moe_ln",
    )(h.reshape(T, D), gl, wg, wu, wd, x.reshape(T, D), g_f.reshape(B, 1, D), ln_g.reshape(1, D),
      ln_b.reshape(1, D))
    return out.reshape(B, S, D)


def _values_t(v):
    B, G, n, _ = v.shape
    return jnp.concatenate([v.transpose(0, 1, 3, 2), jnp.ones((B, G, V_ROWS - HEAD_DIM, n), v.dtype)], axis=2)


def _nsa_layer_attention(x, sh, sc, w_in, k_w1, k_w2, v_w1, v_w2, k_pos, v_pos, fb):
    B, S, _ = x.shape
    nc, n_slc = S // CMP_STRIDE, S // SLC_LEN
    qT, gatesT, ks_aug, vsT, kw_aug, vwT, ck, cv = _nsa_proj(x, sh, sc, w_in)
    merge = lambda t: t.reshape(B, N_KV * nc, CMP_STRIDE * HEAD_DIM)
    k_cmp = _compress(merge(ck), k_w1, k_w2, k_pos)
    v_cmp = _compress(merge(cv), v_w1, v_w2, v_pos)
    kc_aug = jnp.pad(k_cmp.astype(BF16), ((0, 0), (0, 0), (0, 0), (0, MASK_LANES)))
    vcT = _values_t(v_cmp.astype(BF16))

    c_start = np.arange(nc)[None, :] * CMP_STRIDE
    s_start = np.arange(MASK_LANES)[:, None] * SLC_LEN
    ovl = ((c_start < s_start + SLC_LEN) & (c_start + CMP_LEN > s_start)
           & (np.arange(nc)[None, :] < nc - 1) & (np.arange(MASK_LANES)[:, None] < n_slc))
    ovl = np.concatenate([np.zeros((HEAD_DIM, nc)), ovl.astype(np.float64)], axis=0).astype(np.float32)

    return _nsa_attention(qT, gatesT, kc_aug, vcT, ks_aug, vsT, kw_aug, vwT, jnp.asarray(ovl),
                          _cmp_band(fb, nc, NSA_QB), _window_table(fb, NSA_QB))


def _swa_layer_attention(x, sh, sc, w_q, w_kv, sinks, rel_bias, fb):
    qT, k_aug, vT = _swa_proj(x, sh, sc, w_q, w_kv)
    sink = (sinks.astype(F32) - rel_bias.astype(F32)[N_BUCKETS - 1]) * LOG2E
    sink = jnp.broadcast_to(sink.reshape(N_KV, 1, HPG, 1), (N_KV, 8, HPG, SWA_QB))
    sink = sink.reshape(N_KV, 8, HPG * SWA_QB)
    band = jnp.concatenate([_near_band(fb, SWA_WINDOW, SWA_QB), _masked_rows(SWA_QB, SWA_QB)], axis=1)
    return _swa_attention(qT, k_aug, vT, band, sink)


def kernel(x, c, nsa_w_in, cmp_k_w1, cmp_k_w2, cmp_v_w1, cmp_v_w2, cmp_k_pos, cmp_v_pos, nsa_w_out,
           swa_w_q, swa_sinks, swa_w_out, shared_w_kv, rel_bias, router_w, router_b, moe_w_gate,
           moe_w_up, moe_w_down, ada_w, ada_b, ln_g, ln_b):
    B, S, D = x.shape
    ada = _ada(c, ada_w, ada_b)
    fb = _shifted_bias(rel_bias)
    for layer in range(DEPTH):
        sh_a, sc_a, g_a, sh_f, sc_f, g_f = [ada[layer, :, i * D:(i + 1) * D] for i in range(6)]
        if layer == 0:
            o = _nsa_layer_attention(x, sh_a, sc_a, nsa_w_in[0], cmp_k_w1[0], cmp_k_w2[0], cmp_v_w1[0],
                                     cmp_v_w2[0], cmp_k_pos[0], cmp_v_pos[0], fb)
            w_out = nsa_w_out[0]
        else:
            o = _swa_layer_attention(x, sh_a, sc_a, swa_w_q[0], shared_w_kv, swa_sinks[0], rel_bias, fb)
            w_out = swa_w_out[0]
        x1, h, gates = _oproj(o, x, w_out, g_a, ln_g[layer, 0], ln_b[layer, 0], sh_f, sc_f,
                              router_w, router_b)
        x = _moe(h, gates, moe_w_gate[layer], moe_w_up[layer], moe_w_down[layer], x1, g_f,
                 ln_g[layer, 1], ln_b[layer, 1])
    return x
```

```python
import functools
import math

import numpy as np
import jax
import jax.numpy as jnp
from jax import lax
from jax.experimental import pallas as pl
from jax.experimental.pallas import tpu as pltpu

F32 = jnp.float32
BF16 = jnp.bfloat16

D_MODEL = 1024
HEAD_DIM = 64
N_HEADS = 16
N_KV = 4
HPG = 4
Q_WIDTH = N_HEADS * HEAD_DIM
KV_WIDTH = N_KV * HEAD_DIM
CMP_LEN = 32
CMP_STRIDE = 16
CMP_HIDDEN = 256
SLC_LEN = 64
N_SELECT = 16
NSA_WINDOW = 512
SWA_WINDOW = 128
N_BUCKETS = 32
MAX_DISTANCE = 128
N_EXPERTS = 16
N_GROUPS = 4
EXPERTS_PER_GROUP = 4
D_EXPERT = 256
DEPTH = 2
ALPHA = (2.0 * DEPTH) ** 0.25
LN_EPS = 1e-5
NEG = -1e30
ATTN_SCALE = HEAD_DIM ** -0.5
LOG2E = math.log2(math.e)
Q_SCALE = ATTN_SCALE * LOG2E

NSA_QB = 256
NSA_QPS = 2
SWA_QB = 128
SWA_QPS = 8
AUG = 128
MASK_LANES = AUG - HEAD_DIM
V_ROWS = HEAD_DIM + 16
CMP_LEAD = 16
FAR_TK = 256
OPROJ_CHUNKS = 4
N_GATE = 3 * N_HEADS
VMEM_LIMIT = 56 * 1024 * 1024

_NT = (((1,), (1,)), ((), ()))


def _cparams(sem):
    return pltpu.CompilerParams(dimension_semantics=sem, vmem_limit_bytes=VMEM_LIMIT)


def _ada_kernel(c_ref, w_ref, b_ref, o_ref):
    c = c_ref[...]
    cond = c * jax.nn.sigmoid(c)
    o_ref[0] = jnp.dot(cond, w_ref[0], preferred_element_type=F32,
                       precision=lax.Precision.HIGHEST) + b_ref[0]


def _ada(c, ada_w, ada_b):
    B, D = c.shape
    n6 = ada_w.shape[-1]
    tn = 1536
    return pl.pallas_call(
        _ada_kernel,
        grid=(DEPTH, n6 // tn),
        in_specs=[pl.BlockSpec((B, D), lambda l, n: (0, 0)),
                  pl.BlockSpec((1, D, tn), lambda l, n: (l, 0, n)),
                  pl.BlockSpec((1, 1, tn), lambda l, n: (l, 0, n))],
        out_specs=pl.BlockSpec((1, B, tn), lambda l, n: (l, 0, n)),
        out_shape=jax.ShapeDtypeStruct((DEPTH, B, n6), F32),
        compiler_params=_cparams(("arbitrary", "arbitrary")),
        name="ada",
    )(c, ada_w, ada_b.reshape(DEPTH, 1, n6))


def _store_q_blocks(q_ref, qT, ts, qb):
    zeros = jnp.zeros((MASK_LANES, HPG * qb), BF16)
    for g in range(N_KV):
        for k in range(ts // qb):
            for j in range(HPG):
                r0 = (g * HPG + j) * HEAD_DIM
                q_ref[0, g, k, 0:HEAD_DIM, j * qb:(j + 1) * qb] = (
                    qT[r0:r0 + HEAD_DIM, k * qb:(k + 1) * qb].astype(BF16))
            q_ref[0, g, k, HEAD_DIM:AUG, :] = zeros


def _store_vt(vt_ref, vT, ts):
    ones = jnp.ones((V_ROWS - HEAD_DIM, ts), BF16)
    for g in range(N_KV):
        vt_ref[0, g, 0:HEAD_DIM, :] = vT[g * HEAD_DIM:(g + 1) * HEAD_DIM].astype(BF16)
        vt_ref[0, g, HEAD_DIM:V_ROWS, :] = ones


def _nsa_proj_kernel(x_ref, sh_ref, sc_ref, wt_ref, ws_ref, q_ref, gt_ref, ks_ref, vst_ref, kw_ref,
                     vwt_ref, ck_ref, cv_ref, cmp_scr, *, ts):
    s = pl.program_id(1)
    h = (x_ref[0] * (1.0 + sc_ref[0]) + sh_ref[0]).astype(BF16)
    yT = lax.dot_general(wt_ref[...], h, _NT, preferred_element_type=F32)
    y = jnp.dot(h, ws_ref[...], preferred_element_type=F32)
    _store_q_blocks(q_ref, yT[:Q_WIDTH], ts, NSA_QB)
    _store_vt(vst_ref, yT[Q_WIDTH:Q_WIDTH + KV_WIDTH], ts)
    _store_vt(vwt_ref, yT[Q_WIDTH + KV_WIDTH:Q_WIDTH + 2 * KV_WIDTH], ts)
    gates = jax.nn.sigmoid(yT[Q_WIDTH + 2 * KV_WIDTH:])
    for k in range(ts // NSA_QB):
        gt_ref[0, k] = gates[:, k * NSA_QB:(k + 1) * NSA_QB]
    lane = lax.broadcasted_iota(jnp.int32, (ts, AUG), 1)
    tok = s * ts + lax.broadcasted_iota(jnp.int32, (ts, AUG), 0)
    onehot = (lane - HEAD_DIM) == tok // SLC_LEN
    for g in range(N_KV):
        ks_ref[0, g] = jnp.where(onehot, 1.0, y[:, g * AUG:(g + 1) * AUG]).astype(BF16)
        kw_ref[0, g] = y[:, (N_KV + g) * AUG:(N_KV + g + 1) * AUG].astype(BF16)
    base = 2 * N_KV * AUG
    for c in range(2 * KV_WIDTH // 128):
        cmp_scr[c] = y[:, base + c * 128:base + (c + 1) * 128]
    for c, dst in enumerate([ck_ref, ck_ref, cv_ref, cv_ref]):
        for r in range(CMP_STRIDE):
            piece = cmp_scr[c, pl.ds(r, ts // CMP_STRIDE, stride=CMP_STRIDE), :]
            for half in range(2):
                g = 2 * (c % 2) + half
                dst[0, g, :, r * HEAD_DIM:(r + 1) * HEAD_DIM] = (
                    piece[:, half * HEAD_DIM:(half + 1) * HEAD_DIM].astype(BF16))


def _pad_heads(w):
    D = w.shape[0]
    w = w.reshape(D, N_KV, HEAD_DIM)
    return jnp.pad(w, ((0, 0), (0, 0), (0, MASK_LANES))).reshape(D, N_KV * AUG)


def _nsa_proj(x, shift, scale, w_in, ts=512):
    B, S, D = x.shape
    nqb, lq = S // NSA_QB, HPG * NSA_QB
    cols = lambda i: w_in[:, Q_WIDTH + i * KV_WIDTH:Q_WIDTH + (i + 1) * KV_WIDTH]
    w_kc, w_vc, w_ksl, w_vsl, w_kw, w_vw = [cols(i) for i in range(6)]
    wt = jnp.concatenate([w_in[:, :Q_WIDTH] * Q_SCALE, w_vsl, w_vw,
                          w_in[:, Q_WIDTH + 6 * KV_WIDTH:]], axis=1).T.astype(BF16)
    ws = jnp.concatenate([_pad_heads(w_ksl), _pad_heads(w_kw), w_kc, w_vc], axis=1).astype(BF16)
    per_b = pl.BlockSpec((1, 1, D), lambda b, s: (b, 0, 0))
    kspec = pl.BlockSpec((1, N_KV, ts, AUG), lambda b, s: (b, 0, s, 0))
    vspec = pl.BlockSpec((1, N_KV, V_ROWS, ts), lambda b, s: (b, 0, 0, s))
    k_shape = jax.ShapeDtypeStruct((B, N_KV, S, AUG), BF16)
    v_shape = jax.ShapeDtypeStruct((B, N_KV, V_ROWS, S), BF16)
    chunk = CMP_STRIDE * HEAD_DIM
    cspec = pl.BlockSpec((1, N_KV, ts // CMP_STRIDE, chunk), lambda b, s: (b, 0, s, 0))
    c_shape = jax.ShapeDtypeStruct((B, N_KV, S // CMP_STRIDE, chunk), BF16)
    return pl.pallas_call(
        functools.partial(_nsa_proj_kernel, ts=ts),
        grid=(B, S // ts),
        in_specs=[pl.BlockSpec((1, ts, D), lambda b, s: (b, s, 0)), per_b, per_b,
                  pl.BlockSpec(wt.shape, lambda b, s: (0, 0)), pl.BlockSpec(ws.shape, lambda b, s: (0, 0))],
        out_specs=[pl.BlockSpec((1, N_KV, ts // NSA_QB, AUG, lq), lambda b, s: (b, 0, s, 0, 0)),
                   pl.BlockSpec((1, ts // NSA_QB, N_GATE, NSA_QB), lambda b, s: (b, s, 0, 0)),
                   kspec, vspec, kspec, vspec,
                   cspec, cspec],
        out_shape=[jax.ShapeDtypeStruct((B, N_KV, nqb, AUG, lq), BF16),
                   jax.ShapeDtypeStruct((B, nqb, N_GATE, NSA_QB), F32),
                   k_shape, v_shape, k_shape, v_shape,
                   c_shape, c_shape],
        scratch_shapes=[pltpu.VMEM((2 * KV_WIDTH // 128, ts, 128), F32)],
        compiler_params=_cparams(("arbitrary", "arbitrary")),
        name="nsa_proj",
    )(x, shift.reshape(B, 1, D), scale.reshape(B, 1, D), wt, ws)


def _swa_proj_kernel(x_ref, sh_ref, sc_ref, wq_ref, wk_ref, wv_ref, q_ref, k_ref, vt_ref, *, ts):
    x = x_ref[0]
    h = (x * (1.0 + sc_ref[0]) + sh_ref[0]).astype(BF16)
    xb = x.astype(BF16)
    _store_q_blocks(q_ref, lax.dot_general(wq_ref[...], h, _NT, preferred_element_type=F32), ts, SWA_QB)
    _store_vt(vt_ref, lax.dot_general(wv_ref[...], xb, _NT, preferred_element_type=F32), ts)
    k = jnp.dot(xb, wk_ref[...], preferred_element_type=F32)
    for g in range(N_KV):
        k_ref[0, g] = k[:, g * AUG:(g + 1) * AUG].astype(BF16)


def _swa_proj(x, shift, scale, w_q, w_kv, ts=512):
    B, S, D = x.shape
    lq = HPG * SWA_QB
    wq = (w_q * Q_SCALE).T.astype(BF16)
    wk = _pad_heads(w_kv[:, :KV_WIDTH]).astype(BF16)
    wv = w_kv[:, KV_WIDTH:].T.astype(BF16)
    per_b = pl.BlockSpec((1, 1, D), lambda b, s: (b, 0, 0))
    full = lambda a: pl.BlockSpec(a.shape, lambda b, s: (0, 0))
    return pl.pallas_call(
        functools.partial(_swa_proj_kernel, ts=ts),
        grid=(B, S // ts),
        in_specs=[pl.BlockSpec((1, ts, D), lambda b, s: (b, s, 0)), per_b, per_b, full(wq), full(wk), full(wv)],
        out_specs=[pl.BlockSpec((1, N_KV, ts // SWA_QB, AUG, lq), lambda b, s: (b, 0, s, 0, 0)),
                   pl.BlockSpec((1, N_KV, ts, AUG), lambda b, s: (b, 0, s, 0)),
                   pl.BlockSpec((1, N_KV, V_ROWS, ts), lambda b, s: (b, 0, 0, s))],
        out_shape=[jax.ShapeDtypeStruct((B, N_KV, S // SWA_QB, AUG, lq), BF16),
                   jax.ShapeDtypeStruct((B, N_KV, S, AUG), BF16),
                   jax.ShapeDtypeStruct((B, N_KV, V_ROWS, S), BF16)],
        compiler_params=_cparams(("arbitrary", "arbitrary")),
        name="swa_proj",
    )(x, shift.reshape(B, 1, D), scale.reshape(B, 1, D), wq, wk, wv)


def _gelu_tanh(x):
    return 0.5 * x * (1.0 + jnp.tanh(math.sqrt(2.0 / math.pi) * (x + 0.044715 * (x * x * x))))


def _compress_kernel(ch_ref, w1_ref, w2_ref, pos_ref, o_ref, *, nc):
    half = CMP_STRIDE * HEAD_DIM
    ch = ch_ref[0]
    w1 = w1_ref[...]
    top = jnp.dot(ch, w1[:half], preferred_element_type=F32)
    bot = jnp.dot(ch, w1[half:], preferred_element_type=F32)
    posw = jnp.dot(pos_ref[...], w1, preferred_element_type=F32)[0:1]
    w2 = w2_ref[...]
    for g in range(N_KV):
        bot_g = pltpu.roll(bot[g * nc:(g + 1) * nc], nc - 1, 0)
        pre = top[g * nc:(g + 1) * nc] + bot_g + posw
        o_ref[0, g] = jnp.dot(_gelu_tanh(pre).astype(BF16), w2, preferred_element_type=F32)


def _compress(chunks, w1, w2, pos):
    B, gn, ck = chunks.shape
    nc = gn // N_KV
    pos8 = jnp.broadcast_to(pos.reshape(1, CMP_LEN * HEAD_DIM), (8, CMP_LEN * HEAD_DIM)).astype(BF16)
    return pl.pallas_call(
        functools.partial(_compress_kernel, nc=nc),
        grid=(B,),
        in_specs=[pl.BlockSpec((1, gn, ck), lambda b: (b, 0, 0)),
                  pl.BlockSpec(w1.shape, lambda b: (0, 0)),
                  pl.BlockSpec(w2.shape, lambda b: (0, 0)),
                  pl.BlockSpec(pos8.shape, lambda b: (0, 0))],
        out_specs=pl.BlockSpec((1, N_KV, nc, HEAD_DIM), lambda b: (b, 0, 0, 0)),
        out_shape=jax.ShapeDtypeStruct((B, N_KV, nc, HEAD_DIM), F32),
        compiler_params=_cparams(("arbitrary",)),
        name="compress",
    )(chunks, w1.astype(BF16), w2.astype(BF16), pos8)


def _bucket_np(d):
    d = np.maximum(np.asarray(d, np.int64), 0)
    max_exact = N_BUCKETS // 2
    large = max_exact + (np.log(np.maximum(d, 1).astype(np.float32) / np.float32(max_exact))
                         / np.float32(math.log(MAX_DISTANCE / max_exact))
                         * np.float32(N_BUCKETS - max_exact)).astype(np.int32)
    large = np.minimum(large, N_BUCKETS - 1)
    return np.where(d < max_exact, d, large).astype(np.int32)


def _shifted_bias(rel_table):
    t = rel_table.astype(F32)
    return ((t[_bucket_np(np.arange(MAX_DISTANCE))] - t[N_BUCKETS - 1][None, :]) * LOG2E).T


def _to_group_lanes(m):
    H, K, Q = m.shape
    return m.reshape(N_KV, HPG, K, Q).transpose(0, 2, 1, 3).reshape(N_KV, K, HPG * Q)


def _bias_of_dist(fb, dist, window):
    ok = (dist >= 0) & (dist < window)
    idx = np.clip(dist, 0, MAX_DISTANCE - 1)
    return jnp.where(ok[None], jnp.where((dist >= MAX_DISTANCE)[None], 0.0, fb[:, idx]), NEG)


def _toeplitz_pair(fb, window):
    H, t = fb.shape[0], MAX_DISTANCE
    nv = 3 * t - 1
    v = _bias_of_dist(fb, np.arange(nv) - (t - 1), window)
    rows = 2 * t
    flat = jnp.tile(v, (1, rows + 1))[:, :rows * (nv + 1)]
    return flat.reshape(H, rows, nv + 1)[:, ::-1, :t]


def _near_band(fb, window, qb):
    H, t = fb.shape[0], MAX_DISTANCE
    nb = qb // t
    assert qb % t == 0 and (nb == 1 or window >= 2 * qb)
    pair = _toeplitz_pair(fb, window)
    blocks = {1: pair[:, :t], 0: pair[:, t:]}
    zero, masked = jnp.zeros((H, t, t), F32), jnp.full((H, t, t), NEG, F32)
    rows = []
    for a in range(2 * nb):
        row = [blocks.get(b + nb - a, zero if b + nb - a > 1 else masked) for b in range(nb)]
        rows.append(jnp.concatenate(row, axis=2))
    return _to_group_lanes(jnp.concatenate(rows, axis=1))


def _masked_rows(n, qb):
    return jnp.full((N_KV, n, HPG * qb), NEG, F32)


def _window_table(fb, qb):
    kk = np.arange(qb)[:, None]
    qi = np.arange(HPG * qb)[None, :] % qb
    edge = np.where(kk > qi, 0.0, NEG).astype(np.float32)
    edge = jnp.broadcast_to(jnp.asarray(edge), (N_KV, qb, HPG * qb))
    mid = jnp.zeros((N_KV, NSA_WINDOW - 2 * qb, HPG * qb), F32)
    return jnp.concatenate([edge, mid, _near_band(fb, NSA_WINDOW, qb), _masked_rows(NSA_WINDOW, qb)], axis=1)


def _cmp_band_rows(qb):
    return CMP_LEAD + qb // CMP_STRIDE


def _cmp_band(fb, nc, qb):
    m = np.arange(_cmp_band_rows(qb))[:, None]
    i = np.arange(qb)[None, :]
    dist = i + CMP_STRIDE * CMP_LEAD - CMP_STRIDE * m - (CMP_LEN - 1)
    band = _to_group_lanes(_bias_of_dist(fb, dist, 1 << 30))
    return jnp.concatenate([jnp.zeros((N_KV, nc, HPG * qb), F32), band, _masked_rows(nc, qb)], axis=1)


def _to_token_rows(oT, qb):
    pair = lambda a: jnp.concatenate([oT[:, a * qb:(a + 1) * qb], oT[:, (a + 1) * qb:(a + 2) * qb]], axis=0)
    return jnp.concatenate([pair(0).T, pair(2).T], axis=1)


def _nsa_kernel(q_ref, gt_ref, kc_ref, vct_ref, ks_ref, vst_ref, kw_ref, vwt_ref, ovl_ref, cb_ref,
                wt_ref, *rest, nc, step):
    o_ref, acc_ref, ow_ref, sa_ref, sb_ref, pa_ref, pb_ref = rest[-7:]
    QB, LQ, C = NSA_QB, HPG * NSA_QB, NSA_QPS
    g = pl.program_id(0)
    chains = range(C)
    ncr = kc_ref.shape[2]
    qbs = [step * C + c for c in chains]
    s0s = [qb * QB for qb in qbs]
    nblk = (step + 1) * C * QB // SLC_LEN
    qTs = [q_ref[0, 0, c] for c in chains]

    def gate(c, branch):
        rows = [gt_ref[0, c, pl.ds(branch * N_HEADS + g * HPG + j, 1), :] for j in range(HPG)]
        return jnp.concatenate(rows, axis=1)

    scs, sws, w0s = [], [], []
    for c in chains:
        cstart = nc + CMP_LEAD - (QB // CMP_STRIDE) * qbs[c]
        scs.append(jnp.dot(kc_ref[0, 0], qTs[c], preferred_element_type=F32)
                   + cb_ref[0, pl.ds(cstart, ncr), :])
    win_keys = NSA_WINDOW + QB
    for c in chains:
        w0 = max(s0s[c] - NSA_WINDOW, 0)
        t0 = w0 - (s0s[c] - NSA_WINDOW)
        w0s.append(w0)
        sws.append(jnp.dot(kw_ref[0, 0, pl.ds(w0, win_keys), :], qTs[c], preferred_element_type=F32)
                   + wt_ref[0, pl.ds(t0, win_keys), :])

    p_cs = []
    for c in chains:
        m_c = jnp.max(scs[c], axis=0, keepdims=True)
        e_c = jnp.exp2(scs[c] - m_c)
        l_c = jnp.sum(e_c, axis=0, keepdims=True)
        p_cs.append(e_c * jnp.where(m_c > 0.1 * NEG, 1.0 / l_c, 0.0))
    o_cmps, imps = [], []
    for c in chains:
        o_cmps.append(jnp.dot(vct_ref[0, 0], p_cs[c].astype(BF16), preferred_element_type=F32)[:HEAD_DIM])
        p_sum = functools.reduce(jnp.add, [p_cs[c][:, j * QB:(j + 1) * QB] for j in range(HPG)])
        p_hi = p_sum.astype(BF16)
        rest1 = p_sum - p_hi.astype(F32)
        p_mid = rest1.astype(BF16)
        p_lo = (rest1 - p_mid.astype(F32)).astype(BF16)
        imp = functools.reduce(jnp.add, [jnp.dot(ovl_ref[...], part, preferred_element_type=F32)
                                         for part in (p_hi, p_mid, p_lo)])
        imps.append(imp[HEAD_DIM:HEAD_DIM + nblk])

    p_ws = []
    for c in chains:
        m_w = jnp.max(sws[c], axis=0, keepdims=True)
        p_ws.append(jnp.exp2(sws[c] - m_w).astype(BF16))
    acc_ws = [jnp.dot(vwt_ref[0, 0, :, pl.ds(w0s[c], win_keys)], p_ws[c], preferred_element_type=F32)
              for c in chains]

    blk = lax.broadcasted_iota(jnp.int32, (nblk, QB), 0)
    sub = lax.broadcasted_iota(jnp.int32, (8, QB), 0)
    unseen = [jnp.full((MASK_LANES - nblk, LQ), NEG, BF16)] if nblk < MASK_LANES else []
    q_nears, q_fars = [], []
    for c in chains:
        tq = s0s[c] + lax.broadcasted_iota(jnp.int32, (nblk, QB), 1)
        cur = tq // SLC_LEN
        forced = (blk == 0) | (blk == cur) | (blk == cur - 1)
        future = blk > cur
        val = jnp.where(forced, 1e9, jnp.where(future, -1e9, imps[c]))
        vals = [val[8 * a:8 * a + 8] for a in range(nblk // 8)]
        cnts = [jnp.zeros((8, QB), F32) for _ in range(nblk // 8)]
        for jp in range(nblk):
            ap, r = divmod(jp, 8)
            rv = jnp.broadcast_to(vals[ap][r:r + 1], (8, QB))
            for a in range(nblk // 8):
                gt = jnp.where(rv > vals[a], 1.0, 0.0)
                ge = jnp.where(rv >= vals[a], 1.0, 0.0)
                if a < ap:
                    inc = gt
                elif a > ap:
                    inc = ge
                else:
                    inc = jnp.where(sub > r, ge, gt)
                cnts[a] = cnts[a] + inc
        cnt = jnp.concatenate(cnts, axis=0)
        allowed = (cnt < float(N_SELECT)) & (blk <= cur)
        near_blk = blk >= (s0s[c] - QB) // SLC_LEN
        neg_near = jnp.where(allowed, 0.0, NEG).astype(BF16)
        neg_far = jnp.where(allowed & jnp.logical_not(near_blk), 0.0, NEG).astype(BF16)
        q_nears.append(jnp.concatenate([qTs[c][:HEAD_DIM], jnp.tile(neg_near, (1, HPG))] + unseen, axis=0))
        q_fars.append(jnp.concatenate([qTs[c][:HEAD_DIM], jnp.tile(neg_far, (1, HPG))] + unseen, axis=0))
    for c in chains:
        ow_ref[c] = gate(c, 2) * (acc_ws[c][:HEAD_DIM] / acc_ws[c][HEAD_DIM:HEAD_DIM + 1])

    s_bufs, p_bufs = (sa_ref, sb_ref), (pa_ref, pb_ref)
    n_far = [max(s0s[c] - QB, 0) // FAR_TK for c in chains]

    def qk(c, t):
        return jnp.dot(ks_ref[0, 0, pl.ds(t * FAR_TK, FAR_TK), :], q_fars[c], preferred_element_type=F32)

    def softmax_update(s, m_run):
        m_new = jnp.maximum(m_run, jnp.max(s, axis=0, keepdims=True))
        return jnp.exp2(s - m_new).astype(BF16), jnp.exp2(m_run - m_new), m_new

    n0s, s_nears = [], []
    for c in chains:
        n0 = max(s0s[c] - QB, 0)
        b0 = NSA_WINDOW - QB + n0 - (s0s[c] - QB)
        n0s.append(n0)
        s_nears.append(jnp.dot(ks_ref[0, 0, pl.ds(n0, 2 * QB), :], q_nears[c], preferred_element_type=F32)
                       + wt_ref[0, pl.ds(b0, 2 * QB), :])
    for c in chains:
        for t in range(min(2, n_far[c])):
            s_bufs[t][c] = qk(c, t)
    m_runs = [jnp.max(s, axis=0, keepdims=True) for s in s_nears]
    p_nears = [jnp.exp2(s - m).astype(BF16) for s, m in zip(s_nears, m_runs)]
    for c in chains:
        acc_ref[c] = jnp.dot(vst_ref[0, 0, :, pl.ds(n0s[c], 2 * QB)], p_nears[c],
                             preferred_element_type=F32)
    for t in range(max(n_far)):
        for c in chains:
            if t < n_far[c]:
                s_buf, p_buf = s_bufs[t % 2], p_bufs[t % 2]
                p, alpha, m_runs[c] = softmax_update(s_buf[c], m_runs[c])
                p_buf[c] = p
                if t + 2 < n_far[c]:
                    s_buf[c] = qk(c, t + 2)
                pv = jnp.dot(vst_ref[0, 0, :, pl.ds(t * FAR_TK, FAR_TK)], p_buf[c], preferred_element_type=F32)
                acc_ref[c] = acc_ref[c] * alpha + pv

    for c in chains:
        a_c = acc_ref[c]
        o_sel = a_c[:HEAD_DIM] / a_c[HEAD_DIM:HEAD_DIM + 1]
        o = gate(c, 0) * o_cmps[c] + gate(c, 1) * o_sel + ow_ref[c]
        o_ref[0, c * QB:(c + 1) * QB, :] = _to_token_rows(o, QB).astype(o_ref.dtype)


def _nsa_attention(qT, gatesT, kc_aug, vcT, ks_aug, vsT, kw_aug, vwT, ovl, cband, wtable):
    B, G, nqb = qT.shape[:3]
    QB, LQ, C = NSA_QB, HPG * NSA_QB, NSA_QPS
    nc, S = kc_aug.shape[2], ks_aug.shape[2]
    per_g = lambda a: pl.BlockSpec((1,) + a.shape[1:], lambda g, b: (g,) + (0,) * (a.ndim - 1))
    rows = lambda a, n: pl.BlockSpec((1, 1, n, a.shape[3]), lambda g, b: (b, g, 0, 0))
    cols = lambda a, n: pl.BlockSpec((1, 1, a.shape[2], n), lambda g, b: (b, g, 0, 0))
    out = None
    for step in range(nqb // C):
        seen = (step + 1) * C * QB
        held = min(S, max(seen, 2 * NSA_WINDOW))
        ncr = min(nc, -(-(seen // CMP_STRIDE) // 128) * 128)
        ovl_r = ovl[:, :ncr]
        in_specs = [pl.BlockSpec((1, 1, C, AUG, LQ), lambda g, b, s=step: (b, g, s, 0, 0)),
                    pl.BlockSpec((1, C, N_GATE, QB), lambda g, b, s=step: (b, s, 0, 0)),
                    rows(kc_aug, ncr), cols(vcT, ncr), rows(ks_aug, held), cols(vsT, held),
                    rows(kw_aug, held), cols(vwT, held), pl.BlockSpec(ovl_r.shape, lambda g, b: (0, 0)),
                    per_g(cband), per_g(wtable)]
        args = [qT, gatesT, kc_aug, vcT, ks_aug, vsT, kw_aug, vwT, ovl_r, cband, wtable]
        aliases = {}
        if out is not None:
            in_specs.append(pl.BlockSpec(memory_space=pl.ANY))
            args.append(out)
            aliases = {len(args) - 1: 0}
        out = pl.pallas_call(
            functools.partial(_nsa_kernel, nc=nc, step=step),
            grid=(G, B),
            in_specs=in_specs,
            out_specs=pl.BlockSpec((1, C * QB, HPG * HEAD_DIM), lambda g, b, s=step: (b, s, g)),
            out_shape=jax.ShapeDtypeStruct((B, nqb * QB, Q_WIDTH), BF16),
            scratch_shapes=[pltpu.VMEM((C, V_ROWS, LQ), F32), pltpu.VMEM((C, HEAD_DIM, LQ), F32),
                            pltpu.VMEM((C, FAR_TK, LQ), F32), pltpu.VMEM((C, FAR_TK, LQ), F32),
                            pltpu.VMEM((C, FAR_TK, LQ), BF16), pltpu.VMEM((C, FAR_TK, LQ), BF16)],
            input_output_aliases=aliases,
            compiler_params=_cparams(("arbitrary", "arbitrary")),
            name=f"nsa_attention_{step}",
        )(*args)
    return out


def _swa_kernel(q_ref, k_ref, vt_ref, band_ref, sink_ref, o_ref):
    QB = SWA_QB
    sink = sink_ref[0, 0:1]
    starts, scores = [], []
    for i in range(SWA_QPS):
        s0 = (pl.program_id(2) * SWA_QPS + i) * QB
        n0 = pl.multiple_of(jnp.maximum(s0 - QB, 0), QB)
        b0 = pl.multiple_of(n0 - (s0 - QB), QB)
        starts.append(n0)
        scores.append(jnp.dot(k_ref[0, 0, pl.ds(n0, 2 * QB), :], q_ref[0, 0, i], preferred_element_type=F32)
                      + band_ref[0, pl.ds(b0, 2 * QB), :])
    maxes = [jnp.maximum(jnp.max(s, axis=0, keepdims=True), sink) for s in scores]
    probs = [jnp.exp2(s - m).astype(BF16) for s, m in zip(scores, maxes)]
    accs = [jnp.dot(vt_ref[0, 0, :, pl.ds(n0, 2 * QB)], p, preferred_element_type=F32)
            for n0, p in zip(starts, probs)]
    for i, (acc, m) in enumerate(zip(accs, maxes)):
        denom = acc[HEAD_DIM:HEAD_DIM + 1] + jnp.exp2(sink - m)
        o_ref[0, i * QB:(i + 1) * QB, :] = _to_token_rows(acc[:HEAD_DIM] / denom, QB).astype(o_ref.dtype)


def _swa_attention(qT, k_aug, vT, band, sink):
    B, G, nqb = qT.shape[:3]
    QB, LQ = SWA_QB, HPG * SWA_QB
    per_bg = lambda a: pl.BlockSpec((1, 1) + a.shape[2:], lambda b, g, q: (b, g) + (0,) * (a.ndim - 2))
    per_g = lambda a: pl.BlockSpec((1,) + a.shape[1:], lambda b, g, q: (g,) + (0,) * (a.ndim - 1))
    return pl.pallas_call(
        _swa_kernel,
        grid=(B, G, nqb // SWA_QPS),
        in_specs=[pl.BlockSpec((1, 1, SWA_QPS, AUG, LQ), lambda b, g, q: (b, g, q, 0, 0)),
                  per_bg(k_aug), per_bg(vT), per_g(band), per_g(sink)],
        out_specs=pl.BlockSpec((1, SWA_QPS * QB, HPG * HEAD_DIM), lambda b, g, q: (b, q, g)),
        out_shape=jax.ShapeDtypeStruct((B, nqb * QB, Q_WIDTH), BF16),
        compiler_params=_cparams(("arbitrary", "arbitrary", "arbitrary")),
        name="swa_attention",
    )(qT, k_aug, vT, band, sink)


def _layer_norm(y, g, b):
    mu = jnp.mean(y, axis=-1, keepdims=True)
    yc = y - mu
    var = jnp.mean(yc * yc, axis=-1, keepdims=True)
    return yc * lax.rsqrt(var + LN_EPS) * g + b


def _top2_of4(a, b, c, d):
    hi1, lo1 = jnp.maximum(a, b), jnp.minimum(a, b)
    hi2, lo2 = jnp.maximum(c, d), jnp.minimum(c, d)
    return jnp.maximum(hi1, hi2) + jnp.maximum(jnp.minimum(hi1, hi2), jnp.maximum(lo1, lo2))


def _route(s, sb):
    n = EXPERTS_PER_GROUP
    score = [_top2_of4(*sb[n * r:n * r + n]) for r in range(N_GROUPS)]
    best = functools.reduce(jnp.maximum, score)
    taken = jnp.zeros_like(best) > 1.0
    in_grp = []
    for r in range(N_GROUPS):
        pick = (score[r] == best) & jnp.logical_not(taken)
        in_grp.append(pick)
        taken = taken | pick
    gates = []
    for e in range(N_EXPERTS):
        r = e // n
        ahead = jnp.zeros_like(best)
        for f in range(n * r, n * r + n):
            if f != e:
                beats = (sb[f] >= sb[e]) if f < e else (sb[f] > sb[e])
                ahead = ahead + jnp.where(beats, 1.0, 0.0)
        gates.append(jnp.where(in_grp[r] & (ahead < 2.0), s[e], 0.0))
    total = functools.reduce(jnp.add, gates)
    return [g / total for g in gates]


def _oproj_kernel(o_ref, x_ref, w_ref, ga_ref, lg_ref, lb_ref, shf_ref, scf_ref, rw_ref, rb_ref,
                  x1_ref, h_ref, gate_ref):
    ts = o_ref.shape[1]
    rows = [slice(k * ts // OPROJ_CHUNKS, (k + 1) * ts // OPROJ_CHUNKS) for k in range(OPROJ_CHUNKS)]
    mixes = [jnp.dot(o_ref[0, r, :], w_ref[...], preferred_element_type=F32) for r in rows]
    hbs = []
    for r, mix in zip(rows, mixes):
        x1 = _layer_norm(ALPHA * x_ref[0, r, :] + (1.0 + ga_ref[0]) * mix, lg_ref[...], lb_ref[...])
        x1_ref[0, r, :] = x1
        hb = (x1 * (1.0 + scf_ref[0]) + shf_ref[0]).astype(BF16)
        h_ref[0, r, :] = hb
        hbs.append(hb)
    logit = jnp.concatenate([lax.dot_general(rw_ref[...], hb, _NT, preferred_element_type=F32) for hb in hbs],
                            axis=1)
    aff = jax.nn.sigmoid(logit)
    biased = aff + rb_ref[...][:, 0:1]
    s = [aff[e:e + 1] for e in range(N_EXPERTS)]
    sb = [biased[e:e + 1] for e in range(N_EXPERTS)]
    gate_ref[0] = jnp.concatenate(_route(s, sb), axis=0)


def _oproj(o, x, w_out, g_a, ln_g, ln_b, sh_f, sc_f, router_w, router_b, ts=512):
    B, S, D = x.shape
    row = lambda a: a.reshape(1, D)
    per_b = pl.BlockSpec((1, 1, D), lambda b, s: (b, 0, 0))
    full = lambda a: pl.BlockSpec(a.shape, lambda b, s: (0,) * a.ndim)
    rwT = router_w.T.astype(BF16)
    rb = jnp.broadcast_to(router_b.reshape(N_EXPERTS, 1), (N_EXPERTS, 128))
    args = (o, x, w_out.astype(BF16), g_a.reshape(B, 1, D), row(ln_g), row(ln_b),
            sh_f.reshape(B, 1, D), sc_f.reshape(B, 1, D), rwT, rb)
    tile = pl.BlockSpec((1, ts, D), lambda b, s: (b, s, 0))
    return pl.pallas_call(
        _oproj_kernel,
        grid=(B, S // ts),
        in_specs=[tile, tile, full(args[2]), per_b, full(args[4]), full(args[5]), per_b, per_b,
                  full(rwT), full(rb)],
        out_specs=[tile, tile, pl.BlockSpec((1, N_EXPERTS, ts), lambda b, s: (b, 0, s))],
        out_shape=[jax.ShapeDtypeStruct((B, S, D), F32), jax.ShapeDtypeStruct((B, S, D), BF16),
                   jax.ShapeDtypeStruct((B, N_EXPERTS, S), F32)],
        compiler_params=_cparams(("arbitrary", "arbitrary")),
        name="oproj_ln_router",
    )(*args)


def _moe_kernel(h_ref, gate_ref, wg_ref, wu_ref, wd_ref, x_ref, gf_ref, lg_ref, lb_ref, o_ref, acc_ref):
    n = EXPERTS_PER_GROUP
    h = h_ref[...]
    gates = gate_ref[...]
    lane = lax.broadcasted_iota(jnp.int32, gates.shape, 1)
    for r in range(N_GROUPS):
        he = []
        for k in range(n):
            e = r * n + k
            a = jnp.dot(h, wg_ref[e], preferred_element_type=F32)
            u = jnp.dot(h, wu_ref[e], preferred_element_type=F32)
            gcol = jnp.sum(jnp.where(lane == e, gates, 0.0), axis=1, keepdims=True)
            he.append((a * jax.nn.sigmoid(a) * u * gcol).astype(BF16))
        wd = wd_ref[r * n:(r + 1) * n].reshape(n * D_EXPERT, wd_ref.shape[-1])
        ffn = jnp.dot(jnp.concatenate(he, axis=1), wd, preferred_element_type=F32)
        if r == 0:
            acc_ref[...] = ffn
        else:
            acc_ref[...] += ffn
    y = ALPHA * x_ref[...] + (1.0 + gf_ref[0]) * acc_ref[...]
    o_ref[...] = _layer_norm(y, lg_ref[...], lb_ref[...])


def _moe(h, gates, w_gate, w_up, w_down, x, g_f, ln_g, ln_b, ts=512):
    B, S, D = x.shape
    T = B * S
    gl = jnp.pad(gates.transpose(0, 2, 1).reshape(T, N_EXPERTS), ((0, 0), (0, 128 - N_EXPERTS)))
    tile = pl.BlockSpec((ts, D), lambda t: (t, 0))
    spt = S // ts
    resident = lambda a: pl.BlockSpec(a.shape, lambda t: (0,) * a.ndim, pipeline_mode=pl.Buffered(1))
    wg, wu, wd = w_gate.astype(BF16), w_up.astype(BF16), w_down.astype(BF16)
    out = pl.pallas_call(
        _moe_kernel,
        grid=(T // ts,),
        in_specs=[tile, pl.BlockSpec((ts, 128), lambda t: (t, 0)), resident(wg), resident(wu), resident(wd),
                  tile, pl.BlockSpec((1, 1, D), lambda t: (t // spt, 0, 0)),
                  pl.BlockSpec((1, D), lambda t: (0, 0)), pl.BlockSpec((1, D), lambda t: (0, 0))],
        out_specs=tile,
        out_shape=jax.ShapeDtypeStruct((T, D), F32),
        scratch_shapes=[pltpu.VMEM((ts, D), F32)],
        compiler_params=_cparams(("arbitrary",)),
        name="moe_ln",
    )(h.reshape(T, D), gl, wg, wu, wd, x.reshape(T, D), g_f.reshape(B, 1, D), ln_g.reshape(1, D),
      ln_b.reshape(1, D))
    return out.reshape(B, S, D)


def _values_t(v):
    B, G, n, _ = v.shape
    return jnp.concatenate([v.transpose(0, 1, 3, 2), jnp.ones((B, G, V_ROWS - HEAD_DIM, n), v.dtype)], axis=2)


def _nsa_layer_attention(x, sh, sc, w_in, k_w1, k_w2, v_w1, v_w2, k_pos, v_pos, fb):
    B, S, _ = x.shape
    nc, n_slc = S // CMP_STRIDE, S // SLC_LEN
    qT, gatesT, ks_aug, vsT, kw_aug, vwT, ck, cv = _nsa_proj(x, sh, sc, w_in)
    merge = lambda t: t.reshape(B, N_KV * nc, CMP_STRIDE * HEAD_DIM)
    k_cmp = _compress(merge(ck), k_w1, k_w2, k_pos)
    v_cmp = _compress(merge(cv), v_w1, v_w2, v_pos)
    kc_aug = jnp.pad(k_cmp.astype(BF16), ((0, 0), (0, 0), (0, 0), (0, MASK_LANES)))
    vcT = _values_t(v_cmp.astype(BF16))

    c_start = np.arange(nc)[None, :] * CMP_STRIDE
    s_start = np.arange(MASK_LANES)[:, None] * SLC_LEN
    ovl = ((c_start < s_start + SLC_LEN) & (c_start + CMP_LEN > s_start)
           & (np.arange(nc)[None, :] < nc - 1) & (np.arange(MASK_LANES)[:, None] < n_slc))
    ovl = np.concatenate([np.zeros((HEAD_DIM, nc)), ovl.astype(np.float64)], axis=0).astype(np.float32)

    return _nsa_attention(qT, gatesT, kc_aug, vcT, ks_aug, vsT, kw_aug, vwT, jnp.asarray(ovl, BF16),
                          _cmp_band(fb, nc, NSA_QB), _window_table(fb, NSA_QB))


def _swa_layer_attention(x, sh, sc, w_q, w_kv, sinks, rel_bias, fb):
    qT, k_aug, vT = _swa_proj(x, sh, sc, w_q, w_kv)
    sink = (sinks.astype(F32) - rel_bias.astype(F32)[N_BUCKETS - 1]) * LOG2E
    sink = jnp.broadcast_to(sink.reshape(N_KV, 1, HPG, 1), (N_KV, 8, HPG, SWA_QB))
    sink = sink.reshape(N_KV, 8, HPG * SWA_QB)
    band = jnp.concatenate([_near_band(fb, SWA_WINDOW, SWA_QB), _masked_rows(SWA_QB, SWA_QB)], axis=1)
    return _swa_attention(qT, k_aug, vT, band, sink)


def kernel(x, c, nsa_w_in, cmp_k_w1, cmp_k_w2, cmp_v_w1, cmp_v_w2, cmp_k_pos, cmp_v_pos, nsa_w_out,
           swa_w_q, swa_sinks, swa_w_out, shared_w_kv, rel_bias, router_w, router_b, moe_w_gate,
           moe_w_up, moe_w_down, ada_w, ada_b, ln_g, ln_b):
    B, S, D = x.shape
    ada = _ada(c, ada_w, ada_b)
    fb = _shifted_bias(rel_bias)
    for layer in range(DEPTH):
        sh_a, sc_a, g_a, sh_f, sc_f, g_f = [ada[layer, :, i * D:(i + 1) * D] for i in range(6)]
        if layer == 0:
            o = _nsa_layer_attention(x, sh_a, sc_a, nsa_w_in[0], cmp_k_w1[0], cmp_k_w2[0], cmp_v_w1[0],
                                     cmp_v_w2[0], cmp_k_pos[0], cmp_v_pos[0], fb)
            w_out = nsa_w_out[0]
        else:
            o = _swa_layer_attention(x, sh_a, sc_a, swa_w_q[0], shared_w_kv, swa_sinks[0], rel_bias, fb)
            w_out = swa_w_out[0]
        x1, h, gates = _oproj(o, x, w_out, g_a, ln_g[layer, 0], ln_b[layer, 0], sh_f, sc_f,
                              router_w, router_b)
        x = _moe(h, gates, moe_w_gate[layer], moe_w_up[layer], moe_w_down[layer], x1, g_f,
                 ln_g[layer, 1], ln_b[layer, 1])
    return x
```

```python
import functools
import math

import numpy as np
import jax
import jax.numpy as jnp
from jax import lax
from jax.experimental import pallas as pl
from jax.experimental.pallas import tpu as pltpu

F32 = jnp.float32
BF16 = jnp.bfloat16

D_MODEL = 1024
HEAD_DIM = 64
N_HEADS = 16
N_KV = 4
HPG = 4
Q_WIDTH = N_HEADS * HEAD_DIM
KV_WIDTH = N_KV * HEAD_DIM
CMP_LEN = 32
CMP_STRIDE = 16
CMP_HIDDEN = 256
SLC_LEN = 64
N_SELECT = 16
NSA_WINDOW = 512
SWA_WINDOW = 128
N_BUCKETS = 32
MAX_DISTANCE = 128
N_EXPERTS = 16
N_GROUPS = 4
EXPERTS_PER_GROUP = 4
D_EXPERT = 256
DEPTH = 2
ALPHA = (2.0 * DEPTH) ** 0.25
LN_EPS = 1e-5
NEG = -1e30
ATTN_SCALE = HEAD_DIM ** -0.5
LOG2E = math.log2(math.e)
Q_SCALE = ATTN_SCALE * LOG2E

NSA_QB = 256
NSA_QPS = 2
WIN_QB = 128
SWA_QB = 128
SWA_QPS = 16
AUG = 128
MASK_LANES = AUG - HEAD_DIM
V_ROWS = HEAD_DIM + 16
CMP_LEAD = 16
FAR_TK = 256
OPROJ_CHUNKS = 4
N_GATE = 3 * N_HEADS
VMEM_LIMIT = 56 * 1024 * 1024

_NT = (((1,), (1,)), ((), ()))


def _cparams(sem):
    return pltpu.CompilerParams(dimension_semantics=sem, vmem_limit_bytes=VMEM_LIMIT)


def _ada_kernel(c_ref, w_ref, b_ref, o_ref):
    c = c_ref[...]
    cond = c * jax.nn.sigmoid(c)
    o_ref[0] = jnp.dot(cond, w_ref[0], preferred_element_type=F32,
                       precision=lax.Precision.HIGHEST) + b_ref[0]


def _ada(c, ada_w, ada_b):
    B, D = c.shape
    n6 = ada_w.shape[-1]
    tn = 1536
    return pl.pallas_call(
        _ada_kernel,
        grid=(DEPTH, n6 // tn),
        in_specs=[pl.BlockSpec((B, D), lambda l, n: (0, 0)),
                  pl.BlockSpec((1, D, tn), lambda l, n: (l, 0, n)),
                  pl.BlockSpec((1, 1, tn), lambda l, n: (l, 0, n))],
        out_specs=pl.BlockSpec((1, B, tn), lambda l, n: (l, 0, n)),
        out_shape=jax.ShapeDtypeStruct((DEPTH, B, n6), F32),
        compiler_params=_cparams(("arbitrary", "arbitrary")),
        name="ada",
    )(c, ada_w, ada_b.reshape(DEPTH, 1, n6))


def _store_q_blocks(q_ref, qT, ts, qb):
    zeros = jnp.zeros((MASK_LANES, HPG * qb), BF16)
    for g in range(N_KV):
        for k in range(ts // qb):
            for j in range(HPG):
                r0 = (g * HPG + j) * HEAD_DIM
                q_ref[0, g, k, 0:HEAD_DIM, j * qb:(j + 1) * qb] = (
                    qT[r0:r0 + HEAD_DIM, k * qb:(k + 1) * qb].astype(BF16))
            q_ref[0, g, k, HEAD_DIM:AUG, :] = zeros


def _store_vt(vt_ref, vT, ts):
    ones = jnp.ones((V_ROWS - HEAD_DIM, ts), BF16)
    for g in range(N_KV):
        vt_ref[0, g, 0:HEAD_DIM, :] = vT[g * HEAD_DIM:(g + 1) * HEAD_DIM].astype(BF16)
        vt_ref[0, g, HEAD_DIM:V_ROWS, :] = ones


def _nsa_proj_kernel(x_ref, sh_ref, sc_ref, wt_ref, ws_ref, q_ref, gt_ref, ks_ref, vst_ref, kw_ref,
                     vwt_ref, ck_ref, cv_ref, cmp_scr, *, ts):
    s = pl.program_id(1)
    h = (x_ref[0] * (1.0 + sc_ref[0]) + sh_ref[0]).astype(BF16)
    yT = lax.dot_general(wt_ref[...], h, _NT, preferred_element_type=F32)
    y = jnp.dot(h, ws_ref[...], preferred_element_type=F32)
    _store_q_blocks(q_ref, yT[:Q_WIDTH], ts, NSA_QB)
    _store_vt(vst_ref, yT[Q_WIDTH:Q_WIDTH + KV_WIDTH], ts)
    _store_vt(vwt_ref, yT[Q_WIDTH + KV_WIDTH:Q_WIDTH + 2 * KV_WIDTH], ts)
    gates = jax.nn.sigmoid(yT[Q_WIDTH + 2 * KV_WIDTH:])
    for k in range(ts // NSA_QB):
        gt_ref[0, k] = gates[:, k * NSA_QB:(k + 1) * NSA_QB]
    lane = lax.broadcasted_iota(jnp.int32, (ts, AUG), 1)
    tok = s * ts + lax.broadcasted_iota(jnp.int32, (ts, AUG), 0)
    onehot = (lane - HEAD_DIM) == tok // SLC_LEN
    for g in range(N_KV):
        ks_ref[0, g] = jnp.where(onehot, 1.0, y[:, g * AUG:(g + 1) * AUG]).astype(BF16)
        kw_ref[0, g] = y[:, (N_KV + g) * AUG:(N_KV + g + 1) * AUG].astype(BF16)
    base = 2 * N_KV * AUG
    for c in range(2 * KV_WIDTH // 128):
        cmp_scr[c] = y[:, base + c * 128:base + (c + 1) * 128]
    for c, dst in enumerate([ck_ref, ck_ref, cv_ref, cv_ref]):
        for r in range(CMP_STRIDE):
            piece = cmp_scr[c, pl.ds(r, ts // CMP_STRIDE, stride=CMP_STRIDE), :]
            for half in range(2):
                g = 2 * (c % 2) + half
                dst[0, g, :, r * HEAD_DIM:(r + 1) * HEAD_DIM] = (
                    piece[:, half * HEAD_DIM:(half + 1) * HEAD_DIM].astype(BF16))


def _pad_heads(w):
    D = w.shape[0]
    w = w.reshape(D, N_KV, HEAD_DIM)
    return jnp.pad(w, ((0, 0), (0, 0), (0, MASK_LANES))).reshape(D, N_KV * AUG)


def _nsa_proj(x, shift, scale, w_in, ts=512):
    B, S, D = x.shape
    nqb, lq = S // NSA_QB, HPG * NSA_QB
    cols = lambda i: w_in[:, Q_WIDTH + i * KV_WIDTH:Q_WIDTH + (i + 1) * KV_WIDTH]
    w_kc, w_vc, w_ksl, w_vsl, w_kw, w_vw = [cols(i) for i in range(6)]
    wt = jnp.concatenate([w_in[:, :Q_WIDTH] * Q_SCALE, w_vsl, w_vw,
                          w_in[:, Q_WIDTH + 6 * KV_WIDTH:]], axis=1).T.astype(BF16)
    ws = jnp.concatenate([_pad_heads(w_ksl), _pad_heads(w_kw), w_kc, w_vc], axis=1).astype(BF16)
    per_b = pl.BlockSpec((1, 1, D), lambda b, s: (b, 0, 0))
    kspec = pl.BlockSpec((1, N_KV, ts, AUG), lambda b, s: (b, 0, s, 0))
    vspec = pl.BlockSpec((1, N_KV, V_ROWS, ts), lambda b, s: (b, 0, 0, s))
    k_shape = jax.ShapeDtypeStruct((B, N_KV, S, AUG), BF16)
    v_shape = jax.ShapeDtypeStruct((B, N_KV, V_ROWS, S), BF16)
    chunk = CMP_STRIDE * HEAD_DIM
    cspec = pl.BlockSpec((1, N_KV, ts // CMP_STRIDE, chunk), lambda b, s: (b, 0, s, 0))
    c_shape = jax.ShapeDtypeStruct((B, N_KV, S // CMP_STRIDE, chunk), BF16)
    return pl.pallas_call(
        functools.partial(_nsa_proj_kernel, ts=ts),
        grid=(B, S // ts),
        in_specs=[pl.BlockSpec((1, ts, D), lambda b, s: (b, s, 0)), per_b, per_b,
                  pl.BlockSpec(wt.shape, lambda b, s: (0, 0)), pl.BlockSpec(ws.shape, lambda b, s: (0, 0))],
        out_specs=[pl.BlockSpec((1, N_KV, ts // NSA_QB, AUG, lq), lambda b, s: (b, 0, s, 0, 0)),
                   pl.BlockSpec((1, ts // NSA_QB, N_GATE, NSA_QB), lambda b, s: (b, s, 0, 0)),
                   kspec, vspec, kspec, vspec,
                   cspec, cspec],
        out_shape=[jax.ShapeDtypeStruct((B, N_KV, nqb, AUG, lq), BF16),
                   jax.ShapeDtypeStruct((B, nqb, N_GATE, NSA_QB), F32),
                   k_shape, v_shape, k_shape, v_shape,
                   c_shape, c_shape],
        scratch_shapes=[pltpu.VMEM((2 * KV_WIDTH // 128, ts, 128), F32)],
        compiler_params=_cparams(("arbitrary", "arbitrary")),
        name="nsa_proj",
    )(x, shift.reshape(B, 1, D), scale.reshape(B, 1, D), wt, ws)


def _swa_proj_kernel(x_ref, sh_ref, sc_ref, wq_ref, wk_ref, wv_ref, q_ref, k_ref, vt_ref, *, ts):
    x = x_ref[0]
    h = (x * (1.0 + sc_ref[0]) + sh_ref[0]).astype(BF16)
    xb = x.astype(BF16)
    _store_q_blocks(q_ref, lax.dot_general(wq_ref[...], h, _NT, preferred_element_type=F32), ts, SWA_QB)
    _store_vt(vt_ref, lax.dot_general(wv_ref[...], xb, _NT, preferred_element_type=F32), ts)
    k = jnp.dot(xb, wk_ref[...], preferred_element_type=F32)
    for g in range(N_KV):
        k_ref[0, g] = k[:, g * AUG:(g + 1) * AUG].astype(BF16)


def _swa_proj(x, shift, scale, w_q, w_kv, ts=512):
    B, S, D = x.shape
    lq = HPG * SWA_QB
    wq = (w_q * Q_SCALE).T.astype(BF16)
    wk = _pad_heads(w_kv[:, :KV_WIDTH]).astype(BF16)
    wv = w_kv[:, KV_WIDTH:].T.astype(BF16)
    per_b = pl.BlockSpec((1, 1, D), lambda b, s: (b, 0, 0))
    full = lambda a: pl.BlockSpec(a.shape, lambda b, s: (0, 0))
    return pl.pallas_call(
        functools.partial(_swa_proj_kernel, ts=ts),
        grid=(B, S // ts),
        in_specs=[pl.BlockSpec((1, ts, D), lambda b, s: (b, s, 0)), per_b, per_b, full(wq), full(wk), full(wv)],
        out_specs=[pl.BlockSpec((1, N_KV, ts // SWA_QB, AUG, lq), lambda b, s: (b, 0, s, 0, 0)),
                   pl.BlockSpec((1, N_KV, ts, AUG), lambda b, s: (b, 0, s, 0)),
                   pl.BlockSpec((1, N_KV, V_ROWS, ts), lambda b, s: (b, 0, 0, s))],
        out_shape=[jax.ShapeDtypeStruct((B, N_KV, S // SWA_QB, AUG, lq), BF16),
                   jax.ShapeDtypeStruct((B, N_KV, S, AUG), BF16),
                   jax.ShapeDtypeStruct((B, N_KV, V_ROWS, S), BF16)],
        compiler_params=_cparams(("arbitrary", "arbitrary")),
        name="swa_proj",
    )(x, shift.reshape(B, 1, D), scale.reshape(B, 1, D), wq, wk, wv)


def _gelu_tanh(x):
    return 0.5 * x * (1.0 + jnp.tanh(math.sqrt(2.0 / math.pi) * (x + 0.044715 * (x * x * x))))


def _compress_kernel(ch_ref, w1_ref, w2_ref, pos_ref, o_ref, *, nc):
    half = CMP_STRIDE * HEAD_DIM
    ch = ch_ref[0]
    w1 = w1_ref[...]
    top = jnp.dot(ch, w1[:half], preferred_element_type=F32)
    bot = jnp.dot(ch, w1[half:], preferred_element_type=F32)
    posw = jnp.dot(pos_ref[...], w1, preferred_element_type=F32)[0:1]
    w2 = w2_ref[...]
    for g in range(N_KV):
        bot_g = pltpu.roll(bot[g * nc:(g + 1) * nc], nc - 1, 0)
        pre = top[g * nc:(g + 1) * nc] + bot_g + posw
        o_ref[0, g] = jnp.dot(_gelu_tanh(pre).astype(BF16), w2, preferred_element_type=F32)


def _compress(chunks, w1, w2, pos):
    B, gn, ck = chunks.shape
    nc = gn // N_KV
    pos8 = jnp.broadcast_to(pos.reshape(1, CMP_LEN * HEAD_DIM), (8, CMP_LEN * HEAD_DIM)).astype(BF16)
    return pl.pallas_call(
        functools.partial(_compress_kernel, nc=nc),
        grid=(B,),
        in_specs=[pl.BlockSpec((1, gn, ck), lambda b: (b, 0, 0)),
                  pl.BlockSpec(w1.shape, lambda b: (0, 0)),
                  pl.BlockSpec(w2.shape, lambda b: (0, 0)),
                  pl.BlockSpec(pos8.shape, lambda b: (0, 0))],
        out_specs=pl.BlockSpec((1, N_KV, nc, HEAD_DIM), lambda b: (b, 0, 0, 0)),
        out_shape=jax.ShapeDtypeStruct((B, N_KV, nc, HEAD_DIM), F32),
        compiler_params=_cparams(("arbitrary",)),
        name="compress",
    )(chunks, w1.astype(BF16), w2.astype(BF16), pos8)


def _bucket_np(d):
    d = np.maximum(np.asarray(d, np.int64), 0)
    max_exact = N_BUCKETS // 2
    large = max_exact + (np.log(np.maximum(d, 1).astype(np.float32) / np.float32(max_exact))
                         / np.float32(math.log(MAX_DISTANCE / max_exact))
                         * np.float32(N_BUCKETS - max_exact)).astype(np.int32)
    large = np.minimum(large, N_BUCKETS - 1)
    return np.where(d < max_exact, d, large).astype(np.int32)


def _shifted_bias(rel_table):
    t = rel_table.astype(F32)
    return ((t[_bucket_np(np.arange(MAX_DISTANCE))] - t[N_BUCKETS - 1][None, :]) * LOG2E).T


def _to_group_lanes(m):
    H, K, Q = m.shape
    return m.reshape(N_KV, HPG, K, Q).transpose(0, 2, 1, 3).reshape(N_KV, K, HPG * Q)


def _bias_of_dist(fb, dist, window):
    ok = (dist >= 0) & (dist < window)
    idx = np.clip(dist, 0, MAX_DISTANCE - 1)
    return jnp.where(ok[None], jnp.where((dist >= MAX_DISTANCE)[None], 0.0, fb[:, idx]), NEG)


def _toeplitz_pair(fb, window):
    H, t = fb.shape[0], MAX_DISTANCE
    nv = 3 * t - 1
    v = _bias_of_dist(fb, np.arange(nv) - (t - 1), window)
    rows = 2 * t
    flat = jnp.tile(v, (1, rows + 1))[:, :rows * (nv + 1)]
    return flat.reshape(H, rows, nv + 1)[:, ::-1, :t]


def _near_band(fb, window, qb):
    H, t = fb.shape[0], MAX_DISTANCE
    nb = qb // t
    assert qb % t == 0 and (nb == 1 or window >= 2 * qb)
    pair = _toeplitz_pair(fb, window)
    blocks = {1: pair[:, :t], 0: pair[:, t:]}
    zero, masked = jnp.zeros((H, t, t), F32), jnp.full((H, t, t), NEG, F32)
    rows = []
    for a in range(2 * nb):
        row = [blocks.get(b + nb - a, zero if b + nb - a > 1 else masked) for b in range(nb)]
        rows.append(jnp.concatenate(row, axis=2))
    return _to_group_lanes(jnp.concatenate(rows, axis=1))


def _masked_rows(n, qb):
    return jnp.full((N_KV, n, HPG * qb), NEG, F32)


def _window_table(fb, qb):
    kk = np.arange(qb)[:, None]
    qi = np.arange(HPG * qb)[None, :] % qb
    edge = np.where(kk > qi, 0.0, NEG).astype(np.float32)
    edge = jnp.broadcast_to(jnp.asarray(edge), (N_KV, qb, HPG * qb))
    mid = jnp.zeros((N_KV, NSA_WINDOW - 2 * qb, HPG * qb), F32)
    return jnp.concatenate([edge, mid, _near_band(fb, NSA_WINDOW, qb), _masked_rows(NSA_WINDOW, qb)], axis=1)


def _cmp_band_rows(qb):
    return CMP_LEAD + qb // CMP_STRIDE


def _cmp_band(fb, nc, qb):
    m = np.arange(_cmp_band_rows(qb))[:, None]
    i = np.arange(qb)[None, :]
    dist = i + CMP_STRIDE * CMP_LEAD - CMP_STRIDE * m - (CMP_LEN - 1)
    band = _to_group_lanes(_bias_of_dist(fb, dist, 1 << 30))
    return jnp.concatenate([jnp.zeros((N_KV, nc, HPG * qb), F32), band, _masked_rows(nc, qb)], axis=1)


def _to_token_rows(oT, qb):
    pair = lambda a: jnp.concatenate([oT[:, a * qb:(a + 1) * qb], oT[:, (a + 1) * qb:(a + 2) * qb]], axis=0)
    return jnp.concatenate([pair(0).T, pair(2).T], axis=1)


def _nsa_kernel(q_ref, gt_ref, kc_ref, vct_ref, ks_ref, vst_ref, kw_ref, vwt_ref, ovl_ref, cb_ref,
                wt_ref, wh_ref, *rest, nc, step):
    o_ref, acc_ref, ow_ref, sa_ref, sb_ref, pa_ref, pb_ref = rest[-7:]
    QB, LQ, C = NSA_QB, HPG * NSA_QB, NSA_QPS
    g = pl.program_id(0)
    chains = range(C)
    ncr = kc_ref.shape[2]
    qbs = [step * C + c for c in chains]
    s0s = [qb * QB for qb in qbs]
    nblk = (step + 1) * C * QB // SLC_LEN
    qTs = [q_ref[0, 0, c] for c in chains]

    def gate(c, branch):
        rows = [gt_ref[0, c, pl.ds(branch * N_HEADS + g * HPG + j, 1), :] for j in range(HPG)]
        return jnp.concatenate(rows, axis=1)

    scs, sws, w0s = [], [], []
    for c in chains:
        cstart = nc + CMP_LEAD - (QB // CMP_STRIDE) * qbs[c]
        scs.append(jnp.dot(kc_ref[0, 0], qTs[c], preferred_element_type=F32)
                   + cb_ref[0, pl.ds(cstart, ncr), :])
    halves = [(c, b) for c in chains for b in range(QB // WIN_QB)]
    win_keys = NSA_WINDOW + WIN_QB
    for c, b in halves:
        s0h = s0s[c] + b * WIN_QB
        w0 = max(s0h - NSA_WINDOW, 0)
        t0 = w0 - (s0h - NSA_WINDOW)
        q_half = jnp.concatenate([qTs[c][:, j * QB + b * WIN_QB:j * QB + (b + 1) * WIN_QB] for j in range(HPG)],
                                 axis=1)
        w0s.append(w0)
        sws.append(jnp.dot(kw_ref[0, 0, pl.ds(w0, win_keys), :], q_half, preferred_element_type=F32)
                   + wh_ref[0, pl.ds(t0, win_keys), :])

    p_cs = []
    for c in chains:
        m_c = jnp.max(scs[c], axis=0, keepdims=True)
        e_c = jnp.exp2(scs[c] - m_c)
        l_c = jnp.sum(e_c, axis=0, keepdims=True)
        p_cs.append(e_c * jnp.where(m_c > 0.1 * NEG, 1.0 / l_c, 0.0))
    o_cmps, imps = [], []
    for c in chains:
        o_cmps.append(jnp.dot(vct_ref[0, 0], p_cs[c].astype(BF16), preferred_element_type=F32)[:HEAD_DIM])
        p_sum = functools.reduce(jnp.add, [p_cs[c][:, j * QB:(j + 1) * QB] for j in range(HPG)])
        p_hi = p_sum.astype(BF16)
        rest1 = p_sum - p_hi.astype(F32)
        p_mid = rest1.astype(BF16)
        p_lo = (rest1 - p_mid.astype(F32)).astype(BF16)
        imp = functools.reduce(jnp.add, [jnp.dot(ovl_ref[...], part, preferred_element_type=F32)
                                         for part in (p_hi, p_mid, p_lo)])
        imps.append(imp[HEAD_DIM:HEAD_DIM + nblk])

    p_ws = []
    for h in range(len(halves)):
        m_w = jnp.max(sws[h], axis=0, keepdims=True)
        p_ws.append(jnp.exp2(sws[h] - m_w).astype(BF16))
    acc_ws = [jnp.dot(vwt_ref[0, 0, :, pl.ds(w0s[h], win_keys)], p_ws[h], preferred_element_type=F32)
              for h in range(len(halves))]

    blk = lax.broadcasted_iota(jnp.int32, (nblk, QB), 0)
    sub = lax.broadcasted_iota(jnp.int32, (8, QB), 0)
    unseen = [jnp.full((MASK_LANES - nblk, LQ), NEG, BF16)] if nblk < MASK_LANES else []
    q_nears, q_fars = [], []
    for c in chains:
        tq = s0s[c] + lax.broadcasted_iota(jnp.int32, (nblk, QB), 1)
        cur = tq // SLC_LEN
        forced = (blk == 0) | (blk == cur) | (blk == cur - 1)
        future = blk > cur
        val = jnp.where(forced, 1e9, jnp.where(future, -1e9, imps[c]))
        vals = [val[8 * a:8 * a + 8] for a in range(nblk // 8)]
        cnts = [jnp.zeros((8, QB), F32) for _ in range(nblk // 8)]
        for jp in range(nblk):
            ap, r = divmod(jp, 8)
            rv = jnp.broadcast_to(vals[ap][r:r + 1], (8, QB))
            for a in range(nblk // 8):
                gt = jnp.where(rv > vals[a], 1.0, 0.0)
                ge = jnp.where(rv >= vals[a], 1.0, 0.0)
                if a < ap:
                    inc = gt
                elif a > ap:
                    inc = ge
                else:
                    inc = jnp.where(sub > r, ge, gt)
                cnts[a] = cnts[a] + inc
        cnt = jnp.concatenate(cnts, axis=0)
        allowed = (cnt < float(N_SELECT)) & (blk <= cur)
        near_blk = blk >= (s0s[c] - QB) // SLC_LEN
        neg_near = jnp.where(allowed, 0.0, NEG).astype(BF16)
        neg_far = jnp.where(allowed & jnp.logical_not(near_blk), 0.0, NEG).astype(BF16)
        q_nears.append(jnp.concatenate([qTs[c][:HEAD_DIM], jnp.tile(neg_near, (1, HPG))] + unseen, axis=0))
        q_fars.append(jnp.concatenate([qTs[c][:HEAD_DIM], jnp.tile(neg_far, (1, HPG))] + unseen, axis=0))
    for c in chains:
        o_halves = [acc_ws[h][:HEAD_DIM] / acc_ws[h][HEAD_DIM:HEAD_DIM + 1]
                    for h, (ch, _) in enumerate(halves) if ch == c]
        o_win = jnp.concatenate([o_h[:, j * WIN_QB:(j + 1) * WIN_QB] for j in range(HPG) for o_h in o_halves],
                                axis=1)
        ow_ref[c] = gate(c, 2) * o_win

    s_bufs, p_bufs = (sa_ref, sb_ref), (pa_ref, pb_ref)
    n_far = [max(s0s[c] - QB, 0) // FAR_TK for c in chains]

    def qk(c, t):
        return jnp.dot(ks_ref[0, 0, pl.ds(t * FAR_TK, FAR_TK), :], q_fars[c], preferred_element_type=F32)

    def softmax_update(s, m_run):
        m_new = jnp.maximum(m_run, jnp.max(s, axis=0, keepdims=True))
        return jnp.exp2(s - m_new).astype(BF16), jnp.exp2(m_run - m_new), m_new

    n0s, s_nears = [], []
    for c in chains:
        n0 = max(s0s[c] - QB, 0)
        b0 = NSA_WINDOW - QB + n0 - (s0s[c] - QB)
        n0s.append(n0)
        s_nears.append(jnp.dot(ks_ref[0, 0, pl.ds(n0, 2 * QB), :], q_nears[c], preferred_element_type=F32)
                       + wt_ref[0, pl.ds(b0, 2 * QB), :])
    for c in chains:
        for t in range(min(2, n_far[c])):
            s_bufs[t][c] = qk(c, t)
    m_runs = [jnp.max(s, axis=0, keepdims=True) for s in s_nears]
    p_nears = [jnp.exp2(s - m).astype(BF16) for s, m in zip(s_nears, m_runs)]
    for c in chains:
        acc_ref[c] = jnp.dot(vst_ref[0, 0, :, pl.ds(n0s[c], 2 * QB)], p_nears[c],
                             preferred_element_type=F32)
    for t in range(max(n_far)):
        for c in chains:
            if t < n_far[c]:
                s_buf, p_buf = s_bufs[t % 2], p_bufs[t % 2]
                p, alpha, m_runs[c] = softmax_update(s_buf[c], m_runs[c])
                p_buf[c] = p
                if t + 2 < n_far[c]:
                    s_buf[c] = qk(c, t + 2)
                pv = jnp.dot(vst_ref[0, 0, :, pl.ds(t * FAR_TK, FAR_TK)], p_buf[c], preferred_element_type=F32)
                acc_ref[c] = acc_ref[c] * alpha + pv

    for c in chains:
        a_c = acc_ref[c]
        o_sel = a_c[:HEAD_DIM] / a_c[HEAD_DIM:HEAD_DIM + 1]
        o = gate(c, 0) * o_cmps[c] + gate(c, 1) * o_sel + ow_ref[c]
        o_ref[0, c * QB:(c + 1) * QB, :] = _to_token_rows(o, QB).astype(o_ref.dtype)


def _nsa_attention(qT, gatesT, kc_aug, vcT, ks_aug, vsT, kw_aug, vwT, ovl, cband, wtable, whalf):
    B, G, nqb = qT.shape[:3]
    QB, LQ, C = NSA_QB, HPG * NSA_QB, NSA_QPS
    nc, S = kc_aug.shape[2], ks_aug.shape[2]
    per_g = lambda a: pl.BlockSpec((1,) + a.shape[1:], lambda g, b: (g,) + (0,) * (a.ndim - 1))
    rows = lambda a, n: pl.BlockSpec((1, 1, n, a.shape[3]), lambda g, b: (b, g, 0, 0))
    cols = lambda a, n: pl.BlockSpec((1, 1, a.shape[2], n), lambda g, b: (b, g, 0, 0))
    out = None
    for step in range(nqb // C):
        seen = (step + 1) * C * QB
        held = min(S, max(seen, 2 * NSA_WINDOW))
        ncr = min(nc, -(-(seen // CMP_STRIDE) // 128) * 128)
        ovl_r = ovl[:, :ncr]
        in_specs = [pl.BlockSpec((1, 1, C, AUG, LQ), lambda g, b, s=step: (b, g, s, 0, 0)),
                    pl.BlockSpec((1, C, N_GATE, QB), lambda g, b, s=step: (b, s, 0, 0)),
                    rows(kc_aug, ncr), cols(vcT, ncr), rows(ks_aug, held), cols(vsT, held),
                    rows(kw_aug, held), cols(vwT, held), pl.BlockSpec(ovl_r.shape, lambda g, b: (0, 0)),
                    per_g(cband), per_g(wtable), per_g(whalf)]
        args = [qT, gatesT, kc_aug, vcT, ks_aug, vsT, kw_aug, vwT, ovl_r, cband, wtable, whalf]
        aliases = {}
        if out is not None:
            in_specs.append(pl.BlockSpec(memory_space=pl.ANY))
            args.append(out)
            aliases = {len(args) - 1: 0}
        out = pl.pallas_call(
            functools.partial(_nsa_kernel, nc=nc, step=step),
            grid=(G, B),
            in_specs=in_specs,
            out_specs=pl.BlockSpec((1, C * QB, HPG * HEAD_DIM), lambda g, b, s=step: (b, s, g)),
            out_shape=jax.ShapeDtypeStruct((B, nqb * QB, Q_WIDTH), BF16),
            scratch_shapes=[pltpu.VMEM((C, V_ROWS, LQ), F32), pltpu.VMEM((C, HEAD_DIM, LQ), F32),
                            pltpu.VMEM((C, FAR_TK, LQ), F32), pltpu.VMEM((C, FAR_TK, LQ), F32),
                            pltpu.VMEM((C, FAR_TK, LQ), BF16), pltpu.VMEM((C, FAR_TK, LQ), BF16)],
            input_output_aliases=aliases,
            compiler_params=_cparams(("arbitrary", "arbitrary")),
            name=f"nsa_attention_{step}",
        )(*args)
    return out


def _swa_kernel(q_ref, k_ref, vt_ref, band_ref, sink_ref, o_ref):
    QB = SWA_QB
    sink = sink_ref[0, 0:1]
    starts, scores = [], []
    for i in range(SWA_QPS):
        s0 = (pl.program_id(2) * SWA_QPS + i) * QB
        n0 = pl.multiple_of(jnp.maximum(s0 - QB, 0), QB)
        b0 = pl.multiple_of(n0 - (s0 - QB), QB)
        starts.append(n0)
        scores.append(jnp.dot(k_ref[0, 0, pl.ds(n0, 2 * QB), :], q_ref[0, 0, i], preferred_element_type=F32)
                      + band_ref[0, pl.ds(b0, 2 * QB), :])
    maxes = [jnp.maximum(jnp.max(s, axis=0, keepdims=True), sink) for s in scores]
    probs = [jnp.exp2(s - m).astype(BF16) for s, m in zip(scores, maxes)]
    accs = [jnp.dot(vt_ref[0, 0, :, pl.ds(n0, 2 * QB)], p, preferred_element_type=F32)
            for n0, p in zip(starts, probs)]
    for i, (acc, m) in enumerate(zip(accs, maxes)):
        denom = acc[HEAD_DIM:HEAD_DIM + 1] + jnp.exp2(sink - m)
        o_ref[0, i * QB:(i + 1) * QB, :] = _to_token_rows(acc[:HEAD_DIM] / denom, QB).astype(o_ref.dtype)


def _swa_attention(qT, k_aug, vT, band, sink):
    B, G, nqb = qT.shape[:3]
    QB, LQ = SWA_QB, HPG * SWA_QB
    per_bg = lambda a: pl.BlockSpec((1, 1) + a.shape[2:], lambda b, g, q: (b, g) + (0,) * (a.ndim - 2))
    per_g = lambda a: pl.BlockSpec((1,) + a.shape[1:], lambda b, g, q: (g,) + (0,) * (a.ndim - 1))
    return pl.pallas_call(
        _swa_kernel,
        grid=(B, G, nqb // SWA_QPS),
        in_specs=[pl.BlockSpec((1, 1, SWA_QPS, AUG, LQ), lambda b, g, q: (b, g, q, 0, 0)),
                  per_bg(k_aug), per_bg(vT), per_g(band), per_g(sink)],
        out_specs=pl.BlockSpec((1, SWA_QPS * QB, HPG * HEAD_DIM), lambda b, g, q: (b, q, g)),
        out_shape=jax.ShapeDtypeStruct((B, nqb * QB, Q_WIDTH), BF16),
        compiler_params=_cparams(("arbitrary", "arbitrary", "arbitrary")),
        name="swa_attention",
    )(qT, k_aug, vT, band, sink)


def _layer_norm(y, g, b):
    mu = jnp.mean(y, axis=-1, keepdims=True)
    yc = y - mu
    var = jnp.mean(yc * yc, axis=-1, keepdims=True)
    return yc * lax.rsqrt(var + LN_EPS) * g + b


def _top2_of4(a, b, c, d):
    hi1, lo1 = jnp.maximum(a, b), jnp.minimum(a, b)
    hi2, lo2 = jnp.maximum(c, d), jnp.minimum(c, d)
    return jnp.maximum(hi1, hi2) + jnp.maximum(jnp.minimum(hi1, hi2), jnp.maximum(lo1, lo2))


def _route(s, sb):
    n = EXPERTS_PER_GROUP
    score = [_top2_of4(*sb[n * r:n * r + n]) for r in range(N_GROUPS)]
    best = functools.reduce(jnp.maximum, score)
    taken = jnp.zeros_like(best) > 1.0
    in_grp = []
    for r in range(N_GROUPS):
        pick = (score[r] == best) & jnp.logical_not(taken)
        in_grp.append(pick)
        taken = taken | pick
    gates = []
    for e in range(N_EXPERTS):
        r = e // n
        ahead = jnp.zeros_like(best)
        for f in range(n * r, n * r + n):
            if f != e:
                beats = (sb[f] >= sb[e]) if f < e else (sb[f] > sb[e])
                ahead = ahead + jnp.where(beats, 1.0, 0.0)
        gates.append(jnp.where(in_grp[r] & (ahead < 2.0), s[e], 0.0))
    total = functools.reduce(jnp.add, gates)
    return [g / total for g in gates]


def _oproj_kernel(o_ref, x_ref, w_ref, ga_ref, lg_ref, lb_ref, shf_ref, scf_ref, rw_ref, rb_ref,
                  x1_ref, h_ref, gate_ref):
    ts = o_ref.shape[1]
    rows = [slice(k * ts // OPROJ_CHUNKS, (k + 1) * ts // OPROJ_CHUNKS) for k in range(OPROJ_CHUNKS)]
    mixes = [jnp.dot(o_ref[0, r, :], w_ref[...], preferred_element_type=F32) for r in rows]
    hbs = []
    for r, mix in zip(rows, mixes):
        x1 = _layer_norm(ALPHA * x_ref[0, r, :] + (1.0 + ga_ref[0]) * mix, lg_ref[...], lb_ref[...])
        x1_ref[0, r, :] = x1
        hb = (x1 * (1.0 + scf_ref[0]) + shf_ref[0]).astype(BF16)
        h_ref[0, r, :] = hb
        hbs.append(hb)
    logit = jnp.concatenate([lax.dot_general(rw_ref[...], hb, _NT, preferred_element_type=F32) for hb in hbs],
                            axis=1)
    aff = jax.nn.sigmoid(logit)
    biased = aff + rb_ref[...][:, 0:1]
    s = [aff[e:e + 1] for e in range(N_EXPERTS)]
    sb = [biased[e:e + 1] for e in range(N_EXPERTS)]
    gate_ref[0] = jnp.concatenate(_route(s, sb), axis=0)


def _oproj(o, x, w_out, g_a, ln_g, ln_b, sh_f, sc_f, router_w, router_b, ts=512):
    B, S, D = x.shape
    row = lambda a: a.reshape(1, D)
    per_b = pl.BlockSpec((1, 1, D), lambda b, s: (b, 0, 0))
    full = lambda a: pl.BlockSpec(a.shape, lambda b, s: (0,) * a.ndim)
    rwT = router_w.T.astype(BF16)
    rb = jnp.broadcast_to(router_b.reshape(N_EXPERTS, 1), (N_EXPERTS, 128))
    args = (o, x, w_out.astype(BF16), g_a.reshape(B, 1, D), row(ln_g), row(ln_b),
            sh_f.reshape(B, 1, D), sc_f.reshape(B, 1, D), rwT, rb)
    tile = pl.BlockSpec((1, ts, D), lambda b, s: (b, s, 0))
    return pl.pallas_call(
        _oproj_kernel,
        grid=(B, S // ts),
        in_specs=[tile, tile, full(args[2]), per_b, full(args[4]), full(args[5]), per_b, per_b,
                  full(rwT), full(rb)],
        out_specs=[tile, tile, pl.BlockSpec((1, N_EXPERTS, ts), lambda b, s: (b, 0, s))],
        out_shape=[jax.ShapeDtypeStruct((B, S, D), F32), jax.ShapeDtypeStruct((B, S, D), BF16),
                   jax.ShapeDtypeStruct((B, N_EXPERTS, S), F32)],
        compiler_params=_cparams(("arbitrary", "arbitrary")),
        name="oproj_ln_router",
    )(*args)


def _moe_kernel(h_ref, gate_ref, wg_ref, wu_ref, wd_ref, x_ref, gf_ref, lg_ref, lb_ref, o_ref, acc_ref):
    n = EXPERTS_PER_GROUP
    h = h_ref[...]
    gates = gate_ref[...]
    lane = lax.broadcasted_iota(jnp.int32, gates.shape, 1)
    for r in range(N_GROUPS):
        he = []
        for k in range(n):
            e = r * n + k
            a = jnp.dot(h, wg_ref[e], preferred_element_type=F32)
            u = jnp.dot(h, wu_ref[e], preferred_element_type=F32)
            gcol = jnp.sum(jnp.where(lane == e, gates, 0.0), axis=1, keepdims=True)
            he.append((a * jax.nn.sigmoid(a) * u * gcol).astype(BF16))
        wd = wd_ref[r * n:(r + 1) * n].reshape(n * D_EXPERT, wd_ref.shape[-1])
        ffn = jnp.dot(jnp.concatenate(he, axis=1), wd, preferred_element_type=F32)
        if r == 0:
            acc_ref[...] = ffn
        else:
            acc_ref[...] += ffn
    y = ALPHA * x_ref[...] + (1.0 + gf_ref[0]) * acc_ref[...]
    o_ref[...] = _layer_norm(y, lg_ref[...], lb_ref[...])


def _moe(h, gates, w_gate, w_up, w_down, x, g_f, ln_g, ln_b, ts=512):
    B, S, D = x.shape
    T = B * S
    gl = jnp.pad(gates.transpose(0, 2, 1).reshape(T, N_EXPERTS), ((0, 0), (0, 128 - N_EXPERTS)))
    tile = pl.BlockSpec((ts, D), lambda t: (t, 0))
    spt = S // ts
    resident = lambda a: pl.BlockSpec(a.shape, lambda t: (0,) * a.ndim, pipeline_mode=pl.Buffered(1))
    wg, wu, wd = w_gate.astype(BF16), w_up.astype(BF16), w_down.astype(BF16)
    out = pl.pallas_call(
        _moe_kernel,
        grid=(T // ts,),
        in_specs=[tile, pl.BlockSpec((ts, 128), lambda t: (t, 0)), resident(wg), resident(wu), resident(wd),
                  tile, pl.BlockSpec((1, 1, D), lambda t: (t // spt, 0, 0)),
                  pl.BlockSpec((1, D), lambda t: (0, 0)), pl.BlockSpec((1, D), lambda t: (0, 0))],
        out_specs=tile,
        out_shape=jax.ShapeDtypeStruct((T, D), F32),
        scratch_shapes=[pltpu.VMEM((ts, D), F32)],
        compiler_params=_cparams(("arbitrary",)),
        name="moe_ln",
    )(h.reshape(T, D), gl, wg, wu, wd, x.reshape(T, D), g_f.reshape(B, 1, D), ln_g.reshape(1, D),
      ln_b.reshape(1, D))
    return out.reshape(B, S, D)


def _values_t(v):
    B, G, n, _ = v.shape
    return jnp.concatenate([v.transpose(0, 1, 3, 2), jnp.ones((B, G, V_ROWS - HEAD_DIM, n), v.dtype)], axis=2)


def _nsa_layer_attention(x, sh, sc, w_in, k_w1, k_w2, v_w1, v_w2, k_pos, v_pos, fb):
    B, S, _ = x.shape
    nc, n_slc = S // CMP_STRIDE, S // SLC_LEN
    qT, gatesT, ks_aug, vsT, kw_aug, vwT, ck, cv = _nsa_proj(x, sh, sc, w_in)
    merge = lambda t: t.reshape(B, N_KV * nc, CMP_STRIDE * HEAD_DIM)
    k_cmp = _compress(merge(ck), k_w1, k_w2, k_pos)
    v_cmp = _compress(merge(cv), v_w1, v_w2, v_pos)
    kc_aug = jnp.pad(k_cmp.astype(BF16), ((0, 0), (0, 0), (0, 0), (0, MASK_LANES)))
    vcT = _values_t(v_cmp.astype(BF16))

    c_start = np.arange(nc)[None, :] * CMP_STRIDE
    s_start = np.arange(MASK_LANES)[:, None] * SLC_LEN
    ovl = ((c_start < s_start + SLC_LEN) & (c_start + CMP_LEN > s_start)
           & (np.arange(nc)[None, :] < nc - 1) & (np.arange(MASK_LANES)[:, None] < n_slc))
    ovl = np.concatenate([np.zeros((HEAD_DIM, nc)), ovl.astype(np.float64)], axis=0).astype(np.float32)

    return _nsa_attention(qT, gatesT, kc_aug, vcT, ks_aug, vsT, kw_aug, vwT, jnp.asarray(ovl, BF16),
                          _cmp_band(fb, nc, NSA_QB), _window_table(fb, NSA_QB), _window_table(fb, WIN_QB))


def _swa_layer_attention(x, sh, sc, w_q, w_kv, sinks, rel_bias, fb):
    qT, k_aug, vT = _swa_proj(x, sh, sc, w_q, w_kv)
    sink = (sinks.astype(F32) - rel_bias.astype(F32)[N_BUCKETS - 1]) * LOG2E
    sink = jnp.broadcast_to(sink.reshape(N_KV, 1, HPG, 1), (N_KV, 8, HPG, SWA_QB))
    sink = sink.reshape(N_KV, 8, HPG * SWA_QB)
    band = jnp.concatenate([_near_band(fb, SWA_WINDOW, SWA_QB), _masked_rows(SWA_QB, SWA_QB)], axis=1)
    return _swa_attention(qT, k_aug, vT, band, sink)


def kernel(x, c, nsa_w_in, cmp_k_w1, cmp_k_w2, cmp_v_w1, cmp_v_w2, cmp_k_pos, cmp_v_pos, nsa_w_out,
           swa_w_q, swa_sinks, swa_w_out, shared_w_kv, rel_bias, router_w, router_b, moe_w_gate,
           moe_w_up, moe_w_down, ada_w, ada_b, ln_g, ln_b):
    B, S, D = x.shape
    ada = _ada(c, ada_w, ada_b)
    fb = _shifted_bias(rel_bias)
    for layer in range(DEPTH):
        sh_a, sc_a, g_a, sh_f, sc_f, g_f = [ada[layer, :, i * D:(i + 1) * D] for i in range(6)]
        if layer == 0:
            o = _nsa_layer_attention(x, sh_a, sc_a, nsa_w_in[0], cmp_k_w1[0], cmp_k_w2[0], cmp_v_w1[0],
                                     cmp_v_w2[0], cmp_k_pos[0], cmp_v_pos[0], fb)
            w_out = nsa_w_out[0]
        else:
            o = _swa_layer_attention(x, sh_a, sc_a, swa_w_q[0], shared_w_kv, swa_sinks[0], rel_bias, fb)
            w_out = swa_w_out[0]
        x1, h, gates = _oproj(o, x, w_out, g_a, ln_g[layer, 0], ln_b[layer, 0], sh_f, sc_f,
                              router_w, router_b)
        x = _moe(h, gates, moe_w_gate[layer], moe_w_up[layer], moe_w_down[layer], x1, g_f,
                 ln_g[layer, 1], ln_b[layer, 1])
    return x
```

```python
import functools
import math

import numpy as np
import jax
import jax.numpy as jnp
from jax import lax
from jax.experimental import pallas as pl
from jax.experimental.pallas import tpu as pltpu

F32 = jnp.float32
BF16 = jnp.bfloat16

D_MODEL = 1024
HEAD_DIM = 64
N_HEADS = 16
N_KV = 4
HPG = 4
Q_WIDTH = N_HEADS * HEAD_DIM
KV_WIDTH = N_KV * HEAD_DIM
CMP_LEN = 32
CMP_STRIDE = 16
CMP_HIDDEN = 256
SLC_LEN = 64
N_SELECT = 16
NSA_WINDOW = 512
SWA_WINDOW = 128
N_BUCKETS = 32
MAX_DISTANCE = 128
N_EXPERTS = 16
N_GROUPS = 4
EXPERTS_PER_GROUP = 4
D_EXPERT = 256
DEPTH = 2
ALPHA = (2.0 * DEPTH) ** 0.25
LN_EPS = 1e-5
NEG = -1e30
ATTN_SCALE = HEAD_DIM ** -0.5
LOG2E = math.log2(math.e)
Q_SCALE = ATTN_SCALE * LOG2E

NSA_QB = 256
NSA_QPS = 2
WIN_QB = 128
SWA_QB = 128
SWA_QPS = 16
AUG = 128
MASK_LANES = AUG - HEAD_DIM
V_ROWS = HEAD_DIM + 16
CMP_LEAD = 16
FAR_TK = 256
OPROJ_CHUNKS = 4
N_GATE = 3 * N_HEADS
VMEM_LIMIT = 56 * 1024 * 1024

_NT = (((1,), (1,)), ((), ()))


def _cparams(sem):
    return pltpu.CompilerParams(dimension_semantics=sem, vmem_limit_bytes=VMEM_LIMIT)


def _ada_kernel(c_ref, w_ref, b_ref, o_ref):
    c = c_ref[...]
    cond = c * jax.nn.sigmoid(c)
    o_ref[0] = jnp.dot(cond, w_ref[0], preferred_element_type=F32,
                       precision=lax.Precision.HIGHEST) + b_ref[0]


def _ada(c, ada_w, ada_b):
    B, D = c.shape
    n6 = ada_w.shape[-1]
    tn = 1536
    return pl.pallas_call(
        _ada_kernel,
        grid=(DEPTH, n6 // tn),
        in_specs=[pl.BlockSpec((B, D), lambda l, n: (0, 0)),
                  pl.BlockSpec((1, D, tn), lambda l, n: (l, 0, n)),
                  pl.BlockSpec((1, 1, tn), lambda l, n: (l, 0, n))],
        out_specs=pl.BlockSpec((1, B, tn), lambda l, n: (l, 0, n)),
        out_shape=jax.ShapeDtypeStruct((DEPTH, B, n6), F32),
        compiler_params=_cparams(("arbitrary", "arbitrary")),
        name="ada",
    )(c, ada_w, ada_b.reshape(DEPTH, 1, n6))


def _store_q_blocks(q_ref, qT, ts, qb):
    zeros = jnp.zeros((MASK_LANES, HPG * qb), BF16)
    for g in range(N_KV):
        for k in range(ts // qb):
            for j in range(HPG):
                r0 = (g * HPG + j) * HEAD_DIM
                q_ref[0, g, k, 0:HEAD_DIM, j * qb:(j + 1) * qb] = (
                    qT[r0:r0 + HEAD_DIM, k * qb:(k + 1) * qb].astype(BF16))
            q_ref[0, g, k, HEAD_DIM:AUG, :] = zeros


def _store_vt(vt_ref, vT, ts):
    ones = jnp.ones((V_ROWS - HEAD_DIM, ts), BF16)
    for g in range(N_KV):
        vt_ref[0, g, 0:HEAD_DIM, :] = vT[g * HEAD_DIM:(g + 1) * HEAD_DIM].astype(BF16)
        vt_ref[0, g, HEAD_DIM:V_ROWS, :] = ones


def _nsa_proj_kernel(x_ref, sh_ref, sc_ref, wt_ref, ws_ref, q_ref, gt_ref, ks_ref, vst_ref, kw_ref,
                     vwt_ref, ck_ref, cv_ref, cmp_scr, *, ts):
    s = pl.program_id(1)
    h = (x_ref[0] * (1.0 + sc_ref[0]) + sh_ref[0]).astype(BF16)
    yT = lax.dot_general(wt_ref[...], h, _NT, preferred_element_type=F32)
    y = jnp.dot(h, ws_ref[...], preferred_element_type=F32)
    _store_q_blocks(q_ref, yT[:Q_WIDTH], ts, NSA_QB)
    _store_vt(vst_ref, yT[Q_WIDTH:Q_WIDTH + KV_WIDTH], ts)
    _store_vt(vwt_ref, yT[Q_WIDTH + KV_WIDTH:Q_WIDTH + 2 * KV_WIDTH], ts)
    gates = jax.nn.sigmoid(yT[Q_WIDTH + 2 * KV_WIDTH:])
    for k in range(ts // NSA_QB):
        gt_ref[0, k] = gates[:, k * NSA_QB:(k + 1) * NSA_QB]
    lane = lax.broadcasted_iota(jnp.int32, (ts, AUG), 1)
    tok = s * ts + lax.broadcasted_iota(jnp.int32, (ts, AUG), 0)
    onehot = (lane - HEAD_DIM) == tok // SLC_LEN
    for g in range(N_KV):
        ks_ref[0, g] = jnp.where(onehot, 1.0, y[:, g * AUG:(g + 1) * AUG]).astype(BF16)
        kw_ref[0, g] = y[:, (N_KV + g) * AUG:(N_KV + g + 1) * AUG].astype(BF16)
    base = 2 * N_KV * AUG
    for c in range(2 * KV_WIDTH // 128):
        cmp_scr[c] = y[:, base + c * 128:base + (c + 1) * 128]
    for c, dst in enumerate([ck_ref, ck_ref, cv_ref, cv_ref]):
        for r in range(CMP_STRIDE):
            piece = cmp_scr[c, pl.ds(r, ts // CMP_STRIDE, stride=CMP_STRIDE), :]
            for half in range(2):
                g = 2 * (c % 2) + half
                dst[0, g, :, r * HEAD_DIM:(r + 1) * HEAD_DIM] = (
                    piece[:, half * HEAD_DIM:(half + 1) * HEAD_DIM].astype(BF16))


def _pad_heads(w):
    D = w.shape[0]
    w = w.reshape(D, N_KV, HEAD_DIM)
    return jnp.pad(w, ((0, 0), (0, 0), (0, MASK_LANES))).reshape(D, N_KV * AUG)


def _nsa_proj(x, shift, scale, w_in, ts=512):
    B, S, D = x.shape
    nqb, lq = S // NSA_QB, HPG * NSA_QB
    cols = lambda i: w_in[:, Q_WIDTH + i * KV_WIDTH:Q_WIDTH + (i + 1) * KV_WIDTH]
    w_kc, w_vc, w_ksl, w_vsl, w_kw, w_vw = [cols(i) for i in range(6)]
    wt = jnp.concatenate([w_in[:, :Q_WIDTH] * Q_SCALE, w_vsl, w_vw,
                          w_in[:, Q_WIDTH + 6 * KV_WIDTH:]], axis=1).T.astype(BF16)
    ws = jnp.concatenate([_pad_heads(w_ksl), _pad_heads(w_kw), w_kc, w_vc], axis=1).astype(BF16)
    per_b = pl.BlockSpec((1, 1, D), lambda b, s: (b, 0, 0))
    kspec = pl.BlockSpec((1, N_KV, ts, AUG), lambda b, s: (b, 0, s, 0))
    vspec = pl.BlockSpec((1, N_KV, V_ROWS, ts), lambda b, s: (b, 0, 0, s))
    k_shape = jax.ShapeDtypeStruct((B, N_KV, S, AUG), BF16)
    v_shape = jax.ShapeDtypeStruct((B, N_KV, V_ROWS, S), BF16)
    chunk = CMP_STRIDE * HEAD_DIM
    cspec = pl.BlockSpec((1, N_KV, ts // CMP_STRIDE, chunk), lambda b, s: (b, 0, s, 0))
    c_shape = jax.ShapeDtypeStruct((B, N_KV, S // CMP_STRIDE, chunk), BF16)
    return pl.pallas_call(
        functools.partial(_nsa_proj_kernel, ts=ts),
        grid=(B, S // ts),
        in_specs=[pl.BlockSpec((1, ts, D), lambda b, s: (b, s, 0)), per_b, per_b,
                  pl.BlockSpec(wt.shape, lambda b, s: (0, 0)), pl.BlockSpec(ws.shape, lambda b, s: (0, 0))],
        out_specs=[pl.BlockSpec((1, N_KV, ts // NSA_QB, AUG, lq), lambda b, s: (b, 0, s, 0, 0)),
                   pl.BlockSpec((1, ts // NSA_QB, N_GATE, NSA_QB), lambda b, s: (b, s, 0, 0)),
                   kspec, vspec, kspec, vspec,
                   cspec, cspec],
        out_shape=[jax.ShapeDtypeStruct((B, N_KV, nqb, AUG, lq), BF16),
                   jax.ShapeDtypeStruct((B, nqb, N_GATE, NSA_QB), F32),
                   k_shape, v_shape, k_shape, v_shape,
                   c_shape, c_shape],
        scratch_shapes=[pltpu.VMEM((2 * KV_WIDTH // 128, ts, 128), F32)],
        compiler_params=_cparams(("arbitrary", "arbitrary")),
        name="nsa_proj",
    )(x, shift.reshape(B, 1, D), scale.reshape(B, 1, D), wt, ws)


def _swa_proj_kernel(x_ref, sh_ref, sc_ref, wq_ref, wk_ref, wv_ref, q_ref, k_ref, vt_ref, *, ts):
    x = x_ref[0]
    h = (x * (1.0 + sc_ref[0]) + sh_ref[0]).astype(BF16)
    xb = x.astype(BF16)
    _store_q_blocks(q_ref, lax.dot_general(wq_ref[...], h, _NT, preferred_element_type=F32), ts, SWA_QB)
    _store_vt(vt_ref, lax.dot_general(wv_ref[...], xb, _NT, preferred_element_type=F32), ts)
    k = jnp.dot(xb, wk_ref[...], preferred_element_type=F32)
    for g in range(N_KV):
        k_ref[0, g] = k[:, g * AUG:(g + 1) * AUG].astype(BF16)


def _swa_proj(x, shift, scale, w_q, w_kv, ts=512):
    B, S, D = x.shape
    lq = HPG * SWA_QB
    wq = (w_q * Q_SCALE).T.astype(BF16)
    wk = _pad_heads(w_kv[:, :KV_WIDTH]).astype(BF16)
    wv = w_kv[:, KV_WIDTH:].T.astype(BF16)
    per_b = pl.BlockSpec((1, 1, D), lambda b, s: (b, 0, 0))
    full = lambda a: pl.BlockSpec(a.shape, lambda b, s: (0, 0))
    return pl.pallas_call(
        functools.partial(_swa_proj_kernel, ts=ts),
        grid=(B, S // ts),
        in_specs=[pl.BlockSpec((1, ts, D), lambda b, s: (b, s, 0)), per_b, per_b, full(wq), full(wk), full(wv)],
        out_specs=[pl.BlockSpec((1, N_KV, ts // SWA_QB, AUG, lq), lambda b, s: (b, 0, s, 0, 0)),
                   pl.BlockSpec((1, N_KV, ts, AUG), lambda b, s: (b, 0, s, 0)),
                   pl.BlockSpec((1, N_KV, V_ROWS, ts), lambda b, s: (b, 0, 0, s))],
        out_shape=[jax.ShapeDtypeStruct((B, N_KV, S // SWA_QB, AUG, lq), BF16),
                   jax.ShapeDtypeStruct((B, N_KV, S, AUG), BF16),
                   jax.ShapeDtypeStruct((B, N_KV, V_ROWS, S), BF16)],
        compiler_params=_cparams(("arbitrary", "arbitrary")),
        name="swa_proj",
    )(x, shift.reshape(B, 1, D), scale.reshape(B, 1, D), wq, wk, wv)


def _gelu_tanh(x):
    return 0.5 * x * (1.0 + jnp.tanh(math.sqrt(2.0 / math.pi) * (x + 0.044715 * (x * x * x))))


def _compress_kernel(ch_ref, w1_ref, w2_ref, pos_ref, o_ref, *, nc):
    half = CMP_STRIDE * HEAD_DIM
    ch = ch_ref[0]
    w1 = w1_ref[...]
    top = jnp.dot(ch, w1[:half], preferred_element_type=F32)
    bot = jnp.dot(ch, w1[half:], preferred_element_type=F32)
    posw = jnp.dot(pos_ref[...], w1, preferred_element_type=F32)[0:1]
    w2 = w2_ref[...]
    for g in range(N_KV):
        bot_g = pltpu.roll(bot[g * nc:(g + 1) * nc], nc - 1, 0)
        pre = top[g * nc:(g + 1) * nc] + bot_g + posw
        o_ref[0, g] = jnp.dot(_gelu_tanh(pre).astype(BF16), w2, preferred_element_type=F32)


def _compress(chunks, w1, w2, pos):
    B, gn, ck = chunks.shape
    nc = gn // N_KV
    pos8 = jnp.broadcast_to(pos.reshape(1, CMP_LEN * HEAD_DIM), (8, CMP_LEN * HEAD_DIM)).astype(BF16)
    return pl.pallas_call(
        functools.partial(_compress_kernel, nc=nc),
        grid=(B,),
        in_specs=[pl.BlockSpec((1, gn, ck), lambda b: (b, 0, 0)),
                  pl.BlockSpec(w1.shape, lambda b: (0, 0)),
                  pl.BlockSpec(w2.shape, lambda b: (0, 0)),
                  pl.BlockSpec(pos8.shape, lambda b: (0, 0))],
        out_specs=pl.BlockSpec((1, N_KV, nc, HEAD_DIM), lambda b: (b, 0, 0, 0)),
        out_shape=jax.ShapeDtypeStruct((B, N_KV, nc, HEAD_DIM), F32),
        compiler_params=_cparams(("arbitrary",)),
        name="compress",
    )(chunks, w1.astype(BF16), w2.astype(BF16), pos8)


def _bucket_np(d):
    d = np.maximum(np.asarray(d, np.int64), 0)
    max_exact = N_BUCKETS // 2
    large = max_exact + (np.log(np.maximum(d, 1).astype(np.float32) / np.float32(max_exact))
                         / np.float32(math.log(MAX_DISTANCE / max_exact))
                         * np.float32(N_BUCKETS - max_exact)).astype(np.int32)
    large = np.minimum(large, N_BUCKETS - 1)
    return np.where(d < max_exact, d, large).astype(np.int32)


def _shifted_bias(rel_table):
    t = rel_table.astype(F32)
    return ((t[_bucket_np(np.arange(MAX_DISTANCE))] - t[N_BUCKETS - 1][None, :]) * LOG2E).T


def _to_group_lanes(m):
    H, K, Q = m.shape
    return m.reshape(N_KV, HPG, K, Q).transpose(0, 2, 1, 3).reshape(N_KV, K, HPG * Q)


def _bias_of_dist(fb, dist, window):
    ok = (dist >= 0) & (dist < window)
    idx = np.clip(dist, 0, MAX_DISTANCE - 1)
    return jnp.where(ok[None], jnp.where((dist >= MAX_DISTANCE)[None], 0.0, fb[:, idx]), NEG)


def _toeplitz_pair(fb, window):
    H, t = fb.shape[0], MAX_DISTANCE
    nv = 3 * t - 1
    v = _bias_of_dist(fb, np.arange(nv) - (t - 1), window)
    rows = 2 * t
    flat = jnp.tile(v, (1, rows + 1))[:, :rows * (nv + 1)]
    return flat.reshape(H, rows, nv + 1)[:, ::-1, :t]


def _near_band(fb, window, qb):
    H, t = fb.shape[0], MAX_DISTANCE
    nb = qb // t
    assert qb % t == 0 and (nb == 1 or window >= 2 * qb)
    pair = _toeplitz_pair(fb, window)
    blocks = {1: pair[:, :t], 0: pair[:, t:]}
    zero, masked = jnp.zeros((H, t, t), F32), jnp.full((H, t, t), NEG, F32)
    rows = []
    for a in range(2 * nb):
        row = [blocks.get(b + nb - a, zero if b + nb - a > 1 else masked) for b in range(nb)]
        rows.append(jnp.concatenate(row, axis=2))
    return _to_group_lanes(jnp.concatenate(rows, axis=1))


def _masked_rows(n, qb):
    return jnp.full((N_KV, n, HPG * qb), NEG, F32)


def _window_table(fb, qb):
    kk = np.arange(qb)[:, None]
    qi = np.arange(HPG * qb)[None, :] % qb
    edge = np.where(kk > qi, 0.0, NEG).astype(np.float32)
    edge = jnp.broadcast_to(jnp.asarray(edge), (N_KV, qb, HPG * qb))
    mid = jnp.zeros((N_KV, NSA_WINDOW - 2 * qb, HPG * qb), F32)
    return jnp.concatenate([edge, mid, _near_band(fb, NSA_WINDOW, qb), _masked_rows(NSA_WINDOW, qb)], axis=1)


def _cmp_band_rows(qb):
    return CMP_LEAD + qb // CMP_STRIDE


def _cmp_band(fb, nc, qb):
    m = np.arange(_cmp_band_rows(qb))[:, None]
    i = np.arange(qb)[None, :]
    dist = i + CMP_STRIDE * CMP_LEAD - CMP_STRIDE * m - (CMP_LEN - 1)
    band = _to_group_lanes(_bias_of_dist(fb, dist, 1 << 30))
    return jnp.concatenate([jnp.zeros((N_KV, nc, HPG * qb), F32), band, _masked_rows(nc, qb)], axis=1)


def _to_token_rows(oT, qb):
    pair = lambda a: jnp.concatenate([oT[:, a * qb:(a + 1) * qb], oT[:, (a + 1) * qb:(a + 2) * qb]], axis=0)
    return jnp.concatenate([pair(0).T, pair(2).T], axis=1)


def _nsa_kernel(q_ref, gt_ref, kc_ref, vct_ref, ks_ref, vst_ref, kw_ref, vwt_ref, ovl_ref, cb_ref,
                wt_ref, wh_ref, *rest, nc, step):
    o_ref, acc_ref, ow_ref, sa_ref, sb_ref, pa_ref, pb_ref = rest[-7:]
    QB, LQ, C = NSA_QB, HPG * NSA_QB, NSA_QPS
    g = pl.program_id(0)
    chains = range(C)
    ncr = kc_ref.shape[2]
    qbs = [step * C + c for c in chains]
    s0s = [qb * QB for qb in qbs]
    nblk = (step + 1) * C * QB // SLC_LEN
    qTs = [q_ref[0, 0, c] for c in chains]

    def gate(c, branch):
        rows = [gt_ref[0, c, pl.ds(branch * N_HEADS + g * HPG + j, 1), :] for j in range(HPG)]
        return jnp.concatenate(rows, axis=1)

    scs, sws, w0s = [], [], []
    for c in chains:
        cstart = nc + CMP_LEAD - (QB // CMP_STRIDE) * qbs[c]
        scs.append(jnp.dot(kc_ref[0, 0], qTs[c], preferred_element_type=F32)
                   + cb_ref[0, pl.ds(cstart, ncr), :])
    halves = [(c, b) for c in chains for b in range(QB // WIN_QB)]
    win_keys = NSA_WINDOW + WIN_QB
    for c, b in halves:
        s0h = s0s[c] + b * WIN_QB
        w0 = max(s0h - NSA_WINDOW, 0)
        t0 = w0 - (s0h - NSA_WINDOW)
        q_half = jnp.concatenate([qTs[c][:, j * QB + b * WIN_QB:j * QB + (b + 1) * WIN_QB] for j in range(HPG)],
                                 axis=1)
        w0s.append(w0)
        sws.append(jnp.dot(kw_ref[0, 0, pl.ds(w0, win_keys), :], q_half, preferred_element_type=F32)
                   + wh_ref[0, pl.ds(t0, win_keys), :])

    p_cs = []
    for c in chains:
        m_c = jnp.max(scs[c], axis=0, keepdims=True)
        e_c = jnp.exp2(scs[c] - m_c)
        l_c = jnp.sum(e_c, axis=0, keepdims=True)
        p_cs.append(e_c * jnp.where(m_c > 0.1 * NEG, 1.0 / l_c, 0.0))
    o_cmps, imps = [], []
    for c in chains:
        o_cmps.append(jnp.dot(vct_ref[0, 0], p_cs[c].astype(BF16), preferred_element_type=F32)[:HEAD_DIM])
        p_sum = functools.reduce(jnp.add, [p_cs[c][:, j * QB:(j + 1) * QB] for j in range(HPG)])
        p_hi = p_sum.astype(BF16)
        rest1 = p_sum - p_hi.astype(F32)
        p_mid = rest1.astype(BF16)
        p_lo = (rest1 - p_mid.astype(F32)).astype(BF16)
        imp = functools.reduce(jnp.add, [jnp.dot(ovl_ref[...], part, preferred_element_type=F32)
                                         for part in (p_hi, p_mid, p_lo)])
        imps.append(imp[HEAD_DIM:HEAD_DIM + nblk])

    p_ws = []
    for h in range(len(halves)):
        m_w = jnp.max(sws[h], axis=0, keepdims=True)
        p_ws.append(jnp.exp2(sws[h] - m_w).astype(BF16))
    acc_ws = [jnp.dot(vwt_ref[0, 0, :, pl.ds(w0s[h], win_keys)], p_ws[h], preferred_element_type=F32)
              for h in range(len(halves))]

    blk = lax.broadcasted_iota(jnp.int32, (nblk, QB), 0)
    sub = lax.broadcasted_iota(jnp.int32, (8, QB), 0)
    unseen = [jnp.full((MASK_LANES - nblk, LQ), NEG, BF16)] if nblk < MASK_LANES else []
    q_nears, q_fars = [], []
    for c in chains:
        tq = s0s[c] + lax.broadcasted_iota(jnp.int32, (nblk, QB), 1)
        cur = tq // SLC_LEN
        forced = (blk == 0) | (blk == cur) | (blk == cur - 1)
        future = blk > cur
        val = jnp.where(forced, 1e9, jnp.where(future, -1e9, imps[c]))
        vals = [val[8 * a:8 * a + 8] for a in range(nblk // 8)]
        cnts = [jnp.zeros((8, QB), F32) for _ in range(nblk // 8)]
        for jp in range(nblk):
            ap, r = divmod(jp, 8)
            rv = jnp.broadcast_to(vals[ap][r:r + 1], (8, QB))
            for a in range(nblk // 8):
                gt = jnp.where(rv > vals[a], 1.0, 0.0)
                ge = jnp.where(rv >= vals[a], 1.0, 0.0)
                if a < ap:
                    inc = gt
                elif a > ap:
                    inc = ge
                else:
                    inc = jnp.where(sub > r, ge, gt)
                cnts[a] = cnts[a] + inc
        cnt = jnp.concatenate(cnts, axis=0)
        allowed = (cnt < float(N_SELECT)) & (blk <= cur)
        near_blk = blk >= (s0s[c] - QB) // SLC_LEN
        neg_near = jnp.where(allowed, 0.0, NEG).astype(BF16)
        neg_far = jnp.where(allowed & jnp.logical_not(near_blk), 0.0, NEG).astype(BF16)
        q_nears.append(jnp.concatenate([qTs[c][:HEAD_DIM], jnp.tile(neg_near, (1, HPG))] + unseen, axis=0))
        q_fars.append(jnp.concatenate([qTs[c][:HEAD_DIM], jnp.tile(neg_far, (1, HPG))] + unseen, axis=0))
    for c in chains:
        o_halves = [acc_ws[h][:HEAD_DIM] / acc_ws[h][HEAD_DIM:HEAD_DIM + 1]
                    for h, (ch, _) in enumerate(halves) if ch == c]
        o_win = jnp.concatenate([o_h[:, j * WIN_QB:(j + 1) * WIN_QB] for j in range(HPG) for o_h in o_halves],
                                axis=1)
        ow_ref[c] = gate(c, 2) * o_win

    s_bufs, p_bufs = (sa_ref, sb_ref), (pa_ref, pb_ref)
    n_far = [max(s0s[c] - QB, 0) // FAR_TK for c in chains]

    def qk(c, t):
        return jnp.dot(ks_ref[0, 0, pl.ds(t * FAR_TK, FAR_TK), :], q_fars[c], preferred_element_type=F32)

    def softmax_update(s, m_run):
        m_new = jnp.maximum(m_run, jnp.max(s, axis=0, keepdims=True))
        return jnp.exp2(s - m_new).astype(BF16), jnp.exp2(m_run - m_new), m_new

    n0s, s_nears = [], []
    for c in chains:
        n0 = max(s0s[c] - QB, 0)
        b0 = NSA_WINDOW - QB + n0 - (s0s[c] - QB)
        n0s.append(n0)
        s_nears.append(jnp.dot(ks_ref[0, 0, pl.ds(n0, 2 * QB), :], q_nears[c], preferred_element_type=F32)
                       + wt_ref[0, pl.ds(b0, 2 * QB), :])
    for c in chains:
        for t in range(min(2, n_far[c])):
            s_bufs[t][c] = qk(c, t)
    m_runs = [jnp.max(s, axis=0, keepdims=True) for s in s_nears]
    p_nears = [jnp.exp2(s - m).astype(BF16) for s, m in zip(s_nears, m_runs)]
    for c in chains:
        acc_ref[c] = jnp.dot(vst_ref[0, 0, :, pl.ds(n0s[c], 2 * QB)], p_nears[c],
                             preferred_element_type=F32)
    for t in range(max(n_far)):
        for c in chains:
            if t < n_far[c]:
                s_buf, p_buf = s_bufs[t % 2], p_bufs[t % 2]
                p, alpha, m_runs[c] = softmax_update(s_buf[c], m_runs[c])
                p_buf[c] = p
                if t + 2 < n_far[c]:
                    s_buf[c] = qk(c, t + 2)
                pv = jnp.dot(vst_ref[0, 0, :, pl.ds(t * FAR_TK, FAR_TK)], p_buf[c], preferred_element_type=F32)
                acc_ref[c] = acc_ref[c] * alpha + pv

    for c in chains:
        a_c = acc_ref[c]
        o_sel = a_c[:HEAD_DIM] / a_c[HEAD_DIM:HEAD_DIM + 1]
        o = gate(c, 0) * o_cmps[c] + gate(c, 1) * o_sel + ow_ref[c]
        o_ref[0, c * QB:(c + 1) * QB, :] = _to_token_rows(o, QB).astype(o_ref.dtype)


def _nsa_attention(qT, gatesT, kc_aug, vcT, ks_aug, vsT, kw_aug, vwT, ovl, cband, wtable, whalf):
    B, G, nqb = qT.shape[:3]
    QB, LQ, C = NSA_QB, HPG * NSA_QB, NSA_QPS
    nc, S = kc_aug.shape[2], ks_aug.shape[2]
    per_g = lambda a: pl.BlockSpec((1,) + a.shape[1:], lambda g, b: (g,) + (0,) * (a.ndim - 1))
    rows = lambda a, n: pl.BlockSpec((1, 1, n, a.shape[3]), lambda g, b: (b, g, 0, 0))
    cols = lambda a, n: pl.BlockSpec((1, 1, a.shape[2], n), lambda g, b: (b, g, 0, 0))
    out = None
    for step in range(nqb // C):
        seen = (step + 1) * C * QB
        held = min(S, max(seen, 2 * NSA_WINDOW))
        ncr = min(nc, -(-(seen // CMP_STRIDE) // 128) * 128)
        ovl_r = ovl[:, :ncr]
        in_specs = [pl.BlockSpec((1, 1, C, AUG, LQ), lambda g, b, s=step: (b, g, s, 0, 0)),
                    pl.BlockSpec((1, C, N_GATE, QB), lambda g, b, s=step: (b, s, 0, 0)),
                    rows(kc_aug, ncr), cols(vcT, ncr), rows(ks_aug, held), cols(vsT, held),
                    rows(kw_aug, held), cols(vwT, held), pl.BlockSpec(ovl_r.shape, lambda g, b: (0, 0)),
                    per_g(cband), per_g(wtable), per_g(whalf)]
        args = [qT, gatesT, kc_aug, vcT, ks_aug, vsT, kw_aug, vwT, ovl_r, cband, wtable, whalf]
        aliases = {}
        if out is not None:
            in_specs.append(pl.BlockSpec(memory_space=pl.ANY))
            args.append(out)
            aliases = {len(args) - 1: 0}
        out = pl.pallas_call(
            functools.partial(_nsa_kernel, nc=nc, step=step),
            grid=(G, B),
            in_specs=in_specs,
            out_specs=pl.BlockSpec((1, C * QB, HPG * HEAD_DIM), lambda g, b, s=step: (b, s, g)),
            out_shape=jax.ShapeDtypeStruct((B, nqb * QB, Q_WIDTH), BF16),
            scratch_shapes=[pltpu.VMEM((C, V_ROWS, LQ), F32), pltpu.VMEM((C, HEAD_DIM, LQ), F32),
                            pltpu.VMEM((C, FAR_TK, LQ), F32), pltpu.VMEM((C, FAR_TK, LQ), F32),
                            pltpu.VMEM((C, FAR_TK, LQ), BF16), pltpu.VMEM((C, FAR_TK, LQ), BF16)],
            input_output_aliases=aliases,
            compiler_params=_cparams(("arbitrary", "arbitrary")),
            name=f"nsa_attention_{step}",
        )(*args)
    return out


def _swa_kernel(q_ref, k_ref, vt_ref, band_ref, sink_ref, o_ref):
    QB = SWA_QB
    sink = sink_ref[0, 0:1]
    starts, scores = [], []
    for i in range(SWA_QPS):
        s0 = (pl.program_id(2) * SWA_QPS + i) * QB
        n0 = pl.multiple_of(jnp.maximum(s0 - QB, 0), QB)
        b0 = pl.multiple_of(n0 - (s0 - QB), QB)
        starts.append(n0)
        scores.append(jnp.dot(k_ref[0, 0, pl.ds(n0, 2 * QB), :], q_ref[0, 0, i], preferred_element_type=F32)
                      + band_ref[0, pl.ds(b0, 2 * QB), :])
    maxes = [jnp.maximum(jnp.max(s, axis=0, keepdims=True), sink) for s in scores]
    probs = [jnp.exp2(s - m).astype(BF16) for s, m in zip(scores, maxes)]
    accs = [jnp.dot(vt_ref[0, 0, :, pl.ds(n0, 2 * QB)], p, preferred_element_type=F32)
            for n0, p in zip(starts, probs)]
    for i, (acc, m) in enumerate(zip(accs, maxes)):
        denom = acc[HEAD_DIM:HEAD_DIM + 1] + jnp.exp2(sink - m)
        o_ref[0, i * QB:(i + 1) * QB, :] = _to_token_rows(acc[:HEAD_DIM] / denom, QB).astype(o_ref.dtype)


def _swa_attention(qT, k_aug, vT, band, sink):
    B, G, nqb = qT.shape[:3]
    QB, LQ = SWA_QB, HPG * SWA_QB
    per_bg = lambda a: pl.BlockSpec((1, 1) + a.shape[2:], lambda b, g, q: (b, g) + (0,) * (a.ndim - 2))
    per_g = lambda a: pl.BlockSpec((1,) + a.shape[1:], lambda b, g, q: (g,) + (0,) * (a.ndim - 1))
    return pl.pallas_call(
        _swa_kernel,
        grid=(B, G, nqb // SWA_QPS),
        in_specs=[pl.BlockSpec((1, 1, SWA_QPS, AUG, LQ), lambda b, g, q: (b, g, q, 0, 0)),
                  per_bg(k_aug), per_bg(vT), per_g(band), per_g(sink)],
        out_specs=pl.BlockSpec((1, SWA_QPS * QB, HPG * HEAD_DIM), lambda b, g, q: (b, q, g)),
        out_shape=jax.ShapeDtypeStruct((B, nqb * QB, Q_WIDTH), BF16),
        compiler_params=_cparams(("arbitrary", "arbitrary", "arbitrary")),
        name="swa_attention",
    )(qT, k_aug, vT, band, sink)


def _layer_norm(y, g, b):
    mu = jnp.mean(y, axis=-1, keepdims=True)
    yc = y - mu
    var = jnp.mean(yc * yc, axis=-1, keepdims=True)
    return yc * lax.rsqrt(var + LN_EPS) * g + b


def _top2_of4(a, b, c, d):
    hi1, lo1 = jnp.maximum(a, b), jnp.minimum(a, b)
    hi2, lo2 = jnp.maximum(c, d), jnp.minimum(c, d)
    return jnp.maximum(hi1, hi2) + jnp.maximum(jnp.minimum(hi1, hi2), jnp.maximum(lo1, lo2))


def _route(s, sb):
    n = EXPERTS_PER_GROUP
    score = [_top2_of4(*sb[n * r:n * r + n]) for r in range(N_GROUPS)]
    best = functools.reduce(jnp.maximum, score)
    taken = jnp.zeros_like(best) > 1.0
    in_grp = []
    for r in range(N_GROUPS):
        pick = (score[r] == best) & jnp.logical_not(taken)
        in_grp.append(pick)
        taken = taken | pick
    gates = []
    for e in range(N_EXPERTS):
        r = e // n
        ahead = jnp.zeros_like(best)
        for f in range(n * r, n * r + n):
            if f != e:
                beats = (sb[f] >= sb[e]) if f < e else (sb[f] > sb[e])
                ahead = ahead + jnp.where(beats, 1.0, 0.0)
        gates.append(jnp.where(in_grp[r] & (ahead < 2.0), s[e], 0.0))
    total = functools.reduce(jnp.add, gates)
    return [g / total for g in gates]


def _sublayer_kernel(o_ref, x_ref, wo_ref, ga_ref, lg1_ref, lb1_ref, shf_ref, scf_ref, rw_ref, rb_ref,
                     wg_ref, wu_ref, wd_ref, gf_ref, lg2_ref, lb2_ref, out_ref, x1_ref, acc_ref):
    ts = o_ref.shape[1]
    rows = [slice(k * ts // OPROJ_CHUNKS, (k + 1) * ts // OPROJ_CHUNKS) for k in range(OPROJ_CHUNKS)]
    mixes = [jnp.dot(o_ref[0, r, :], wo_ref[...], preferred_element_type=F32) for r in rows]
    hbs = []
    for r, mix in zip(rows, mixes):
        x1 = _layer_norm(ALPHA * x_ref[0, r, :] + (1.0 + ga_ref[0]) * mix, lg1_ref[...], lb1_ref[...])
        x1_ref[r, :] = x1
        hbs.append((x1 * (1.0 + scf_ref[0]) + shf_ref[0]).astype(BF16))
    logit = jnp.concatenate([lax.dot_general(rw_ref[...], hb, _NT, preferred_element_type=F32) for hb in hbs],
                            axis=1)
    aff = jax.nn.sigmoid(logit)
    biased = aff + rb_ref[...][:, 0:1]
    s = [aff[e:e + 1] for e in range(N_EXPERTS)]
    sb = [biased[e:e + 1] for e in range(N_EXPERTS)]
    gate_rows = jnp.concatenate(_route(s, sb) + [jnp.zeros((128 - N_EXPERTS, ts), F32)], axis=0)
    gates = gate_rows.T

    n = EXPERTS_PER_GROUP
    h = jnp.concatenate(hbs, axis=0)
    lane = lax.broadcasted_iota(jnp.int32, gates.shape, 1)
    for r in range(N_GROUPS):
        he = []
        for k in range(n):
            e = r * n + k
            a = jnp.dot(h, wg_ref[e], preferred_element_type=F32)
            u = jnp.dot(h, wu_ref[e], preferred_element_type=F32)
            gcol = jnp.sum(jnp.where(lane == e, gates, 0.0), axis=1, keepdims=True)
            he.append((a * jax.nn.sigmoid(a) * u * gcol).astype(BF16))
        wd = wd_ref[r * n:(r + 1) * n].reshape(n * D_EXPERT, wd_ref.shape[-1])
        ffn = jnp.dot(jnp.concatenate(he, axis=1), wd, preferred_element_type=F32)
        if r == 0:
            acc_ref[...] = ffn
        else:
            acc_ref[...] += ffn
    y = ALPHA * x1_ref[...] + (1.0 + gf_ref[0]) * acc_ref[...]
    out_ref[0] = _layer_norm(y, lg2_ref[...], lb2_ref[...])


def _sublayer(o, x, w_out, g_a, ln1_g, ln1_b, sh_f, sc_f, router_w, router_b, w_gate, w_up, w_down, g_f,
              ln2_g, ln2_b, ts=512):
    B, S, D = x.shape
    row = lambda a: a.reshape(1, D)
    per_b = pl.BlockSpec((1, 1, D), lambda b, s: (b, 0, 0))
    small = lambda a: pl.BlockSpec(a.shape, lambda b, s: (0,) * a.ndim)
    resident = lambda a: pl.BlockSpec(a.shape, lambda b, s: (0,) * a.ndim, pipeline_mode=pl.Buffered(1))
    tile = pl.BlockSpec((1, ts, D), lambda b, s: (b, s, 0))
    rwT = router_w.T.astype(BF16)
    rb = jnp.broadcast_to(router_b.reshape(N_EXPERTS, 1), (N_EXPERTS, 128))
    wo, wg, wu, wd = [w.astype(BF16) for w in (w_out, w_gate, w_up, w_down)]
    return pl.pallas_call(
        _sublayer_kernel,
        grid=(B, S // ts),
        in_specs=[tile, tile, resident(wo), per_b, small(row(ln1_g)), small(row(ln1_b)), per_b, per_b,
                  small(rwT), small(rb), resident(wg), resident(wu), resident(wd), per_b,
                  small(row(ln2_g)), small(row(ln2_b))],
        out_specs=tile,
        out_shape=jax.ShapeDtypeStruct((B, S, D), F32),
        scratch_shapes=[pltpu.VMEM((ts, D), F32), pltpu.VMEM((ts, D), F32)],
        compiler_params=_cparams(("arbitrary", "arbitrary")),
        name="oproj_moe",
    )(o, x, wo, g_a.reshape(B, 1, D), row(ln1_g), row(ln1_b), sh_f.reshape(B, 1, D), sc_f.reshape(B, 1, D),
      rwT, rb, wg, wu, wd, g_f.reshape(B, 1, D), row(ln2_g), row(ln2_b))


def _values_t(v):
    B, G, n, _ = v.shape
    return jnp.concatenate([v.transpose(0, 1, 3, 2), jnp.ones((B, G, V_ROWS - HEAD_DIM, n), v.dtype)], axis=2)


def _nsa_layer_attention(x, sh, sc, w_in, k_w1, k_w2, v_w1, v_w2, k_pos, v_pos, fb):
    B, S, _ = x.shape
    nc, n_slc = S // CMP_STRIDE, S // SLC_LEN
    qT, gatesT, ks_aug, vsT, kw_aug, vwT, ck, cv = _nsa_proj(x, sh, sc, w_in)
    merge = lambda t: t.reshape(B, N_KV * nc, CMP_STRIDE * HEAD_DIM)
    k_cmp = _compress(merge(ck), k_w1, k_w2, k_pos)
    v_cmp = _compress(merge(cv), v_w1, v_w2, v_pos)
    kc_aug = jnp.pad(k_cmp.astype(BF16), ((0, 0), (0, 0), (0, 0), (0, MASK_LANES)))
    vcT = _values_t(v_cmp.astype(BF16))

    c_start = np.arange(nc)[None, :] * CMP_STRIDE
    s_start = np.arange(MASK_LANES)[:, None] * SLC_LEN
    ovl = ((c_start < s_start + SLC_LEN) & (c_start + CMP_LEN > s_start)
           & (np.arange(nc)[None, :] < nc - 1) & (np.arange(MASK_LANES)[:, None] < n_slc))
    ovl = np.concatenate([np.zeros((HEAD_DIM, nc)), ovl.astype(np.float64)], axis=0).astype(np.float32)

    return _nsa_attention(qT, gatesT, kc_aug, vcT, ks_aug, vsT, kw_aug, vwT, jnp.asarray(ovl, BF16),
                          _cmp_band(fb, nc, NSA_QB), _window_table(fb, NSA_QB), _window_table(fb, WIN_QB))


def _swa_layer_attention(x, sh, sc, w_q, w_kv, sinks, rel_bias, fb):
    qT, k_aug, vT = _swa_proj(x, sh, sc, w_q, w_kv)
    sink = (sinks.astype(F32) - rel_bias.astype(F32)[N_BUCKETS - 1]) * LOG2E
    sink = jnp.broadcast_to(sink.reshape(N_KV, 1, HPG, 1), (N_KV, 8, HPG, SWA_QB))
    sink = sink.reshape(N_KV, 8, HPG * SWA_QB)
    band = jnp.concatenate([_near_band(fb, SWA_WINDOW, SWA_QB), _masked_rows(SWA_QB, SWA_QB)], axis=1)
    return _swa_attention(qT, k_aug, vT, band, sink)


def kernel(x, c, nsa_w_in, cmp_k_w1, cmp_k_w2, cmp_v_w1, cmp_v_w2, cmp_k_pos, cmp_v_pos, nsa_w_out,
           swa_w_q, swa_sinks, swa_w_out, shared_w_kv, rel_bias, router_w, router_b, moe_w_gate,
           moe_w_up, moe_w_down, ada_w, ada_b, ln_g, ln_b):
    B, S, D = x.shape
    ada = _ada(c, ada_w, ada_b)
    fb = _shifted_bias(rel_bias)
    for layer in range(DEPTH):
        sh_a, sc_a, g_a, sh_f, sc_f, g_f = [ada[layer, :, i * D:(i + 1) * D] for i in range(6)]
        if layer == 0:
            o = _nsa_layer_attention(x, sh_a, sc_a, nsa_w_in[0], cmp_k_w1[0], cmp_k_w2[0], cmp_v_w1[0],
                                     cmp_v_w2[0], cmp_k_pos[0], cmp_v_pos[0], fb)
            w_out = nsa_w_out[0]
        else:
            o = _swa_layer_attention(x, sh_a, sc_a, swa_w_q[0], shared_w_kv, swa_sinks[0], rel_bias, fb)
            w_out = swa_w_out[0]
        x = _sublayer(o, x, w_out, g_a, ln_g[layer, 0], ln_b[layer, 0], sh_f, sc_f, router_w, router_b,
                      moe_w_gate[layer], moe_w_up[layer], moe_w_down[layer], g_f, ln_g[layer, 1], ln_b[layer, 1])
    return x
```

```python
import functools
import math

import numpy as np
import jax
import jax.numpy as jnp
from jax import lax
from jax.experimental import pallas as pl
from jax.experimental.pallas import tpu as pltpu

F32 = jnp.float32
BF16 = jnp.bfloat16

D_MODEL = 1024
HEAD_DIM = 64
N_HEADS = 16
N_KV = 4
HPG = 4
Q_WIDTH = N_HEADS * HEAD_DIM
KV_WIDTH = N_KV * HEAD_DIM
CMP_LEN = 32
CMP_STRIDE = 16
CMP_HIDDEN = 256
SLC_LEN = 64
N_SELECT = 16
NSA_WINDOW = 512
SWA_WINDOW = 128
N_BUCKETS = 32
MAX_DISTANCE = 128
N_EXPERTS = 16
N_GROUPS = 4
EXPERTS_PER_GROUP = 4
D_EXPERT = 256
DEPTH = 2
ALPHA = (2.0 * DEPTH) ** 0.25
LN_EPS = 1e-5
NEG = -1e30
ATTN_SCALE = HEAD_DIM ** -0.5
LOG2E = math.log2(math.e)
Q_SCALE = ATTN_SCALE * LOG2E

NSA_QB = 256
NSA_QPS = 2
WIN_QB = 128
SWA_QB = 128
SWA_QPS = 16
AUG = 128
MASK_LANES = AUG - HEAD_DIM
V_ROWS = HEAD_DIM + 16
CMP_LEAD = 16
FAR_TK = 256
OPROJ_CHUNKS = 4
N_GATE = 3 * N_HEADS
VMEM_LIMIT = 56 * 1024 * 1024

_NT = (((1,), (1,)), ((), ()))


def _cparams(sem):
    return pltpu.CompilerParams(dimension_semantics=sem, vmem_limit_bytes=VMEM_LIMIT)


def _ada_kernel(c_ref, w_ref, b_ref, o_ref):
    c = c_ref[...]
    cond = c * jax.nn.sigmoid(c)
    o_ref[0] = jnp.dot(cond, w_ref[0], preferred_element_type=F32,
                       precision=lax.Precision.HIGHEST) + b_ref[0]


def _ada(c, ada_w, ada_b):
    B, D = c.shape
    n6 = ada_w.shape[-1]
    tn = 1536
    return pl.pallas_call(
        _ada_kernel,
        grid=(DEPTH, n6 // tn),
        in_specs=[pl.BlockSpec((B, D), lambda l, n: (0, 0)),
                  pl.BlockSpec((1, D, tn), lambda l, n: (l, 0, n)),
                  pl.BlockSpec((1, 1, tn), lambda l, n: (l, 0, n))],
        out_specs=pl.BlockSpec((1, B, tn), lambda l, n: (l, 0, n)),
        out_shape=jax.ShapeDtypeStruct((DEPTH, B, n6), F32),
        compiler_params=_cparams(("arbitrary", "arbitrary")),
        name="ada",
    )(c, ada_w, ada_b.reshape(DEPTH, 1, n6))


def _store_q_blocks(q_ref, qT, ts, qb):
    zeros = jnp.zeros((MASK_LANES, HPG * qb), BF16)
    for g in range(N_KV):
        for k in range(ts // qb):
            for j in range(HPG):
                r0 = (g * HPG + j) * HEAD_DIM
                q_ref[0, g, k, 0:HEAD_DIM, j * qb:(j + 1) * qb] = (
                    qT[r0:r0 + HEAD_DIM, k * qb:(k + 1) * qb].astype(BF16))
            q_ref[0, g, k, HEAD_DIM:AUG, :] = zeros


def _store_vt(vt_ref, vT, ts):
    ones = jnp.ones((V_ROWS - HEAD_DIM, ts), BF16)
    for g in range(N_KV):
        vt_ref[0, g, 0:HEAD_DIM, :] = vT[g * HEAD_DIM:(g + 1) * HEAD_DIM].astype(BF16)
        vt_ref[0, g, HEAD_DIM:V_ROWS, :] = ones


def _nsa_proj_kernel(x_ref, sh_ref, sc_ref, wt_ref, ws_ref, q_ref, gt_ref, ks_ref, vst_ref, kw_ref,
                     vwt_ref, ck_ref, cv_ref, cmp_scr, *, ts):
    s = pl.program_id(1)
    h = (x_ref[0] * (1.0 + sc_ref[0]) + sh_ref[0]).astype(BF16)
    yT = lax.dot_general(wt_ref[...], h, _NT, preferred_element_type=F32)
    y = jnp.dot(h, ws_ref[...], preferred_element_type=F32)
    _store_q_blocks(q_ref, yT[:Q_WIDTH], ts, NSA_QB)
    _store_vt(vst_ref, yT[Q_WIDTH:Q_WIDTH + KV_WIDTH], ts)
    _store_vt(vwt_ref, yT[Q_WIDTH + KV_WIDTH:Q_WIDTH + 2 * KV_WIDTH], ts)
    gates = jax.nn.sigmoid(yT[Q_WIDTH + 2 * KV_WIDTH:])
    for k in range(ts // NSA_QB):
        gt_ref[0, k] = gates[:, k * NSA_QB:(k + 1) * NSA_QB]
    lane = lax.broadcasted_iota(jnp.int32, (ts, AUG), 1)
    tok = s * ts + lax.broadcasted_iota(jnp.int32, (ts, AUG), 0)
    onehot = (lane - HEAD_DIM) == tok // SLC_LEN
    for g in range(N_KV):
        ks_ref[0, g] = jnp.where(onehot, 1.0, y[:, g * AUG:(g + 1) * AUG]).astype(BF16)
        kw_ref[0, g] = y[:, (N_KV + g) * AUG:(N_KV + g + 1) * AUG].astype(BF16)
    base = 2 * N_KV * AUG
    for c in range(2 * KV_WIDTH // 128):
        cmp_scr[c] = y[:, base + c * 128:base + (c + 1) * 128]
    for c, dst in enumerate([ck_ref, ck_ref, cv_ref, cv_ref]):
        for r in range(CMP_STRIDE):
            piece = cmp_scr[c, pl.ds(r, ts // CMP_STRIDE, stride=CMP_STRIDE), :]
            for half in range(2):
                g = 2 * (c % 2) + half
                dst[0, g, :, r * HEAD_DIM:(r + 1) * HEAD_DIM] = (
                    piece[:, half * HEAD_DIM:(half + 1) * HEAD_DIM].astype(BF16))


def _pad_heads(w):
    D = w.shape[0]
    w = w.reshape(D, N_KV, HEAD_DIM)
    return jnp.pad(w, ((0, 0), (0, 0), (0, MASK_LANES))).reshape(D, N_KV * AUG)


def _nsa_proj(x, shift, scale, w_in, ts=512):
    B, S, D = x.shape
    nqb, lq = S // NSA_QB, HPG * NSA_QB
    cols = lambda i: w_in[:, Q_WIDTH + i * KV_WIDTH:Q_WIDTH + (i + 1) * KV_WIDTH]
    w_kc, w_vc, w_ksl, w_vsl, w_kw, w_vw = [cols(i) for i in range(6)]
    wt = jnp.concatenate([w_in[:, :Q_WIDTH] * Q_SCALE, w_vsl, w_vw,
                          w_in[:, Q_WIDTH + 6 * KV_WIDTH:]], axis=1).T.astype(BF16)
    ws = jnp.concatenate([_pad_heads(w_ksl), _pad_heads(w_kw), w_kc, w_vc], axis=1).astype(BF16)
    per_b = pl.BlockSpec((1, 1, D), lambda b, s: (b, 0, 0))
    kspec = pl.BlockSpec((1, N_KV, ts, AUG), lambda b, s: (b, 0, s, 0))
    vspec = pl.BlockSpec((1, N_KV, V_ROWS, ts), lambda b, s: (b, 0, 0, s))
    k_shape = jax.ShapeDtypeStruct((B, N_KV, S, AUG), BF16)
    v_shape = jax.ShapeDtypeStruct((B, N_KV, V_ROWS, S), BF16)
    chunk = CMP_STRIDE * HEAD_DIM
    cspec = pl.BlockSpec((1, N_KV, ts // CMP_STRIDE, chunk), lambda b, s: (b, 0, s, 0))
    c_shape = jax.ShapeDtypeStruct((B, N_KV, S // CMP_STRIDE, chunk), BF16)
    return pl.pallas_call(
        functools.partial(_nsa_proj_kernel, ts=ts),
        grid=(B, S // ts),
        in_specs=[pl.BlockSpec((1, ts, D), lambda b, s: (b, s, 0)), per_b, per_b,
                  pl.BlockSpec(wt.shape, lambda b, s: (0, 0)), pl.BlockSpec(ws.shape, lambda b, s: (0, 0))],
        out_specs=[pl.BlockSpec((1, N_KV, ts // NSA_QB, AUG, lq), lambda b, s: (b, 0, s, 0, 0)),
                   pl.BlockSpec((1, ts // NSA_QB, N_GATE, NSA_QB), lambda b, s: (b, s, 0, 0)),
                   kspec, vspec, kspec, vspec,
                   cspec, cspec],
        out_shape=[jax.ShapeDtypeStruct((B, N_KV, nqb, AUG, lq), BF16),
                   jax.ShapeDtypeStruct((B, nqb, N_GATE, NSA_QB), F32),
                   k_shape, v_shape, k_shape, v_shape,
                   c_shape, c_shape],
        scratch_shapes=[pltpu.VMEM((2 * KV_WIDTH // 128, ts, 128), F32)],
        compiler_params=_cparams(("arbitrary", "arbitrary")),
        name="nsa_proj",
    )(x, shift.reshape(B, 1, D), scale.reshape(B, 1, D), wt, ws)


def _swa_proj_kernel(x_ref, sh_ref, sc_ref, wq_ref, wk_ref, wv_ref, q_ref, k_ref, vt_ref, *, ts):
    x = x_ref[0]
    h = (x * (1.0 + sc_ref[0]) + sh_ref[0]).astype(BF16)
    xb = x.astype(BF16)
    _store_q_blocks(q_ref, lax.dot_general(wq_ref[...], h, _NT, preferred_element_type=F32), ts, SWA_QB)
    _store_vt(vt_ref, lax.dot_general(wv_ref[...], xb, _NT, preferred_element_type=F32), ts)
    k = jnp.dot(xb, wk_ref[...], preferred_element_type=F32)
    for g in range(N_KV):
        k_ref[0, g] = k[:, g * AUG:(g + 1) * AUG].astype(BF16)


def _swa_proj(x, shift, scale, w_q, w_kv, ts=512):
    B, S, D = x.shape
    lq = HPG * SWA_QB
    wq = (w_q * Q_SCALE).T.astype(BF16)
    wk = _pad_heads(w_kv[:, :KV_WIDTH]).astype(BF16)
    wv = w_kv[:, KV_WIDTH:].T.astype(BF16)
    per_b = pl.BlockSpec((1, 1, D), lambda b, s: (b, 0, 0))
    full = lambda a: pl.BlockSpec(a.shape, lambda b, s: (0, 0))
    return pl.pallas_call(
        functools.partial(_swa_proj_kernel, ts=ts),
        grid=(B, S // ts),
        in_specs=[pl.BlockSpec((1, ts, D), lambda b, s: (b, s, 0)), per_b, per_b, full(wq), full(wk), full(wv)],
        out_specs=[pl.BlockSpec((1, N_KV, ts // SWA_QB, AUG, lq), lambda b, s: (b, 0, s, 0, 0)),
                   pl.BlockSpec((1, N_KV, ts, AUG), lambda b, s: (b, 0, s, 0)),
                   pl.BlockSpec((1, N_KV, V_ROWS, ts), lambda b, s: (b, 0, 0, s))],
        out_shape=[jax.ShapeDtypeStruct((B, N_KV, S // SWA_QB, AUG, lq), BF16),
                   jax.ShapeDtypeStruct((B, N_KV, S, AUG), BF16),
                   jax.ShapeDtypeStruct((B, N_KV, V_ROWS, S), BF16)],
        compiler_params=_cparams(("arbitrary", "arbitrary")),
        name="swa_proj",
    )(x, shift.reshape(B, 1, D), scale.reshape(B, 1, D), wq, wk, wv)


def _gelu_tanh(x):
    return 0.5 * x * (1.0 + jnp.tanh(math.sqrt(2.0 / math.pi) * (x + 0.044715 * (x * x * x))))


def _compress_kernel(ch_ref, w1_ref, w2_ref, pos_ref, o_ref, *, nc):
    half = CMP_STRIDE * HEAD_DIM
    ch = ch_ref[0]
    w1 = w1_ref[...]
    top = jnp.dot(ch, w1[:half], preferred_element_type=F32)
    bot = jnp.dot(ch, w1[half:], preferred_element_type=F32)
    posw = jnp.dot(pos_ref[...], w1, preferred_element_type=F32)[0:1]
    w2 = w2_ref[...]
    for g in range(N_KV):
        bot_g = pltpu.roll(bot[g * nc:(g + 1) * nc], nc - 1, 0)
        pre = top[g * nc:(g + 1) * nc] + bot_g + posw
        o_ref[0, g] = jnp.dot(_gelu_tanh(pre).astype(BF16), w2, preferred_element_type=F32)


def _compress(chunks, w1, w2, pos):
    B, gn, ck = chunks.shape
    nc = gn // N_KV
    pos8 = jnp.broadcast_to(pos.reshape(1, CMP_LEN * HEAD_DIM), (8, CMP_LEN * HEAD_DIM)).astype(BF16)
    return pl.pallas_call(
        functools.partial(_compress_kernel, nc=nc),
        grid=(B,),
        in_specs=[pl.BlockSpec((1, gn, ck), lambda b: (b, 0, 0)),
                  pl.BlockSpec(w1.shape, lambda b: (0, 0)),
                  pl.BlockSpec(w2.shape, lambda b: (0, 0)),
                  pl.BlockSpec(pos8.shape, lambda b: (0, 0))],
        out_specs=pl.BlockSpec((1, N_KV, nc, HEAD_DIM), lambda b: (b, 0, 0, 0)),
        out_shape=jax.ShapeDtypeStruct((B, N_KV, nc, HEAD_DIM), F32),
        compiler_params=_cparams(("arbitrary",)),
        name="compress",
    )(chunks, w1.astype(BF16), w2.astype(BF16), pos8)


def _bucket_np(d):
    d = np.maximum(np.asarray(d, np.int64), 0)
    max_exact = N_BUCKETS // 2
    large = max_exact + (np.log(np.maximum(d, 1).astype(np.float32) / np.float32(max_exact))
                         / np.float32(math.log(MAX_DISTANCE / max_exact))
                         * np.float32(N_BUCKETS - max_exact)).astype(np.int32)
    large = np.minimum(large, N_BUCKETS - 1)
    return np.where(d < max_exact, d, large).astype(np.int32)


def _shifted_bias(rel_table):
    t = rel_table.astype(F32)
    return ((t[_bucket_np(np.arange(MAX_DISTANCE))] - t[N_BUCKETS - 1][None, :]) * LOG2E).T


def _to_group_lanes(m):
    H, K, Q = m.shape
    return m.reshape(N_KV, HPG, K, Q).transpose(0, 2, 1, 3).reshape(N_KV, K, HPG * Q)


def _bias_of_dist(fb, dist, window):
    ok = (dist >= 0) & (dist < window)
    idx = np.clip(dist, 0, MAX_DISTANCE - 1)
    return jnp.where(ok[None], jnp.where((dist >= MAX_DISTANCE)[None], 0.0, fb[:, idx]), NEG)


def _toeplitz_pair(fb, window):
    H, t = fb.shape[0], MAX_DISTANCE
    nv = 3 * t - 1
    v = _bias_of_dist(fb, np.arange(nv) - (t - 1), window)
    rows = 2 * t
    flat = jnp.tile(v, (1, rows + 1))[:, :rows * (nv + 1)]
    return flat.reshape(H, rows, nv + 1)[:, ::-1, :t]


def _near_band(fb, window, qb):
    H, t = fb.shape[0], MAX_DISTANCE
    nb = qb // t
    assert qb % t == 0 and (nb == 1 or window >= 2 * qb)
    pair = _toeplitz_pair(fb, window)
    blocks = {1: pair[:, :t], 0: pair[:, t:]}
    zero, masked = jnp.zeros((H, t, t), F32), jnp.full((H, t, t), NEG, F32)
    rows = []
    for a in range(2 * nb):
        row = [blocks.get(b + nb - a, zero if b + nb - a > 1 else masked) for b in range(nb)]
        rows.append(jnp.concatenate(row, axis=2))
    return _to_group_lanes(jnp.concatenate(rows, axis=1))


def _masked_rows(n, qb):
    return jnp.full((N_KV, n, HPG * qb), NEG, F32)


def _window_table(fb, qb):
    kk = np.arange(qb)[:, None]
    qi = np.arange(HPG * qb)[None, :] % qb
    edge = np.where(kk > qi, 0.0, NEG).astype(np.float32)
    edge = jnp.broadcast_to(jnp.asarray(edge), (N_KV, qb, HPG * qb))
    mid = jnp.zeros((N_KV, NSA_WINDOW - 2 * qb, HPG * qb), F32)
    return jnp.concatenate([edge, mid, _near_band(fb, NSA_WINDOW, qb), _masked_rows(NSA_WINDOW, qb)], axis=1)


def _cmp_band_rows(qb):
    return CMP_LEAD + qb // CMP_STRIDE


def _cmp_band(fb, nc, qb):
    m = np.arange(_cmp_band_rows(qb))[:, None]
    i = np.arange(qb)[None, :]
    dist = i + CMP_STRIDE * CMP_LEAD - CMP_STRIDE * m - (CMP_LEN - 1)
    band = _to_group_lanes(_bias_of_dist(fb, dist, 1 << 30))
    return jnp.concatenate([jnp.zeros((N_KV, nc, HPG * qb), F32), band, _masked_rows(nc, qb)], axis=1)


def _to_token_rows(oT, qb):
    pair = lambda a: jnp.concatenate([oT[:, a * qb:(a + 1) * qb], oT[:, (a + 1) * qb:(a + 2) * qb]], axis=0)
    return jnp.concatenate([pair(0).T, pair(2).T], axis=1)


def _nsa_kernel(q_ref, gt_ref, kc_ref, vct_ref, ks_ref, vst_ref, kw_ref, vwt_ref, ovl_ref, cb_ref,
                wt_ref, wh_ref, *rest, nc, step):
    o_ref, acc_ref, ow_ref, sa_ref, sb_ref, pa_ref, pb_ref = rest[-7:]
    QB, LQ, C = NSA_QB, HPG * NSA_QB, NSA_QPS
    g = pl.program_id(0)
    chains = range(C)
    ncr = kc_ref.shape[2]
    qbs = [step * C + c for c in chains]
    s0s = [qb * QB for qb in qbs]
    nblk = (step + 1) * C * QB // SLC_LEN
    qTs = [q_ref[0, 0, c] for c in chains]

    def gate(c, branch):
        rows = [gt_ref[0, c, pl.ds(branch * N_HEADS + g * HPG + j, 1), :] for j in range(HPG)]
        return jnp.concatenate(rows, axis=1)

    scs, sws, w0s = [], [], []
    for c in chains:
        cstart = nc + CMP_LEAD - (QB // CMP_STRIDE) * qbs[c]
        scs.append(jnp.dot(kc_ref[0, 0], qTs[c], preferred_element_type=F32)
                   + cb_ref[0, pl.ds(cstart, ncr), :])
    halves = [(c, b) for c in chains for b in range(QB // WIN_QB)]
    win_keys = NSA_WINDOW + WIN_QB
    for c, b in halves:
        s0h = s0s[c] + b * WIN_QB
        w0 = max(s0h - NSA_WINDOW, 0)
        t0 = w0 - (s0h - NSA_WINDOW)
        q_half = jnp.concatenate([qTs[c][:, j * QB + b * WIN_QB:j * QB + (b + 1) * WIN_QB] for j in range(HPG)],
                                 axis=1)
        w0s.append(w0)
        sws.append(jnp.dot(kw_ref[0, 0, pl.ds(w0, win_keys), :], q_half, preferred_element_type=F32)
                   + wh_ref[0, pl.ds(t0, win_keys), :])

    p_cs = []
    for c in chains:
        m_c = jnp.max(scs[c], axis=0, keepdims=True)
        e_c = jnp.exp2(scs[c] - m_c)
        l_c = jnp.sum(e_c, axis=0, keepdims=True)
        p_cs.append(e_c * jnp.where(m_c > 0.1 * NEG, 1.0 / l_c, 0.0))
    o_cmps, imps = [], []
    for c in chains:
        o_cmps.append(jnp.dot(vct_ref[0, 0], p_cs[c].astype(BF16), preferred_element_type=F32)[:HEAD_DIM])
        p_sum = functools.reduce(jnp.add, [p_cs[c][:, j * QB:(j + 1) * QB] for j in range(HPG)])
        p_hi = p_sum.astype(BF16)
        rest1 = p_sum - p_hi.astype(F32)
        p_mid = rest1.astype(BF16)
        p_lo = (rest1 - p_mid.astype(F32)).astype(BF16)
        imp = functools.reduce(jnp.add, [jnp.dot(ovl_ref[...], part, preferred_element_type=F32)
                                         for part in (p_hi, p_mid, p_lo)])
        imps.append(imp[HEAD_DIM:HEAD_DIM + nblk])

    p_ws = []
    for h in range(len(halves)):
        m_w = jnp.max(sws[h], axis=0, keepdims=True)
        p_ws.append(jnp.exp2(sws[h] - m_w).astype(BF16))
    acc_ws = [jnp.dot(vwt_ref[0, 0, :, pl.ds(w0s[h], win_keys)], p_ws[h], preferred_element_type=F32)
              for h in range(len(halves))]

    blk = lax.broadcasted_iota(jnp.int32, (nblk, QB), 0)
    sub = lax.broadcasted_iota(jnp.int32, (8, QB), 0)
    unseen = [jnp.full((MASK_LANES - nblk, LQ), NEG, BF16)] if nblk < MASK_LANES else []
    q_nears, q_fars = [], []
    for c in chains:
        tq = s0s[c] + lax.broadcasted_iota(jnp.int32, (nblk, QB), 1)
        cur = tq // SLC_LEN
        forced = (blk == 0) | (blk == cur) | (blk == cur - 1)
        future = blk > cur
        val = jnp.where(forced, 1e9, jnp.where(future, -1e9, imps[c]))
        vals = [val[8 * a:8 * a + 8] for a in range(nblk // 8)]
        cnts = [jnp.zeros((8, QB), F32) for _ in range(nblk // 8)]
        for jp in range(nblk):
            ap, r = divmod(jp, 8)
            rv = jnp.broadcast_to(vals[ap][r:r + 1], (8, QB))
            for a in range(nblk // 8):
                gt = jnp.where(rv > vals[a], 1.0, 0.0)
                ge = jnp.where(rv >= vals[a], 1.0, 0.0)
                if a < ap:
                    inc = gt
                elif a > ap:
                    inc = ge
                else:
                    inc = jnp.where(sub > r, ge, gt)
                cnts[a] = cnts[a] + inc
        cnt = jnp.concatenate(cnts, axis=0)
        allowed = (cnt < float(N_SELECT)) & (blk <= cur)
        near_blk = blk >= (s0s[c] - QB) // SLC_LEN
        neg_near = jnp.where(allowed, 0.0, NEG).astype(BF16)
        neg_far = jnp.where(allowed & jnp.logical_not(near_blk), 0.0, NEG).astype(BF16)
        q_nears.append(jnp.concatenate([qTs[c][:HEAD_DIM], jnp.tile(neg_near, (1, HPG))] + unseen, axis=0))
        q_fars.append(jnp.concatenate([qTs[c][:HEAD_DIM], jnp.tile(neg_far, (1, HPG))] + unseen, axis=0))
    for c in chains:
        o_halves = [acc_ws[h][:HEAD_DIM] / acc_ws[h][HEAD_DIM:HEAD_DIM + 1]
                    for h, (ch, _) in enumerate(halves) if ch == c]
        o_win = jnp.concatenate([o_h[:, j * WIN_QB:(j + 1) * WIN_QB] for j in range(HPG) for o_h in o_halves],
                                axis=1)
        ow_ref[c] = gate(c, 2) * o_win

    s_bufs, p_bufs = (sa_ref, sb_ref), (pa_ref, pb_ref)
    n_far = [max(s0s[c] - QB, 0) // FAR_TK for c in chains]

    def qk(c, t):
        return jnp.dot(ks_ref[0, 0, pl.ds(t * FAR_TK, FAR_TK), :], q_fars[c], preferred_element_type=F32)

    def softmax_update(s, m_run):
        m_new = jnp.maximum(m_run, jnp.max(s, axis=0, keepdims=True))
        return jnp.exp2(s - m_new).astype(BF16), jnp.exp2(m_run - m_new), m_new

    n0s, s_nears = [], []
    for c in chains:
        n0 = max(s0s[c] - QB, 0)
        b0 = NSA_WINDOW - QB + n0 - (s0s[c] - QB)
        n0s.append(n0)
        s_nears.append(jnp.dot(ks_ref[0, 0, pl.ds(n0, 2 * QB), :], q_nears[c], preferred_element_type=F32)
                       + wt_ref[0, pl.ds(b0, 2 * QB), :])
    for c in chains:
        for t in range(min(2, n_far[c])):
            s_bufs[t][c] = qk(c, t)
    m_runs = [jnp.max(s, axis=0, keepdims=True) for s in s_nears]
    p_nears = [jnp.exp2(s - m).astype(BF16) for s, m in zip(s_nears, m_runs)]
    for c in chains:
        acc_ref[c] = jnp.dot(vst_ref[0, 0, :, pl.ds(n0s[c], 2 * QB)], p_nears[c],
                             preferred_element_type=F32)
    for t in range(max(n_far)):
        for c in chains:
            if t < n_far[c]:
                s_buf, p_buf = s_bufs[t % 2], p_bufs[t % 2]
                p, alpha, m_runs[c] = softmax_update(s_buf[c], m_runs[c])
                p_buf[c] = p
                if t + 2 < n_far[c]:
                    s_buf[c] = qk(c, t + 2)
                pv = jnp.dot(vst_ref[0, 0, :, pl.ds(t * FAR_TK, FAR_TK)], p_buf[c], preferred_element_type=F32)
                acc_ref[c] = acc_ref[c] * alpha + pv

    for c in chains:
        a_c = acc_ref[c]
        o_sel = a_c[:HEAD_DIM] / a_c[HEAD_DIM:HEAD_DIM + 1]
        o = gate(c, 0) * o_cmps[c] + gate(c, 1) * o_sel + ow_ref[c]
        o_ref[0, c * QB:(c + 1) * QB, :] = _to_token_rows(o, QB).astype(o_ref.dtype)


def _nsa_attention(qT, gatesT, kc_aug, vcT, ks_aug, vsT, kw_aug, vwT, ovl, cband, wtable, whalf):
    B, G, nqb = qT.shape[:3]
    QB, LQ, C = NSA_QB, HPG * NSA_QB, NSA_QPS
    nc, S = kc_aug.shape[2], ks_aug.shape[2]
    per_g = lambda a: pl.BlockSpec((1,) + a.shape[1:], lambda g, b: (g,) + (0,) * (a.ndim - 1))
    rows = lambda a, n: pl.BlockSpec((1, 1, n, a.shape[3]), lambda g, b: (b, g, 0, 0))
    cols = lambda a, n: pl.BlockSpec((1, 1, a.shape[2], n), lambda g, b: (b, g, 0, 0))
    out = jnp.zeros((B, nqb * QB, Q_WIDTH), BF16)
    for step in range(nqb // C):
        seen = (step + 1) * C * QB
        held = min(S, max(seen, 2 * NSA_WINDOW))
        ncr = min(nc, -(-(seen // CMP_STRIDE) // 128) * 128)
        ovl_r = ovl[:, :ncr]
        in_specs = [pl.BlockSpec((1, 1, C, AUG, LQ), lambda g, b, s=step: (b, g, s, 0, 0)),
                    pl.BlockSpec((1, C, N_GATE, QB), lambda g, b, s=step: (b, s, 0, 0)),
                    rows(kc_aug, ncr), cols(vcT, ncr), rows(ks_aug, held), cols(vsT, held),
                    rows(kw_aug, held), cols(vwT, held), pl.BlockSpec(ovl_r.shape, lambda g, b: (0, 0)),
                    per_g(cband), per_g(wtable), per_g(whalf)]
        in_specs.append(pl.BlockSpec(memory_space=pl.ANY))
        args = [qT, gatesT, kc_aug, vcT, ks_aug, vsT, kw_aug, vwT, ovl_r, cband, wtable, whalf, out]
        aliases = {len(args) - 1: 0}
        out = pl.pallas_call(
            functools.partial(_nsa_kernel, nc=nc, step=step),
            grid=(G, B),
            in_specs=in_specs,
            out_specs=pl.BlockSpec((1, C * QB, HPG * HEAD_DIM), lambda g, b, s=step: (b, s, g)),
            out_shape=jax.ShapeDtypeStruct((B, nqb * QB, Q_WIDTH), BF16),
            scratch_shapes=[pltpu.VMEM((C, V_ROWS, LQ), F32), pltpu.VMEM((C, HEAD_DIM, LQ), F32),
                            pltpu.VMEM((C, FAR_TK, LQ), F32), pltpu.VMEM((C, FAR_TK, LQ), F32),
                            pltpu.VMEM((C, FAR_TK, LQ), BF16), pltpu.VMEM((C, FAR_TK, LQ), BF16)],
            input_output_aliases=aliases,
            compiler_params=_cparams(("arbitrary", "arbitrary")),
            name=f"nsa_attention_{step}",
        )(*args)
    return out


def _swa_kernel(q_ref, k_ref, vt_ref, band_ref, sink_ref, o_ref):
    QB = SWA_QB
    sink = sink_ref[0, 0:1]
    starts, scores = [], []
    for i in range(SWA_QPS):
        s0 = (pl.program_id(2) * SWA_QPS + i) * QB
        n0 = pl.multiple_of(jnp.maximum(s0 - QB, 0), QB)
        b0 = pl.multiple_of(n0 - (s0 - QB), QB)
        starts.append(n0)
        scores.append(jnp.dot(k_ref[0, 0, pl.ds(n0, 2 * QB), :], q_ref[0, 0, i], preferred_element_type=F32)
                      + band_ref[0, pl.ds(b0, 2 * QB), :])
    maxes = [jnp.maximum(jnp.max(s, axis=0, keepdims=True), sink) for s in scores]
    probs = [jnp.exp2(s - m).astype(BF16) for s, m in zip(scores, maxes)]
    accs = [jnp.dot(vt_ref[0, 0, :, pl.ds(n0, 2 * QB)], p, preferred_element_type=F32)
            for n0, p in zip(starts, probs)]
    for i, (acc, m) in enumerate(zip(accs, maxes)):
        denom = acc[HEAD_DIM:HEAD_DIM + 1] + jnp.exp2(sink - m)
        o_ref[0, i * QB:(i + 1) * QB, :] = _to_token_rows(acc[:HEAD_DIM] / denom, QB).astype(o_ref.dtype)


def _swa_attention(qT, k_aug, vT, band, sink):
    B, G, nqb = qT.shape[:3]
    QB, LQ = SWA_QB, HPG * SWA_QB
    per_bg = lambda a: pl.BlockSpec((1, 1) + a.shape[2:], lambda b, g, q: (b, g) + (0,) * (a.ndim - 2))
    per_g = lambda a: pl.BlockSpec((1,) + a.shape[1:], lambda b, g, q: (g,) + (0,) * (a.ndim - 1))
    return pl.pallas_call(
        _swa_kernel,
        grid=(B, G, nqb // SWA_QPS),
        in_specs=[pl.BlockSpec((1, 1, SWA_QPS, AUG, LQ), lambda b, g, q: (b, g, q, 0, 0)),
                  per_bg(k_aug), per_bg(vT), per_g(band), per_g(sink)],
        out_specs=pl.BlockSpec((1, SWA_QPS * QB, HPG * HEAD_DIM), lambda b, g, q: (b, q, g)),
        out_shape=jax.ShapeDtypeStruct((B, nqb * QB, Q_WIDTH), BF16),
        compiler_params=_cparams(("arbitrary", "arbitrary", "arbitrary")),
        name="swa_attention",
    )(qT, k_aug, vT, band, sink)


def _layer_norm(y, g, b):
    mu = jnp.mean(y, axis=-1, keepdims=True)
    yc = y - mu
    var = jnp.mean(yc * yc, axis=-1, keepdims=True)
    return yc * lax.rsqrt(var + LN_EPS) * g + b


def _top2_of4(a, b, c, d):
    hi1, lo1 = jnp.maximum(a, b), jnp.minimum(a, b)
    hi2, lo2 = jnp.maximum(c, d), jnp.minimum(c, d)
    return jnp.maximum(hi1, hi2) + jnp.maximum(jnp.minimum(hi1, hi2), jnp.maximum(lo1, lo2))


def _route(s, sb):
    n = EXPERTS_PER_GROUP
    score = [_top2_of4(*sb[n * r:n * r + n]) for r in range(N_GROUPS)]
    best = functools.reduce(jnp.maximum, score)
    taken = jnp.zeros_like(best) > 1.0
    in_grp = []
    for r in range(N_GROUPS):
        pick = (score[r] == best) & jnp.logical_not(taken)
        in_grp.append(pick)
        taken = taken | pick
    gates = []
    for e in range(N_EXPERTS):
        r = e // n
        ahead = jnp.zeros_like(best)
        for f in range(n * r, n * r + n):
            if f != e:
                beats = (sb[f] >= sb[e]) if f < e else (sb[f] > sb[e])
                ahead = ahead + jnp.where(beats, 1.0, 0.0)
        gates.append(jnp.where(in_grp[r] & (ahead < 2.0), s[e], 0.0))
    total = functools.reduce(jnp.add, gates)
    return [g / total for g in gates]


def _sublayer_kernel(o_ref, x_ref, wo_ref, ga_ref, lg1_ref, lb1_ref, shf_ref, scf_ref, rw_ref, rb_ref,
                     wg_ref, wu_ref, wd_ref, gf_ref, lg2_ref, lb2_ref, out_ref, x1_ref, acc_ref):
    ts = o_ref.shape[1]
    rows = [slice(k * ts // OPROJ_CHUNKS, (k + 1) * ts // OPROJ_CHUNKS) for k in range(OPROJ_CHUNKS)]
    mixes = [jnp.dot(o_ref[0, r, :], wo_ref[...], preferred_element_type=F32) for r in rows]
    hbs = []
    for r, mix in zip(rows, mixes):
        x1 = _layer_norm(ALPHA * x_ref[0, r, :] + (1.0 + ga_ref[0]) * mix, lg1_ref[...], lb1_ref[...])
        x1_ref[r, :] = x1
        hbs.append((x1 * (1.0 + scf_ref[0]) + shf_ref[0]).astype(BF16))
    logit = jnp.concatenate([lax.dot_general(rw_ref[...], hb, _NT, preferred_element_type=F32) for hb in hbs],
                            axis=1)
    aff = jax.nn.sigmoid(logit)
    biased = aff + rb_ref[...][:, 0:1]
    s = [aff[e:e + 1] for e in range(N_EXPERTS)]
    sb = [biased[e:e + 1] for e in range(N_EXPERTS)]
    gate_rows = jnp.concatenate(_route(s, sb) + [jnp.zeros((128 - N_EXPERTS, ts), F32)], axis=0)
    gates = gate_rows.T

    n = EXPERTS_PER_GROUP
    h = jnp.concatenate(hbs, axis=0)
    lane = lax.broadcasted_iota(jnp.int32, gates.shape, 1)
    for r in range(N_GROUPS):
        he = []
        for k in range(n):
            e = r * n + k
            a = jnp.dot(h, wg_ref[e], preferred_element_type=F32)
            u = jnp.dot(h, wu_ref[e], preferred_element_type=F32)
            gcol = jnp.sum(jnp.where(lane == e, gates, 0.0), axis=1, keepdims=True)
            he.append((a * jax.nn.sigmoid(a) * u * gcol).astype(BF16))
        wd = wd_ref[r * n:(r + 1) * n].reshape(n * D_EXPERT, wd_ref.shape[-1])
        ffn = jnp.dot(jnp.concatenate(he, axis=1), wd, preferred_element_type=F32)
        if r == 0:
            acc_ref[...] = ffn
        else:
            acc_ref[...] += ffn
    y = ALPHA * x1_ref[...] + (1.0 + gf_ref[0]) * acc_ref[...]
    out_ref[0] = _layer_norm(y, lg2_ref[...], lb2_ref[...])


def _sublayer(o, x, w_out, g_a, ln1_g, ln1_b, sh_f, sc_f, router_w, router_b, w_gate, w_up, w_down, g_f,
              ln2_g, ln2_b, ts=512):
    B, S, D = x.shape
    row = lambda a: a.reshape(1, D)
    per_b = pl.BlockSpec((1, 1, D), lambda b, s: (b, 0, 0))
    small = lambda a: pl.BlockSpec(a.shape, lambda b, s: (0,) * a.ndim)
    resident = lambda a: pl.BlockSpec(a.shape, lambda b, s: (0,) * a.ndim, pipeline_mode=pl.Buffered(1))
    tile = pl.BlockSpec((1, ts, D), lambda b, s: (b, s, 0))
    rwT = router_w.T.astype(BF16)
    rb = jnp.broadcast_to(router_b.reshape(N_EXPERTS, 1), (N_EXPERTS, 128))
    wo, wg, wu, wd = [w.astype(BF16) for w in (w_out, w_gate, w_up, w_down)]
    return pl.pallas_call(
        _sublayer_kernel,
        grid=(B, S // ts),
        in_specs=[tile, tile, resident(wo), per_b, small(row(ln1_g)), small(row(ln1_b)), per_b, per_b,
                  small(rwT), small(rb), resident(wg), resident(wu), resident(wd), per_b,
                  small(row(ln2_g)), small(row(ln2_b))],
        out_specs=tile,
        out_shape=jax.ShapeDtypeStruct((B, S, D), F32),
        scratch_shapes=[pltpu.VMEM((ts, D), F32), pltpu.VMEM((ts, D), F32)],
        compiler_params=_cparams(("arbitrary", "arbitrary")),
        name="oproj_moe",
    )(o, x, wo, g_a.reshape(B, 1, D), row(ln1_g), row(ln1_b), sh_f.reshape(B, 1, D), sc_f.reshape(B, 1, D),
      rwT, rb, wg, wu, wd, g_f.reshape(B, 1, D), row(ln2_g), row(ln2_b))


def _values_t(v):
    B, G, n, _ = v.shape
    return jnp.concatenate([v.transpose(0, 1, 3, 2), jnp.ones((B, G, V_ROWS - HEAD_DIM, n), v.dtype)], axis=2)


def _nsa_layer_attention(x, sh, sc, w_in, k_w1, k_w2, v_w1, v_w2, k_pos, v_pos, fb):
    B, S, _ = x.shape
    nc, n_slc = S // CMP_STRIDE, S // SLC_LEN
    qT, gatesT, ks_aug, vsT, kw_aug, vwT, ck, cv = _nsa_proj(x, sh, sc, w_in)
    merge = lambda t: t.reshape(B, N_KV * nc, CMP_STRIDE * HEAD_DIM)
    k_cmp = _compress(merge(ck), k_w1, k_w2, k_pos)
    v_cmp = _compress(merge(cv), v_w1, v_w2, v_pos)
    kc_aug = jnp.pad(k_cmp.astype(BF16), ((0, 0), (0, 0), (0, 0), (0, MASK_LANES)))
    vcT = _values_t(v_cmp.astype(BF16))

    c_start = np.arange(nc)[None, :] * CMP_STRIDE
    s_start = np.arange(MASK_LANES)[:, None] * SLC_LEN
    ovl = ((c_start < s_start + SLC_LEN) & (c_start + CMP_LEN > s_start)
           & (np.arange(nc)[None, :] < nc - 1) & (np.arange(MASK_LANES)[:, None] < n_slc))
    ovl = np.concatenate([np.zeros((HEAD_DIM, nc)), ovl.astype(np.float64)], axis=0).astype(np.float32)

    return _nsa_attention(qT, gatesT, kc_aug, vcT, ks_aug, vsT, kw_aug, vwT, jnp.asarray(ovl, BF16),
                          _cmp_band(fb, nc, NSA_QB), _window_table(fb, NSA_QB), _window_table(fb, WIN_QB))


def _swa_layer_attention(x, sh, sc, w_q, w_kv, sinks, rel_bias, fb):
    qT, k_aug, vT = _swa_proj(x, sh, sc, w_q, w_kv)
    sink = (sinks.astype(F32) - rel_bias.astype(F32)[N_BUCKETS - 1]) * LOG2E
    sink = jnp.broadcast_to(sink.reshape(N_KV, 1, HPG, 1), (N_KV, 8, HPG, SWA_QB))
    sink = sink.reshape(N_KV, 8, HPG * SWA_QB)
    band = jnp.concatenate([_near_band(fb, SWA_WINDOW, SWA_QB), _masked_rows(SWA_QB, SWA_QB)], axis=1)
    return _swa_attention(qT, k_aug, vT, band, sink)


def kernel(x, c, nsa_w_in, cmp_k_w1, cmp_k_w2, cmp_v_w1, cmp_v_w2, cmp_k_pos, cmp_v_pos, nsa_w_out,
           swa_w_q, swa_sinks, swa_w_out, shared_w_kv, rel_bias, router_w, router_b, moe_w_gate,
           moe_w_up, moe_w_down, ada_w, ada_b, ln_g, ln_b):
    B, S, D = x.shape
    ada = _ada(c, ada_w, ada_b)
    fb = _shifted_bias(rel_bias)
    for layer in range(DEPTH):
        sh_a, sc_a, g_a, sh_f, sc_f, g_f = [ada[layer, :, i * D:(i + 1) * D] for i in range(6)]
        if layer == 0:
            o = _nsa_layer_attention(x, sh_a, sc_a, nsa_w_in[0], cmp_k_w1[0], cmp_k_w2[0], cmp_v_w1[0],
                                     cmp_v_w2[0], cmp_k_pos[0], cmp_v_pos[0], fb)
            w_out = nsa_w_out[0]
        else:
            o = _swa_layer_attention(x, sh_a, sc_a, swa_w_q[0], shared_w_kv, swa_sinks[0], rel_bias, fb)
            w_out = swa_w_out[0]
        x = _sublayer(o, x, w_out, g_a, ln_g[layer, 0], ln_b[layer, 0], sh_f, sc_f, router_w, router_b,
                      moe_w_gate[layer], moe_w_up[layer], moe_w_down[layer], g_f, ln_g[layer, 1], ln_b[layer, 1])
    return x
```
